```python
import math
import jax, jax.numpy as jnp
from jax import lax
import numpy as np

D_MODEL = 2048
BATCH = 2
SEQ = 4096
DEPTH = 1
DEC_BATCH = 32
DEC_SEQ = 4
PAST_LEN = 8192
PAGE_SIZE = 128

MIX_WIDTH = D_MODEL
ATTN_WIDTH = MIX_WIDTH // 2
POOL_WIDTH = MIX_WIDTH - ATTN_WIDTH
HEAD_DIM = 64
N_DIFF_HEADS = ATTN_WIDTH // (2 * HEAD_DIM)
V_HEAD_DIM = 2 * HEAD_DIM
POOL_WINDOWS = (2, 4, 8, 16)
N_POOL_GROUPS = len(POOL_WINDOWS)
POOL_GROUP_WIDTH = POOL_WIDTH // N_POOL_GROUPS
POOL_HIST = max(POOL_WINDOWS) - 1
N_EXPERT_GROUPS = 4
EXPERTS_PER_GROUP = 8
N_EXPERTS = N_EXPERT_GROUPS * EXPERTS_PER_GROUP
TOP_K_FINE = 2
D_EXPERT = D_MODEL // 8
ROPE_THETA = 10000.0
RMS_EPS = 1e-6
Q_BLOCK = 128
NEG_INF = -1e30

kernel_name = 'hymba_diffattn_pool_hiermoe_step'


def rms_norm(x, g):
    xf = x.astype(jnp.float32)
    y = xf * lax.rsqrt(jnp.mean(xf * xf, axis=-1, keepdims=True) + RMS_EPS) * g.astype(jnp.float32)
    return y.astype(x.dtype)


def rope(x, pos):
    half = HEAD_DIM // 2
    inv = ROPE_THETA ** (-jnp.arange(0, HEAD_DIM, 2, dtype=jnp.float32) / HEAD_DIM)
    ang = pos.astype(jnp.float32)[:, None] * inv[None, :]
    cos = jnp.cos(ang)[None, :, None, None, :]
    sin = jnp.sin(ang)[None, :, None, None, :]
    xf = x.astype(jnp.float32)
    x1, x2 = xf[..., :half], xf[..., half:]
    return jnp.concatenate([x1 * cos - x2 * sin, x2 * cos + x1 * sin], axis=-1).astype(x.dtype)


def project(h, w_in, q_norm_g, k_norm_g, pos):
    b, s, _ = h.shape
    z = jnp.einsum('bsd,de->bse', h, w_in)
    q, k, v, u = jnp.split(z, [ATTN_WIDTH, 2 * ATTN_WIDTH, 3 * ATTN_WIDTH], axis=-1)
    q = q.reshape(b, s, N_DIFF_HEADS, 2, HEAD_DIM)
    k = k.reshape(b, s, N_DIFF_HEADS, 2, HEAD_DIM)
    v = v.reshape(b, s, N_DIFF_HEADS, V_HEAD_DIM)
    q = rope(rms_norm(q, q_norm_g), pos)
    k = rope(rms_norm(k, k_norm_g), pos)
    return q, k, v, u


def diff_lambda(lq1, lk1, lq2, lk2, lam_init):
    f = jnp.float32
    return (jnp.exp(jnp.sum(lq1.astype(f) * lk1.astype(f)))
            - jnp.exp(jnp.sum(lq2.astype(f) * lk2.astype(f))) + lam_init)


def diff_probs(s, lam):
    p = jax.nn.softmax(s, axis=-1)
    return p[:, :, 0] - lam * p[:, :, 1]


def attn_prompt(q, k, v, lam):
    b, s = q.shape[:2]
    nb = s // Q_BLOCK
    scale = HEAD_DIM ** -0.5
    qb = q.reshape(b, nb, Q_BLOCK, N_DIFF_HEADS, 2, HEAD_DIM).transpose(1, 0, 2, 3, 4, 5)
    kpos = jnp.arange(s)

    def one_block(args):
        qi, i = args
        qpos = i * Q_BLOCK + jnp.arange(Q_BLOCK)
        sc = jnp.einsum('bqhcd,bkhcd->bhcqk', qi, k, preferred_element_type=jnp.float32) * scale
        sc = jnp.where(kpos[None, :] <= qpos[:, None], sc, NEG_INF)
        pd = diff_probs(sc, lam)
        return jnp.einsum('bhqk,bkhe->bqhe', pd.astype(v.dtype), v)

    o = lax.map(one_block, (qb, jnp.arange(nb)))
    return o.transpose(1, 0, 2, 3, 4).reshape(b, s, N_DIFF_HEADS, V_HEAD_DIM)


def attn_sample(q, k_new, v_new, k_past, v_past, lam):
    t = q.shape[1]
    p_len = k_past.shape[1]
    scale = HEAD_DIM ** -0.5
    s_past = jnp.einsum('bqhcd,bkhcd->bhcqk', q, k_past, preferred_element_type=jnp.float32) * scale
    s_new = jnp.einsum('bqhcd,bkhcd->bhcqk', q, k_new, preferred_element_type=jnp.float32) * scale
    causal = jnp.arange(t)[None, :] <= jnp.arange(t)[:, None]
    s_new = jnp.where(causal, s_new, NEG_INF)
    pd = diff_probs(jnp.concatenate([s_past, s_new], axis=-1), lam)
    return (jnp.einsum('bhqk,bkhe->bqhe', pd[..., :p_len].astype(v_past.dtype), v_past)
            + jnp.einsum('bhqk,bkhe->bqhe', pd[..., p_len:].astype(v_new.dtype), v_new))


def pool_mix(u, hist, start_pos, w_pool, pool_scale):
    b, s, _ = u.shape
    u_ext = jnp.concatenate([hist, u], axis=1)
    cs = jnp.cumsum(u_ext.astype(jnp.float32), axis=1)
    cs0 = jnp.concatenate([jnp.zeros((b, 1, POOL_WIDTH), jnp.float32), cs], axis=1)
    pos = start_pos + jnp.arange(s)
    means = []
    for g, w in enumerate(POOL_WINDOWS):
        sl = slice(g * POOL_GROUP_WIDTH, (g + 1) * POOL_GROUP_WIDTH)
        wsum = cs0[:, POOL_HIST + 1:POOL_HIST + 1 + s, sl] - cs0[:, POOL_HIST + 1 - w:POOL_HIST + 1 - w + s, sl]
        cnt = jnp.minimum(pos + 1, w).astype(jnp.float32)[None, :, None]
        means.append(wsum / cnt)
    d = (jnp.concatenate(means, axis=-1) - u.astype(jnp.float32)).astype(u.dtype)
    d = d.reshape(b, s, N_POOL_GROUPS, POOL_GROUP_WIDTH)
    y = jnp.einsum('bsgc,gce->bsge', d, w_pool).reshape(b, s, POOL_WIDTH) * pool_scale
    return y, u_ext[:, -POOL_HIST:]


def hier_moe(h, w_coarse, b_coarse, w_fine, b_fine, w_gate, w_up, w_down):
    shp = h.shape
    t = h.reshape(-1, D_MODEL)
    f = jnp.float32
    lc = jnp.einsum('td,dg->tg', t, w_coarse, preferred_element_type=f) + b_coarse.astype(f)
    pc = jax.nn.softmax(lc, axis=-1)
    g_idx = jnp.argmax(lc, axis=-1)
    p_g = jnp.take_along_axis(pc, g_idx[:, None], axis=1)
    lf = jnp.einsum('td,gde->tge', t, w_fine, preferred_element_type=f) + b_fine.astype(f)
    lf_sel = jnp.take_along_axis(lf, g_idx[:, None, None], axis=1)[:, 0]
    top_v, top_i = lax.top_k(lf_sel, TOP_K_FINE)
    w_top = jax.nn.softmax(top_v, axis=-1) * p_g
    e_idx = g_idx[:, None] * EXPERTS_PER_GROUP + top_i
    gates = jnp.sum(jax.nn.one_hot(e_idx, N_EXPERTS, dtype=f) * w_top[..., None], axis=1)
    hg = jnp.einsum('td,edf->tef', t, w_gate)
    hu = jnp.einsum('td,edf->tef', t, w_up)
    a = jax.nn.silu(hg) * hu * gates[:, :, None].astype(t.dtype)
    return jnp.einsum('tef,efd->td', a, w_down).reshape(shp)


def setup_inputs(seed: int = 0) -> dict:
    key = jax.random.key(seed)
    ks = jax.random.split(key, 32)
    f = jnp.float32
    n_pages = PAST_LEN // PAGE_SIZE
    n_phys = (DEC_BATCH * n_pages * 5) // 4
    nrm = lambda k, shp, sc: jax.random.normal(k, shp, f) * sc
    perm = jax.random.permutation(ks[5], n_phys)[:DEC_BATCH * n_pages]
    return {
        'x_prompt': nrm(ks[0], (BATCH, SEQ, D_MODEL), 1.0),
        'x_sample': nrm(ks[1], (DEC_BATCH, DEC_SEQ, D_MODEL), 1.0),
        'cache_k': nrm(ks[2], (DEPTH, n_phys, PAGE_SIZE, 2 * N_DIFF_HEADS, HEAD_DIM), 1.0),
        'cache_v': nrm(ks[3], (DEPTH, n_phys, PAGE_SIZE, N_DIFF_HEADS, V_HEAD_DIM), 1.0),
        'state_pool': nrm(ks[4], (DEPTH, DEC_BATCH, POOL_HIST, POOL_WIDTH), 1.0),
        'page_table': perm.reshape(DEC_BATCH, n_pages).astype(jnp.int32),
        'norm1_g': 1.0 + nrm(ks[6], (DEPTH, D_MODEL), 0.02),
        'w_in': nrm(ks[7], (DEPTH, D_MODEL, 3 * ATTN_WIDTH + POOL_WIDTH), D_MODEL ** -0.5),
        'q_norm_g': 1.0 + nrm(ks[8], (DEPTH, HEAD_DIM), 0.02),
        'k_norm_g': 1.0 + nrm(ks[9], (DEPTH, HEAD_DIM), 0.02),
        'lambda_q1': nrm(ks[10], (DEPTH, HEAD_DIM), 0.1),
        'lambda_k1': nrm(ks[11], (DEPTH, HEAD_DIM), 0.1),
        'lambda_q2': nrm(ks[12], (DEPTH, HEAD_DIM), 0.1),
        'lambda_k2': nrm(ks[13], (DEPTH, HEAD_DIM), 0.1),
        'subln_g': 1.0 + nrm(ks[14], (DEPTH, V_HEAD_DIM), 0.02),
        'w_pool': nrm(ks[15], (DEPTH, N_POOL_GROUPS, POOL_GROUP_WIDTH, POOL_GROUP_WIDTH), POOL_GROUP_WIDTH ** -0.5),
        'pool_scale': 1.0 + nrm(ks[16], (DEPTH, POOL_WIDTH), 0.02),
        'w_out': nrm(ks[17], (DEPTH, MIX_WIDTH, D_MODEL), MIX_WIDTH ** -0.5),
        'norm2_g': 1.0 + nrm(ks[18], (DEPTH, D_MODEL), 0.02),
        'w_coarse': nrm(ks[19], (DEPTH, D_MODEL, N_EXPERT_GROUPS), D_MODEL ** -0.5),
        'b_coarse': nrm(ks[20], (DEPTH, N_EXPERT_GROUPS), 0.01),
        'w_fine': nrm(ks[21], (DEPTH, N_EXPERT_GROUPS, D_MODEL, EXPERTS_PER_GROUP), D_MODEL ** -0.5),
        'b_fine': nrm(ks[22], (DEPTH, N_EXPERT_GROUPS, EXPERTS_PER_GROUP), 0.01),
        'w_gate': nrm(ks[23], (DEPTH, N_EXPERTS, D_MODEL, D_EXPERT), D_MODEL ** -0.5),
        'w_up': nrm(ks[24], (DEPTH, N_EXPERTS, D_MODEL, D_EXPERT), D_MODEL ** -0.5),
        'w_down': nrm(ks[25], (DEPTH, N_EXPERTS, D_EXPERT, D_MODEL), D_EXPERT ** -0.5),
    }


def reference(x_prompt, x_sample, cache_k, cache_v, state_pool, page_table, norm1_g, w_in,
              q_norm_g, k_norm_g, lambda_q1, lambda_k1, lambda_q2, lambda_k2, subln_g, w_pool,
              pool_scale, w_out, norm2_g, w_coarse, b_coarse, w_fine, b_fine, w_gate, w_up, w_down):
    xp, xs = x_prompt, x_sample
    bp, sp = xp.shape[:2]
    bs, ts = xs.shape[:2]
    n_pages = page_table.shape[1]
    past = n_pages * PAGE_SIZE
    pos_p = jnp.arange(sp, dtype=jnp.int32)
    pos_s = past + jnp.arange(ts, dtype=jnp.int32)
    kp_l, vp_l, hp_l, ks_l, vs_l, hs_l = [], [], [], [], [], []
    for l in range(DEPTH):
        lam_init = 0.8 - 0.6 * math.exp(-0.3 * l)
        lam = diff_lambda(lambda_q1[l], lambda_k1[l], lambda_q2[l], lambda_k2[l], lam_init)

        h = rms_norm(xp, norm1_g[l])
        q, k, v, u = project(h, w_in[l], q_norm_g[l], k_norm_g[l], pos_p)
        a = attn_prompt(q, k, v, lam)
        a = (rms_norm(a, subln_g[l]) * (1.0 - lam_init)).reshape(bp, sp, ATTN_WIDTH)
        pool_y, hist_p = pool_mix(u, jnp.zeros((bp, POOL_HIST, POOL_WIDTH), u.dtype), 0, w_pool[l], pool_scale[l])
        xp = xp + jnp.einsum('bse,ed->bsd', jnp.concatenate([a, pool_y], axis=-1), w_out[l])
        xp = xp + hier_moe(rms_norm(xp, norm2_g[l]), w_coarse[l], b_coarse[l], w_fine[l], b_fine[l],
                           w_gate[l], w_up[l], w_down[l])
        kp_l.append(k.reshape(bp, sp, 2 * N_DIFF_HEADS, HEAD_DIM))
        vp_l.append(v)
        hp_l.append(hist_p)

        h = rms_norm(xs, norm1_g[l])
        q, k, v, u = project(h, w_in[l], q_norm_g[l], k_norm_g[l], pos_s)
        k_past = cache_k[l][page_table].reshape(bs, past, N_DIFF_HEADS, 2, HEAD_DIM)
        v_past = cache_v[l][page_table].reshape(bs, past, N_DIFF_HEADS, V_HEAD_DIM)
        a = attn_sample(q, k, v, k_past, v_past, lam)
        a = (rms_norm(a, subln_g[l]) * (1.0 - lam_init)).reshape(bs, ts, ATTN_WIDTH)
        pool_y, hist_s = pool_mix(u, state_pool[l], past, w_pool[l], pool_scale[l])
        xs = xs + jnp.einsum('bse,ed->bsd', jnp.concatenate([a, pool_y], axis=-1), w_out[l])
        xs = xs + hier_moe(rms_norm(xs, norm2_g[l]), w_coarse[l], b_coarse[l], w_fine[l], b_fine[l],
                           w_gate[l], w_up[l], w_down[l])
        ks_l.append(k.reshape(bs, ts, 2 * N_DIFF_HEADS, HEAD_DIM))
        vs_l.append(v)
        hs_l.append(hist_s)

    new_k_prompt = jnp.stack(kp_l)
    new_v_prompt = jnp.stack(vp_l)
    new_pool_prompt = jnp.stack(hp_l)
    new_k_sample = jnp.stack(ks_l)
    new_v_sample = jnp.stack(vs_l)
    new_pool_sample = jnp.stack(hs_l)
    return (xp, xs, new_k_prompt, new_v_prompt, new_pool_prompt, new_k_sample, new_v_sample, new_pool_sample)
```

```python
import functools
import math

import jax
import jax.numpy as jnp
from jax import lax
from jax.experimental import pallas as pl
from jax.experimental.pallas import tpu as pltpu

_F32 = jnp.float32
_BF16 = jnp.bfloat16
_I32 = jnp.int32

D_MODEL = 2048
ATTN_WIDTH = 1024
POOL_WIDTH = 1024
HEAD_DIM = 64
N_HEADS = 8
V_HEAD_DIM = 128
POOL_WINDOWS = (2, 4, 8, 16)
POOL_GROUP_WIDTH = 256
POOL_HIST = 15
HALO_ROWS = 16
N_EXPERTS = 32
EXPERTS_PER_GROUP = 8
N_EXPERT_GROUPS = 4
D_EXPERT = 256
ROPE_THETA = 10000.0
RMS_EPS = 1e-6
NEG_INF = -1e30
PAGE_SIZE = 128
LANES = 128
ROUTE_LANES = 128
VMEM_LIMIT = 56 * 1024 * 1024

PROJ_TILE = 256
ATTN_TILE = 512
MIX_TILE = 256
EXPERT_TILE = 256
COMBINE_TILE = 256
PAGES_PER_STEP = 4
QROWS = 64


def _dot(a, b):
    return jnp.dot(a, b, preferred_element_type=_F32)


def _dot_nt(a, b):
    return lax.dot_general(a, b, (((1,), (1,)), ((), ())), preferred_element_type=_F32)


def _split_bf16(x):
    hi = x.astype(_BF16)
    lo = (x - hi.astype(_F32)).astype(_BF16)
    return hi, lo


def _params(sem):
    return pltpu.CompilerParams(dimension_semantics=sem, vmem_limit_bytes=VMEM_LIMIT)


def _whole_vmem():
    return pl.BlockSpec(memory_space=pltpu.VMEM)


def _proj_kernel(x_ref, g1_ref, w_ref, gq_ref, gk_ref, cos_ref, sin_ref, esum_ref, eexp_ref,
                 q_ref, kf_ref, kb_ref, vf_ref, vb_ref, u_ref):
    x = x_ref[...]
    ms = jnp.mean(x * x, axis=-1, keepdims=True)
    h = (x * lax.rsqrt(ms + RMS_EPS) * g1_ref[...]).astype(_BF16)
    cos = cos_ref[...]
    sin = sin_ref[...]
    lane = lax.broadcasted_iota(_I32, (1, LANES), 1)
    upper = (lane & (HEAD_DIM - 1)) >= HEAD_DIM // 2

    def normed_rope(z, g_ref, outs):
        zh, zl = _split_bf16(z * z)
        msq = _dot(zh, esum_ref[...]) + _dot(zl, esum_ref[...])
        rh, rl = _split_bf16(lax.rsqrt(msq + RMS_EPS))
        rb = _dot(jnp.concatenate([rh, rl], axis=1), eexp_ref[...])
        n = z * rb * g_ref[...]
        for j in range(ATTN_WIDTH // LANES):
            sl = slice(j * LANES, (j + 1) * LANES)
            nj = n[:, sl]
            swapped = jnp.where(upper, pltpu.roll(nj, HEAD_DIM // 2, 1),
                                pltpu.roll(nj, LANES - HEAD_DIM // 2, 1))
            o = nj * cos + swapped * sin
            for ref, scale in outs:
                ref[:, sl] = (o * scale).astype(ref.dtype)

    a = ATTN_WIDTH
    normed_rope(_dot(h, w_ref[:, 0:a]), gq_ref, ((q_ref, HEAD_DIM ** -0.5),))
    normed_rope(_dot(h, w_ref[:, a:2 * a]), gk_ref, ((kf_ref, 1.0), (kb_ref, 1.0)))
    zv = _dot(h, w_ref[:, 2 * a:3 * a])
    vf_ref[...] = zv
    vb_ref[...] = zv.astype(_BF16)
    u_ref[...] = _dot(h, w_ref[:, 3 * a:])


def _proj(x, g1, w_bf, gq, gk, cos, sin, esum, eexp, tm):
    t = x.shape[0]
    n_pos_tiles = cos.shape[0] // tm
    row = lambda i: (i, 0)
    const = lambda i: (0, 0)
    tok_spec = lambda w: pl.BlockSpec((tm, w), row)
    a = ATTN_WIDTH
    return pl.pallas_call(
        _proj_kernel,
        grid=(t // tm,),
        in_specs=[
            tok_spec(D_MODEL),
            pl.BlockSpec((1, D_MODEL), const),
            _whole_vmem(),
            pl.BlockSpec((1, a), const),
            pl.BlockSpec((1, a), const),
            pl.BlockSpec((tm, LANES), lambda i: (i % n_pos_tiles, 0)),
            pl.BlockSpec((tm, LANES), lambda i: (i % n_pos_tiles, 0)),
            _whole_vmem(),
            _whole_vmem(),
        ],
        out_specs=[tok_spec(a)] * 6,
        out_shape=[
            jax.ShapeDtypeStruct((t, a), _BF16),
            jax.ShapeDtypeStruct((t, a), _F32),
            jax.ShapeDtypeStruct((t, a), _BF16),
            jax.ShapeDtypeStruct((t, a), _F32),
            jax.ShapeDtypeStruct((t, a), _BF16),
            jax.ShapeDtypeStruct((t, POOL_WIDTH), _F32),
        ],
        compiler_params=_params(("arbitrary",)),
        name="proj",
    )(x, g1, w_bf, gq, gk, cos, sin, esum, eexp)


def _lambda(lam_ref, lam_init):
    lp = lam_ref[...]
    s1 = jnp.sum(lp[0:1] * lp[1:2], axis=-1, keepdims=True)
    s2 = jnp.sum(lp[2:3] * lp[3:4], axis=-1, keepdims=True)
    return jnp.exp(s1) - jnp.exp(s2) + lam_init


def _sub_norm(o, sg, lam_init):
    ms = jnp.mean(o * o, axis=-1, keepdims=True)
    return o * lax.rsqrt(ms + RMS_EPS) * sg * (1.0 - lam_init)


def _attn_prompt_kernel(lam_ref, sg_ref, q_ref, k_ref, v_ref, o_ref, *, tile, lam_init):
    qi = pl.program_id(2)
    q = q_ref[...]
    lane = lax.broadcasted_iota(_I32, (1, LANES), 1)
    zero = jnp.zeros_like(q)
    qc = (jnp.where(lane < HEAD_DIM, q, zero), jnp.where(lane >= HEAD_DIM, q, zero))

    def update(s, state, vblk):
        m, l, acc = state
        m_new = jnp.maximum(m, jnp.max(s, axis=-1, keepdims=True))
        alpha = jnp.exp(m - m_new)
        p = jnp.exp(s - m_new)
        l = alpha * l + jnp.sum(p, axis=-1, keepdims=True)
        acc = alpha * acc + _dot(p.astype(_BF16), vblk)
        return m_new, l, acc

    def block(kb, carry, masked):
        start = pl.multiple_of(kb * tile, tile)
        kblk = k_ref[pl.ds(start, tile), :]
        vblk = v_ref[pl.ds(start, tile), :]
        out = []
        for c in range(2):
            s = _dot_nt(qc[c], kblk)
            if masked:
                r = lax.broadcasted_iota(_I32, (tile, tile), 0)
                col = lax.broadcasted_iota(_I32, (tile, tile), 1)
                s = jnp.where(col <= r, s, NEG_INF)
            out.append(update(s, carry[c], vblk))
        return tuple(out)

    init_one = (jnp.full((tile, 1), NEG_INF, _F32), jnp.zeros((tile, 1), _F32),
                jnp.zeros((tile, V_HEAD_DIM), _F32))
    carry = lax.fori_loop(0, qi, lambda kb, c: block(kb, c, False), (init_one, init_one))
    (_, l0, a0), (_, l1, a1) = block(qi, carry, True)
    lam = _lambda(lam_ref, lam_init)
    o = a0 / l0 - lam * (a1 / l1)
    o_ref[...] = _sub_norm(o, sg_ref[...], lam_init).astype(o_ref.dtype)


def _attn_prompt(lam_p, sg, q, k, v, lam_init):
    b, s, _ = q.shape
    tile = ATTN_TILE
    kern = functools.partial(_attn_prompt_kernel, tile=tile, lam_init=lam_init)
    return pl.pallas_call(
        kern,
        grid=(b, N_HEADS, s // tile),
        in_specs=[
            pl.BlockSpec((4, HEAD_DIM), lambda bi, h, i: (0, 0)),
            pl.BlockSpec((1, V_HEAD_DIM), lambda bi, h, i: (0, 0)),
            pl.BlockSpec((None, tile, LANES), lambda bi, h, i: (bi, i, h)),
            pl.BlockSpec((None, s, LANES), lambda bi, h, i: (bi, 0, h)),
            pl.BlockSpec((None, s, LANES), lambda bi, h, i: (bi, 0, h)),
        ],
        out_specs=pl.BlockSpec((None, tile, LANES), lambda bi, h, i: (bi, i, h)),
        out_shape=jax.ShapeDtypeStruct((b, s, ATTN_WIDTH), _BF16),
        compiler_params=_params(("arbitrary", "arbitrary", "arbitrary")),
        name="attn_prompt",
    )(lam_p, sg, q, k, v)


def _attn_sample_kernel(pt_ref, lam_ref, sg_ref, q_ref, kn_ref, vn_ref, *rest, n_tok, lam_init):
    npg = PAGES_PER_STEP
    k_refs = rest[:npg]
    v_refs = rest[npg:2 * npg]
    o_ref = rest[2 * npg]
    m_ref, l_ref, acc_ref = rest[2 * npg + 1:]
    j = pl.program_id(1)
    q = q_ref[...]
    rows_per_head = 2 * n_tok
    head_rows = lambda h: slice(h * rows_per_head, (h + 1) * rows_per_head)

    @pl.when(j == 0)
    def _():
        kn = kn_ref[...]
        vn = vn_ref[...]
        tok = lax.broadcasted_iota(_I32, (QROWS, 1), 0) & (n_tok - 1)
        ss = [jnp.where(tok >= jn, jnp.sum(q * kn[jn:jn + 1, :], axis=-1, keepdims=True), NEG_INF)
              for jn in range(n_tok)]
        m = functools.reduce(jnp.maximum, ss)
        ps = [jnp.exp(s - m) for s in ss]
        m_ref[...] = m
        l_ref[...] = functools.reduce(jnp.add, ps)
        for h in range(N_HEADS):
            lanes = slice(h * V_HEAD_DIM, (h + 1) * V_HEAD_DIM)
            acc_ref[head_rows(h), :] = functools.reduce(
                jnp.add, [ps[jn][head_rows(h)] * vn[jn:jn + 1, lanes] for jn in range(n_tok)])

    ss = [_dot(q, kr[...]) for kr in k_refs]
    m_old = m_ref[...]
    m_new = m_old
    for s in ss:
        m_new = jnp.maximum(m_new, jnp.max(s, axis=-1, keepdims=True))
    alpha = jnp.exp(m_old - m_new)
    ps = [jnp.exp(s - m_new) for s in ss]
    l = alpha * l_ref[...] + functools.reduce(jnp.add, [jnp.sum(p, axis=-1, keepdims=True) for p in ps])
    m_ref[...] = m_new
    l_ref[...] = l
    for h in range(N_HEADS):
        pv = functools.reduce(jnp.add, [
            _dot(p[head_rows(h)], vr[pl.ds(h, PAGE_SIZE, stride=N_HEADS), :]) for p, vr in zip(ps, v_refs)])
        acc_ref[head_rows(h), :] = alpha[head_rows(h)] * acc_ref[head_rows(h), :] + pv

    @pl.when(j == pl.num_programs(1) - 1)
    def _():
        o = acc_ref[...] / l
        lam = _lambda(lam_ref, lam_init)
        d = o - lam * pltpu.roll(o, QROWS - n_tok, 0)
        o_ref[...] = _sub_norm(d, sg_ref[...], lam_init)


def _attn_sample(page_table, lam_p, sg, qrep, kn, vn, cache_k, cache_v, n_tok, lam_init):
    nb, n_pages = page_table.shape
    npg = PAGES_PER_STEP
    steps = n_pages // npg
    width = ATTN_WIDTH
    assert QROWS == 2 * N_HEADS * n_tok and cache_k.shape[1:] == (width, PAGE_SIZE)
    assert cache_v.shape[1:] == (PAGE_SIZE * N_HEADS, V_HEAD_DIM)

    def page_spec(p, rows, cols):
        return pl.BlockSpec((None, rows, cols), lambda b, j, pt: (pt[b, j * npg + p], 0, 0))

    per_b = lambda rows: pl.BlockSpec((None, rows, width), lambda b, j, pt: (b, 0, 0))
    kern = functools.partial(_attn_sample_kernel, n_tok=n_tok, lam_init=lam_init)
    grid_spec = pltpu.PrefetchScalarGridSpec(
        num_scalar_prefetch=1,
        grid=(nb, steps),
        in_specs=[
            pl.BlockSpec((4, HEAD_DIM), lambda b, j, pt: (0, 0)),
            pl.BlockSpec((1, V_HEAD_DIM), lambda b, j, pt: (0, 0)),
            per_b(QROWS), per_b(kn.shape[1]), per_b(vn.shape[1]),
        ] + [page_spec(p, width, PAGE_SIZE) for p in range(npg)]
          + [page_spec(p, PAGE_SIZE * N_HEADS, V_HEAD_DIM) for p in range(npg)],
        out_specs=pl.BlockSpec((None, QROWS, V_HEAD_DIM), lambda b, j, pt: (b, 0, 0)),
        scratch_shapes=[pltpu.VMEM((QROWS, 1), _F32), pltpu.VMEM((QROWS, 1), _F32),
                        pltpu.VMEM((QROWS, V_HEAD_DIM), _F32)],
    )
    return pl.pallas_call(
        kern,
        grid_spec=grid_spec,
        out_shape=jax.ShapeDtypeStruct((nb, QROWS, V_HEAD_DIM), _F32),
        compiler_params=_params(("arbitrary", "arbitrary")),
        name="attn_sample",
    )(page_table, lam_p, sg, qrep, kn, vn, *([cache_k] * npg), *([cache_v] * npg))


def _poolwin_kernel(u_ref, halo_ref, hist_ref, d_ref, *, tm, tiles_per_seq, start_pos):
    i = pl.program_id(0)
    t_in_seq = i % tiles_per_seq
    u = u_ref[...]
    halo = jnp.where(t_in_seq == 0, hist_ref[...], halo_ref[...])
    ext = jnp.concatenate([halo, u], axis=0)
    pos = start_pos + t_in_seq * tm + lax.broadcasted_iota(_I32, (tm, 1), 0)
    for g, w in enumerate(POOL_WINDOWS):
        sl = slice(g * POOL_GROUP_WIDTH, (g + 1) * POOL_GROUP_WIDTH)
        acc = ext[:, sl]
        span = 1
        while span < w:
            acc = acc + pltpu.roll(acc, span, 0)
            span *= 2
        cnt = jnp.minimum(pos + 1, w).astype(_F32)
        d_ref[:, sl] = (acc[HALO_ROWS:] / cnt - u[:, sl]).astype(d_ref.dtype)


def _poolwin(u, hist, tm, tiles_per_seq, start_pos):
    t = u.shape[0]
    halo_blocks = max(tm // HALO_ROWS, 1)
    kern = functools.partial(_poolwin_kernel, tm=tm, tiles_per_seq=tiles_per_seq, start_pos=start_pos)
    return pl.pallas_call(
        kern,
        grid=(t // tm,),
        in_specs=[
            pl.BlockSpec((tm, POOL_WIDTH), lambda i: (i, 0)),
            pl.BlockSpec((HALO_ROWS, POOL_WIDTH),
                         lambda i: (jnp.where(i % tiles_per_seq == 0, 0, i * halo_blocks - 1), 0)),
            pl.BlockSpec((None, HALO_ROWS, POOL_WIDTH), lambda i: (i // tiles_per_seq, 0, 0)),
        ],
        out_specs=pl.BlockSpec((tm, POOL_WIDTH), lambda i: (i, 0)),
        out_shape=jax.ShapeDtypeStruct((t, POOL_WIDTH), _BF16),
        compiler_params=_params(("arbitrary",)),
        name="poolwin",
    )(u, u, hist)


def _mix_kernel(x_ref, a_ref, d_ref, wp_ref, ps_ref, wo_ref, g2_ref, wrh_ref, wrl_ref, br_ref,
                cin_ref, h_alias_ref, x1_ref, h2_ref, route_ref, cout_ref, carry_ref, *, tm):
    del h_alias_ref
    i = pl.program_id(0)

    @pl.when(i == 0)
    def _():
        carry_ref[...] = cin_ref[...]

    d = d_ref[...]
    gw = POOL_GROUP_WIDTH
    py = jnp.concatenate([_dot(d[:, g * gw:(g + 1) * gw], wp_ref[g]) for g in range(len(POOL_WINDOWS))],
                         axis=1) * ps_ref[...]
    x1 = (x_ref[...] + _dot(a_ref[...], wo_ref[0:ATTN_WIDTH, :])
          + _dot(py.astype(_BF16), wo_ref[ATTN_WIDTH:, :]))
    x1_ref[...] = x1
    ms = jnp.mean(x1 * x1, axis=-1, keepdims=True)
    h2 = x1 * lax.rsqrt(ms + RMS_EPS) * g2_ref[...]
    h2_ref[...] = h2

    hh, hl = _split_bf16(h2)
    logits = _dot(hh, wrh_ref[...]) + _dot(hl, wrh_ref[...]) + _dot(hh, wrl_ref[...]) + br_ref[...]
    lane = lax.broadcasted_iota(_I32, (1, ROUTE_LANES), 1)
    big = jnp.int32(ROUTE_LANES)
    is_coarse = (lane >= N_EXPERTS) & (lane < N_EXPERTS + N_EXPERT_GROUPS)
    lc = jnp.where(is_coarse, logits, NEG_INF)
    mc = jnp.max(lc, axis=-1, keepdims=True)
    g_idx = jnp.min(jnp.where(lc == mc, lane, big), axis=-1, keepdims=True) - N_EXPERTS
    p_g = 1.0 / jnp.sum(jnp.exp(lc - mc), axis=-1, keepdims=True)
    in_group = (lane < N_EXPERTS) & ((lane >> (EXPERTS_PER_GROUP.bit_length() - 1)) == g_idx)
    lf = jnp.where(in_group, logits, NEG_INF)
    v1 = jnp.max(lf, axis=-1, keepdims=True)
    i1 = jnp.min(jnp.where(lf == v1, lane, big), axis=-1, keepdims=True)
    lf2 = jnp.where(lane == i1, NEG_INF, lf)
    v2 = jnp.max(lf2, axis=-1, keepdims=True)
    i2 = jnp.min(jnp.where(lf2 == v2, lane, big), axis=-1, keepdims=True)
    e21 = jnp.exp(v2 - v1)
    w1 = p_g / (1.0 + e21)
    w2 = p_g * e21 / (1.0 + e21)

    oh1 = lane == i1
    oh2 = lane == i2
    onehot = (oh1 | oh2).astype(_F32)
    r = lax.broadcasted_iota(_I32, (tm, tm), 0)
    col = lax.broadcasted_iota(_I32, (tm, tm), 1)
    lower = (col < r).astype(_BF16)
    before = _dot(lower, onehot.astype(_BF16)) + carry_ref[...]
    rank1 = jnp.sum(jnp.where(oh1, before, 0.0), axis=-1, keepdims=True)
    rank2 = jnp.sum(jnp.where(oh2, before, 0.0), axis=-1, keepdims=True)
    carry = carry_ref[...] + jnp.sum(onehot, axis=0, keepdims=True)
    carry_ref[...] = carry
    cout_ref[...] = carry

    rec = jnp.zeros((tm, ROUTE_LANES), _F32)
    for k, val in enumerate((i1.astype(_F32), i2.astype(_F32), w1, w2, rank1, rank2)):
        rec = jnp.where(lane == k, val, rec)
    route_ref[...] = rec


def _mix(x, a, d, wp_bf, ps, wo_bf, g2, wrh, wrl, br, counts_in, h_all, row_offset_tiles, tm):
    t = x.shape[0]
    row = lambda i: (i, 0)
    const = lambda i: (0, 0)
    kern = functools.partial(_mix_kernel, tm=tm)
    return pl.pallas_call(
        kern,
        grid=(t // tm,),
        in_specs=[
            pl.BlockSpec((tm, D_MODEL), row),
            pl.BlockSpec((tm, ATTN_WIDTH), row),
            pl.BlockSpec((tm, POOL_WIDTH), row),
            _whole_vmem(),
            pl.BlockSpec((1, POOL_WIDTH), const),
            _whole_vmem(),
            pl.BlockSpec((1, D_MODEL), const),
            _whole_vmem(),
            _whole_vmem(),
            pl.BlockSpec((1, ROUTE_LANES), const),
            pl.BlockSpec((1, ROUTE_LANES), const),
            pl.BlockSpec(memory_space=pl.ANY),
        ],
        out_specs=[
            pl.BlockSpec((tm, D_MODEL), row),
            pl.BlockSpec((tm, D_MODEL), lambda i: (i + row_offset_tiles, 0)),
            pl.BlockSpec((tm, ROUTE_LANES), row),
            pl.BlockSpec((1, ROUTE_LANES), const),
        ],
        out_shape=[
            jax.ShapeDtypeStruct((t, D_MODEL), _F32),
            jax.ShapeDtypeStruct(h_all.shape, _F32),
            jax.ShapeDtypeStruct((t, ROUTE_LANES), _F32),
            jax.ShapeDtypeStruct((1, ROUTE_LANES), _F32),
        ],
        scratch_shapes=[pltpu.VMEM((1, ROUTE_LANES), _F32)],
        input_output_aliases={11: 1},
        compiler_params=_params(("arbitrary",)),
        name="mix",
    )(x, a, d, wp_bf, ps, wo_bf, g2, wrh, wrl, br, counts_in, h_all)


def _row_copy(src_hbm, dst_vmem, src_row, dst_row, sem):
    return pltpu.make_async_copy(src_hbm.at[pl.ds(src_row, 1), :], dst_vmem.at[pl.ds(dst_row, 1), :], sem)


def _experts_kernel(texp_ref, nused_ref, rows_ref, h_hbm, wg_ref, wu_ref, wd_ref, y_ref, xbuf, sem, *, tm):
    del texp_ref
    i = pl.program_id(0)

    @pl.when(i < nused_ref[0])
    def _():
        def start(r, c):
            _row_copy(h_hbm, xbuf, rows_ref[0, 0, r], r, sem).start()
            return c

        lax.fori_loop(0, tm, start, 0)

        def wait(r, c):
            _row_copy(h_hbm, xbuf, 0, r, sem).wait()
            return c

        lax.fori_loop(0, tm, wait, 0)
        x = xbuf[...].astype(_BF16)
        hg = _dot(x, wg_ref[...].astype(_BF16))
        hu = _dot(x, wu_ref[...].astype(_BF16))
        act = (hg * jax.nn.sigmoid(hg) * hu).astype(_BF16)
        y_ref[...] = _dot(act, wd_ref[...].astype(_BF16))

    @pl.when(i >= nused_ref[0])
    def _():
        y_ref[...] = jnp.zeros_like(y_ref)


def _experts(tile_expert, n_used, row_token, h_all, w_gate, w_up, w_down, tm):
    n_tiles = tile_expert.shape[0]
    rows3 = row_token.reshape(n_tiles, 1, tm)
    kern = functools.partial(_experts_kernel, tm=tm)
    grid_spec = pltpu.PrefetchScalarGridSpec(
        num_scalar_prefetch=2,
        grid=(n_tiles,),
        in_specs=[
            pl.BlockSpec((1, 1, tm), lambda i, te, nu: (i, 0, 0), memory_space=pltpu.SMEM),
            pl.BlockSpec(memory_space=pl.ANY),
            pl.BlockSpec((None, D_MODEL, D_EXPERT), lambda i, te, nu: (te[i], 0, 0)),
            pl.BlockSpec((None, D_MODEL, D_EXPERT), lambda i, te, nu: (te[i], 0, 0)),
            pl.BlockSpec((None, D_EXPERT, D_MODEL), lambda i, te, nu: (te[i], 0, 0)),
        ],
        out_specs=pl.BlockSpec((tm, D_MODEL), lambda i, te, nu: (i, 0)),
        scratch_shapes=[pltpu.VMEM((tm, D_MODEL), _F32), pltpu.SemaphoreType.DMA(())],
    )
    return pl.pallas_call(
        kern,
        grid_spec=grid_spec,
        out_shape=jax.ShapeDtypeStruct((n_tiles * tm, D_MODEL), _F32),
        compiler_params=_params(("arbitrary",)),
        name="experts",
    )(tile_expert, n_used, rows3, h_all, w_gate, w_up, w_down)


def _combine_kernel(pos_ref, x1_ref, route_ref, y_hbm, o_ref, r0, r1, sem, *, tm):
    def start(r, c):
        _row_copy(y_hbm, r0, pos_ref[0, 0, 2 * r], r, sem.at[0]).start()
        _row_copy(y_hbm, r1, pos_ref[0, 0, 2 * r + 1], r, sem.at[1]).start()
        return c

    lax.fori_loop(0, tm, start, 0)

    def wait(r, c):
        _row_copy(y_hbm, r0, 0, r, sem.at[0]).wait()
        _row_copy(y_hbm, r1, 0, r, sem.at[1]).wait()
        return c

    lax.fori_loop(0, tm, wait, 0)
    rec = route_ref[...]
    w1 = rec[:, 2:3]
    w2 = rec[:, 3:4]
    o_ref[...] = x1_ref[...] + w1 * r0[...] + w2 * r1[...]


def _combine(pos, x1, route, y_sorted, tm):
    t = x1.shape[0]
    pos3 = pos.reshape(t // tm, 1, 2 * tm)
    kern = functools.partial(_combine_kernel, tm=tm)
    return pl.pallas_call(
        kern,
        grid=(t // tm,),
        in_specs=[
            pl.BlockSpec((1, 1, 2 * tm), lambda i: (i, 0, 0), memory_space=pltpu.SMEM),
            pl.BlockSpec((tm, D_MODEL), lambda i: (i, 0)),
            pl.BlockSpec((tm, ROUTE_LANES), lambda i: (i, 0)),
            pl.BlockSpec(memory_space=pl.ANY),
        ],
        out_specs=pl.BlockSpec((tm, D_MODEL), lambda i: (i, 0)),
        out_shape=jax.ShapeDtypeStruct((t, D_MODEL), _F32),
        scratch_shapes=[pltpu.VMEM((tm, D_MODEL), _F32), pltpu.VMEM((tm, D_MODEL), _F32),
                        pltpu.SemaphoreType.DMA((2,))],
        compiler_params=_params(("arbitrary",)),
        name="combine",
    )(pos3, x1, route, y_sorted)


def _rope_tables(pos):
    half = HEAD_DIM // 2
    inv = ROPE_THETA ** (-jnp.arange(0, HEAD_DIM, 2, dtype=_F32) / HEAD_DIM)
    ang = pos.astype(_F32)[:, None] * inv[None, :]
    cos = jnp.cos(ang)
    sin = jnp.sin(ang)
    del half
    return jnp.tile(cos, (1, 4)), jnp.tile(jnp.concatenate([-sin, sin], axis=1), (1, 2))


def _head_sum_matrices():
    lane_head = jnp.arange(ATTN_WIDTH) // HEAD_DIM
    cols = jnp.arange(LANES)
    esum = (lane_head[:, None] == cols[None, :]).astype(_F32) / HEAD_DIM
    eexp = (cols[:, None] == lane_head[None, :]).astype(_F32)
    return esum.astype(_BF16), jnp.concatenate([eexp, eexp], axis=0).astype(_BF16)


def kernel(x_prompt, x_sample, cache_k, cache_v, state_pool, page_table, norm1_g, w_in, q_norm_g, k_norm_g,
           lambda_q1, lambda_k1, lambda_q2, lambda_k2, subln_g, w_pool, pool_scale, w_out, norm2_g, w_coarse,
           b_coarse, w_fine, b_fine, w_gate, w_up, w_down):
    depth = w_in.shape[0]
    assert depth == 1
    l = 0
    lam_init = 0.8 - 0.6 * math.exp(-0.3 * l)
    bp, sp, _ = x_prompt.shape
    bs, ts, _ = x_sample.shape
    tp, tsamp = bp * sp, bs * ts
    n_pages = page_table.shape[1]
    past = n_pages * PAGE_SIZE
    n_hc = 2 * N_HEADS

    w_in_bf = w_in[l].astype(_BF16)
    w_out_bf = w_out[l].astype(_BF16)
    w_pool_bf = w_pool[l].astype(_BF16)
    g1 = norm1_g[l][None, :]
    g2 = norm2_g[l][None, :]
    gq = jnp.tile(q_norm_g[l], n_hc)[None, :]
    gk = jnp.tile(k_norm_g[l], n_hc)[None, :]
    sg = subln_g[l][None, :]
    ps = pool_scale[l][None, :]
    lam_p = jnp.stack([lambda_q1[l], lambda_k1[l], lambda_q2[l], lambda_k2[l]])
    esum, eexp = _head_sum_matrices()
    pad = ROUTE_LANES - N_EXPERTS - N_EXPERT_GROUPS
    w_fine_flat = jnp.transpose(w_fine[l], (1, 0, 2)).reshape(D_MODEL, N_EXPERTS)
    w_route = jnp.concatenate([w_fine_flat, w_coarse[l], jnp.zeros((D_MODEL, pad), _F32)], axis=1)
    wrh, wrl = _split_bf16(w_route)
    b_route = jnp.concatenate([b_fine[l].reshape(-1), b_coarse[l], jnp.zeros((pad,), _F32)])[None, :]

    cos_p, sin_p = _rope_tables(jnp.arange(sp, dtype=jnp.int32))
    cos_s, sin_s = _rope_tables(past + jnp.arange(ts, dtype=jnp.int32))
    cos_s, sin_s = jnp.tile(cos_s, (bs, 1)), jnp.tile(sin_s, (bs, 1))
    xp2 = x_prompt.reshape(tp, D_MODEL)
    xs2 = x_sample.reshape(tsamp, D_MODEL)
    qp, kp, kp_bf, vp, vp_bf, up = _proj(xp2, g1, w_in_bf, gq, gk, cos_p, sin_p, esum, eexp, PROJ_TILE)
    qs, ks, ks_bf, vs, vs_bf, us = _proj(xs2, g1, w_in_bf, gq, gk, cos_s, sin_s, esum, eexp, tsamp)

    shp = (bp, sp, ATTN_WIDTH)
    a_p = _attn_prompt(lam_p, sg, qp.reshape(shp), kp_bf.reshape(shp), vp_bf.reshape(shp), lam_init)
    a_p = a_p.reshape(tp, ATTN_WIDTH)

    q4 = qs.astype(_F32).reshape(bs, 1, 1, ts, ATTN_WIDTH)
    lane_hc = jnp.arange(ATTN_WIDTH) // HEAD_DIM
    head = jnp.arange(N_HEADS)[:, None, None, None]
    comp = jnp.arange(2)[None, :, None, None]
    keep = lane_hc[None, None, None, :] == head * 2 + comp
    qrep = jnp.where(keep[None], q4, 0.0).reshape(bs, QROWS, ATTN_WIDTH)
    new_rows = 8
    kn = jnp.pad(ks.reshape(bs, ts, ATTN_WIDTH), ((0, 0), (0, new_rows - ts), (0, 0)))
    vn = jnp.pad(vs.reshape(bs, ts, ATTN_WIDTH), ((0, 0), (0, new_rows - ts), (0, 0)))
    n_phys = cache_k.shape[1]
    ck = jnp.transpose(cache_k[l], (0, 2, 3, 1)).reshape(n_phys, ATTN_WIDTH, PAGE_SIZE)
    cv = cache_v[l].reshape(n_phys, PAGE_SIZE * N_HEADS, V_HEAD_DIM)
    a_s = _attn_sample(page_table, lam_p, sg, qrep, kn, vn, ck, cv, ts, lam_init)
    a_s = a_s.reshape(bs, N_HEADS, 2, ts, V_HEAD_DIM)[:, :, 0]
    a_s = a_s.transpose(0, 2, 1, 3).reshape(tsamp, ATTN_WIDTH).astype(_BF16)

    hist_p = jnp.zeros((bp, HALO_ROWS, POOL_WIDTH), _F32)
    d_p = _poolwin(up, hist_p, MIX_TILE, sp // MIX_TILE, 0)
    seq_rows = 8
    us_pad = jnp.pad(us.reshape(bs, ts, POOL_WIDTH), ((0, 0), (0, seq_rows - ts), (0, 0)))
    hist_s = jnp.pad(state_pool[l], ((0, 0), (HALO_ROWS - POOL_HIST, 0), (0, 0)))
    d_s = _poolwin(us_pad.reshape(bs * seq_rows, POOL_WIDTH), hist_s, seq_rows, 1, past)
    d_s = d_s.reshape(bs, seq_rows, POOL_WIDTH)[:, :ts].reshape(tsamp, POOL_WIDTH)

    t_all = tp + tsamp
    h_all0 = jnp.zeros((t_all, D_MODEL), _F32)
    counts0 = jnp.zeros((1, ROUTE_LANES), _F32)
    x1_p, h_all1, route_p, counts_p = _mix(xp2, a_p, d_p, w_pool_bf, ps, w_out_bf, g2, wrh, wrl, b_route,
                                           counts0, h_all0, 0, MIX_TILE)
    x1_s, h_all, route_s, counts = _mix(xs2, a_s, d_s, w_pool_bf, ps, w_out_bf, g2, wrh, wrl, b_route,
                                        counts_p, h_all1, tp // tsamp, tsamp)

    tm = EXPERT_TILE
    n_tiles = (2 * t_all + N_EXPERTS * (tm - 1) + tm - 1) // tm
    route = jnp.concatenate([route_p, route_s], axis=0)
    e_idx = route[:, 0:2].astype(jnp.int32)
    rank = route[:, 4:6].astype(jnp.int32)
    cnt = counts[0, :N_EXPERTS].astype(jnp.int32)
    padded = (cnt + tm - 1) // tm * tm
    ends = jnp.cumsum(padded)
    pos = (ends - padded)[e_idx] + rank
    tok = jnp.broadcast_to(jnp.arange(t_all, dtype=jnp.int32)[:, None], (t_all, 2))
    row_token = jnp.zeros((n_tiles * tm,), jnp.int32).at[pos.reshape(-1)].set(tok.reshape(-1))
    n_used = (ends[-1] // tm).astype(jnp.int32)
    tile_ids = jnp.arange(n_tiles, dtype=jnp.int32)
    te = jnp.searchsorted(ends, tile_ids * tm, side='right').astype(jnp.int32)
    te = jnp.minimum(te, N_EXPERTS - 1)
    last = te[jnp.maximum(n_used - 1, 0)]
    tile_expert = jnp.where(tile_ids < n_used, te, last)

    y_sorted = _experts(tile_expert, n_used.reshape(1), row_token, h_all, w_gate[l], w_up[l], w_down[l], tm)
    y_p = _combine(pos[:tp], x1_p, route_p, y_sorted, COMBINE_TILE)
    y_s = _combine(pos[tp:], x1_s, route_s, y_sorted, tsamp)

    new_pool_p = up.reshape(bp, sp, POOL_WIDTH)[:, sp - POOL_HIST:]
    new_pool_s = jnp.concatenate([state_pool[l], us.reshape(bs, ts, POOL_WIDTH)], axis=1)[:, -POOL_HIST:]
    return (y_p.reshape(bp, sp, D_MODEL),
            y_s.reshape(bs, ts, D_MODEL),
            kp.reshape(1, bp, sp, n_hc, HEAD_DIM),
            vp.reshape(1, bp, sp, N_HEADS, V_HEAD_DIM),
            new_pool_p[None],
            ks.reshape(1, bs, ts, n_hc, HEAD_DIM),
            vs.reshape(1, bs, ts, N_HEADS, V_HEAD_DIM),
            new_pool_s[None])
```

```python
import functools
import math

import jax
import jax.numpy as jnp
from jax import lax
from jax.experimental import pallas as pl
from jax.experimental.pallas import tpu as pltpu

_F32 = jnp.float32
_BF16 = jnp.bfloat16
_I32 = jnp.int32

D_MODEL = 2048
ATTN_WIDTH = 1024
POOL_WIDTH = 1024
HEAD_DIM = 64
N_HEADS = 8
V_HEAD_DIM = 128
POOL_WINDOWS = (2, 4, 8, 16)
POOL_GROUP_WIDTH = 256
POOL_HIST = 15
HALO_ROWS = 16
N_EXPERTS = 32
EXPERTS_PER_GROUP = 8
N_EXPERT_GROUPS = 4
D_EXPERT = 256
ROPE_THETA = 10000.0
RMS_EPS = 1e-6
NEG_INF = -1e30
PAGE_SIZE = 128
LANES = 128
ROUTE_LANES = 128
REC_ROWS = 8
VMEM_LIMIT = 56 * 1024 * 1024

PROJ_TILE = 256
ATTN_TILE = 512
MIX_TILE = 256
EXPERT_TILE = 256
PAGES_PER_STEP = 8
QROWS = 64
DMA_UNROLL = 8


def _dot(a, b):
    return jnp.dot(a, b, preferred_element_type=_F32)


def _split_bf16(x):
    hi = x.astype(_BF16)
    lo = (x - hi.astype(_F32)).astype(_BF16)
    return hi, lo


def _params(sem):
    return pltpu.CompilerParams(dimension_semantics=sem, vmem_limit_bytes=VMEM_LIMIT)


def _whole_vmem():
    return pl.BlockSpec(memory_space=pltpu.VMEM)


def _proj_kernel(x_ref, g1_ref, w_ref, gq_ref, gk_ref, cos_ref, sin_ref, esum_ref, eexp_ref,
                 q_ref, kf_ref, kb_ref, vf_ref, vb_ref, u_ref, *, k_transposed):
    x = x_ref[...]
    ms = jnp.mean(x * x, axis=-1, keepdims=True)
    h = (x * lax.rsqrt(ms + RMS_EPS) * g1_ref[...]).astype(_BF16)
    cos = cos_ref[...]
    sin = sin_ref[...]
    lane = lax.broadcasted_iota(_I32, (1, LANES), 1)
    upper = (lane & (HEAD_DIM - 1)) >= HEAD_DIM // 2

    def normed_rope(z, g_ref, outs, transposed):
        zh, zl = _split_bf16(z * z)
        msq = _dot(zh, esum_ref[...]) + _dot(zl, esum_ref[...])
        rh, rl = _split_bf16(lax.rsqrt(msq + RMS_EPS))
        rb = _dot(jnp.concatenate([rh, rl], axis=1), eexp_ref[...])
        n = z * rb * g_ref[...]
        for j in range(ATTN_WIDTH // LANES):
            sl = slice(j * LANES, (j + 1) * LANES)
            nj = n[:, sl]
            swapped = jnp.where(upper, pltpu.roll(nj, HEAD_DIM // 2, 1),
                                pltpu.roll(nj, LANES - HEAD_DIM // 2, 1))
            o = nj * cos + swapped * sin
            if transposed:
                ot = o.T
                for ref, _ in outs:
                    ref[sl, :] = ot.astype(ref.dtype)
            else:
                for ref, scale in outs:
                    ref[:, sl] = (o * scale).astype(ref.dtype)

    a = ATTN_WIDTH
    normed_rope(_dot(h, w_ref[:, 0:a]), gq_ref, ((q_ref, HEAD_DIM ** -0.5),), False)
    normed_rope(_dot(h, w_ref[:, a:2 * a]), gk_ref, ((kf_ref, 1.0), (kb_ref, 1.0)), k_transposed)
    zv = _dot(h, w_ref[:, 2 * a:3 * a])
    vf_ref[...] = zv
    vb_ref[...] = zv.astype(_BF16)
    u_ref[...] = _dot(h, w_ref[:, 3 * a:])


def _proj(x, g1, w_bf, gq, gk, cos, sin, esum, eexp, tm, seq_len=None):
    t = x.shape[0]
    n_pos_tiles = cos.shape[0] // tm
    row = lambda i: (i, 0)
    const = lambda i: (0, 0)
    tok_spec = lambda w: pl.BlockSpec((tm, w), row)
    a = ATTN_WIDTH
    if seq_len is None:
        k_spec, k_shape = tok_spec(a), (t, a)
    else:
        tiles_per_seq = seq_len // tm
        k_spec = pl.BlockSpec((None, a, tm), lambda i: (i // tiles_per_seq, 0, i % tiles_per_seq))
        k_shape = (t // seq_len, a, seq_len)
    return pl.pallas_call(
        functools.partial(_proj_kernel, k_transposed=seq_len is not None),
        grid=(t // tm,),
        in_specs=[
            tok_spec(D_MODEL),
            pl.BlockSpec((1, D_MODEL), const),
            _whole_vmem(),
            pl.BlockSpec((1, a), const),
            pl.BlockSpec((1, a), const),
            pl.BlockSpec((tm, LANES), lambda i: (i % n_pos_tiles, 0)),
            pl.BlockSpec((tm, LANES), lambda i: (i % n_pos_tiles, 0)),
            _whole_vmem(),
            _whole_vmem(),
        ],
        out_specs=[tok_spec(a), k_spec, k_spec, tok_spec(a), tok_spec(a), tok_spec(POOL_WIDTH)],
        out_shape=[
            jax.ShapeDtypeStruct((t, a), _BF16),
            jax.ShapeDtypeStruct(k_shape, _F32),
            jax.ShapeDtypeStruct(k_shape, _BF16),
            jax.ShapeDtypeStruct((t, a), _F32),
            jax.ShapeDtypeStruct((t, a), _BF16),
            jax.ShapeDtypeStruct((t, POOL_WIDTH), _F32),
        ],
        compiler_params=_params(("arbitrary",)),
        name="proj",
    )(x, g1, w_bf, gq, gk, cos, sin, esum, eexp)


def _lambda(lam_ref, lam_init):
    lp = lam_ref[...]
    s1 = jnp.sum(lp[0:1] * lp[1:2], axis=-1, keepdims=True)
    s2 = jnp.sum(lp[2:3] * lp[3:4], axis=-1, keepdims=True)
    return jnp.exp(s1) - jnp.exp(s2) + lam_init


def _sub_norm(o, sg, lam_init):
    ms = jnp.mean(o * o, axis=-1, keepdims=True)
    return o * lax.rsqrt(ms + RMS_EPS) * sg * (1.0 - lam_init)


def _attn_prompt_kernel(lam_ref, sg_ref, q_ref, kt_ref, v_ref, o_ref, *, tile, lam_init):
    qi = pl.program_id(2)
    q = q_ref[...]
    lane = lax.broadcasted_iota(_I32, (1, LANES), 1)
    zero = jnp.zeros_like(q)
    qc = (jnp.where(lane < HEAD_DIM, q, zero), jnp.where(lane >= HEAD_DIM, q, zero))

    def update(s, state, vblk):
        m, l, acc = state
        m_new = jnp.maximum(m, jnp.max(s, axis=-1, keepdims=True))
        alpha = jnp.exp(m - m_new)
        p = jnp.exp(s - m_new)
        l = alpha * l + jnp.sum(p, axis=-1, keepdims=True)
        acc = alpha * acc + _dot(p.astype(_BF16), vblk)
        return m_new, l, acc

    def block(kb, carry, masked):
        start = pl.multiple_of(kb * tile, tile)
        ktblk = kt_ref[:, pl.ds(start, tile)]
        vblk = v_ref[pl.ds(start, tile), :]
        out = []
        for c in range(2):
            s = _dot(qc[c], ktblk)
            if masked:
                r = lax.broadcasted_iota(_I32, (tile, tile), 0)
                col = lax.broadcasted_iota(_I32, (tile, tile), 1)
                s = jnp.where(col <= r, s, NEG_INF)
            out.append(update(s, carry[c], vblk))
        return tuple(out)

    init_one = (jnp.full((tile, 1), NEG_INF, _F32), jnp.zeros((tile, 1), _F32),
                jnp.zeros((tile, V_HEAD_DIM), _F32))
    carry = lax.fori_loop(0, qi, lambda kb, c: block(kb, c, False), (init_one, init_one))
    (_, l0, a0), (_, l1, a1) = block(qi, carry, True)
    lam = _lambda(lam_ref, lam_init)
    o = a0 / l0 - lam * (a1 / l1)
    o_ref[...] = _sub_norm(o, sg_ref[...], lam_init).astype(o_ref.dtype)


def _attn_prompt(lam_p, sg, q, kt, v, lam_init):
    b, s, _ = q.shape
    tile = ATTN_TILE
    kern = functools.partial(_attn_prompt_kernel, tile=tile, lam_init=lam_init)
    return pl.pallas_call(
        kern,
        grid=(b, N_HEADS, s // tile),
        in_specs=[
            pl.BlockSpec((4, HEAD_DIM), lambda bi, h, i: (0, 0)),
            pl.BlockSpec((1, V_HEAD_DIM), lambda bi, h, i: (0, 0)),
            pl.BlockSpec((None, tile, LANES), lambda bi, h, i: (bi, i, h)),
            pl.BlockSpec((None, LANES, s), lambda bi, h, i: (bi, h, 0)),
            pl.BlockSpec((None, s, LANES), lambda bi, h, i: (bi, 0, h)),
        ],
        out_specs=pl.BlockSpec((None, tile, LANES), lambda bi, h, i: (bi, i, h)),
        out_shape=jax.ShapeDtypeStruct((b, s, ATTN_WIDTH), _BF16),
        compiler_params=_params(("arbitrary", "arbitrary", "arbitrary")),
        name="attn_prompt",
    )(lam_p, sg, q, kt, v)


def _attn_sample_kernel(pt_ref, lam_ref, sg_ref, q_ref, kn_ref, vn_ref, *rest, n_tok, lam_init):
    del pt_ref
    npg = PAGES_PER_STEP
    k_refs = rest[:npg]
    v_refs = rest[npg:2 * npg]
    o_ref = rest[2 * npg]
    m_ref, l_ref, acc_ref = rest[2 * npg + 1:]
    j = pl.program_id(1)
    q = q_ref[...]
    rows_per_head = 2 * n_tok
    head_rows = lambda h: slice(h * rows_per_head, (h + 1) * rows_per_head)

    @pl.when(j == 0)
    def _():
        kn = kn_ref[...]
        vn = vn_ref[...]
        tok = lax.broadcasted_iota(_I32, (QROWS, 1), 0) & (n_tok - 1)
        ss = [jnp.where(tok >= jn, jnp.sum(q * kn[jn:jn + 1, :], axis=-1, keepdims=True), NEG_INF)
              for jn in range(n_tok)]
        m = functools.reduce(jnp.maximum, ss)
        ps = [jnp.exp(s - m) for s in ss]
        m_ref[...] = m
        l_ref[...] = functools.reduce(jnp.add, ps)
        for h in range(N_HEADS):
            lanes = slice(h * V_HEAD_DIM, (h + 1) * V_HEAD_DIM)
            acc_ref[head_rows(h), :] = functools.reduce(
                jnp.add, [ps[jn][head_rows(h)] * vn[jn:jn + 1, lanes] for jn in range(n_tok)])

    ss = [_dot(q, kr[...]) for kr in k_refs]
    m_old = m_ref[...]
    m_new = m_old
    for s in ss:
        m_new = jnp.maximum(m_new, jnp.max(s, axis=-1, keepdims=True))
    alpha = jnp.exp(m_old - m_new)
    ps = [jnp.exp(s - m_new) for s in ss]
    l = alpha * l_ref[...] + functools.reduce(jnp.add, [jnp.sum(p, axis=-1, keepdims=True) for p in ps])
    m_ref[...] = m_new
    l_ref[...] = l
    for h in range(N_HEADS):
        pv = functools.reduce(jnp.add, [
            _dot(p[head_rows(h)], vr[pl.ds(h, PAGE_SIZE, stride=N_HEADS), :]) for p, vr in zip(ps, v_refs)])
        acc_ref[head_rows(h), :] = alpha[head_rows(h)] * acc_ref[head_rows(h), :] + pv

    @pl.when(j == pl.num_programs(1) - 1)
    def _():
        o = acc_ref[...] / l
        lam = _lambda(lam_ref, lam_init)
        d = o - lam * pltpu.roll(o, QROWS - n_tok, 0)
        o_ref[...] = _sub_norm(d, sg_ref[...], lam_init)


def _attn_sample(page_table, lam_p, sg, qrep, kn, vn, cache_k, cache_v, n_tok, lam_init):
    nb, n_pages = page_table.shape
    npg = PAGES_PER_STEP
    steps = n_pages // npg
    width = ATTN_WIDTH
    assert QROWS == 2 * N_HEADS * n_tok and cache_k.shape[1:] == (width, PAGE_SIZE)
    assert cache_v.shape[1:] == (PAGE_SIZE * N_HEADS, V_HEAD_DIM)

    def page_spec(p, rows, cols):
        return pl.BlockSpec((None, rows, cols), lambda b, j, pt: (pt[b, j * npg + p], 0, 0))

    per_b = lambda rows: pl.BlockSpec((None, rows, width), lambda b, j, pt: (b, 0, 0))
    kern = functools.partial(_attn_sample_kernel, n_tok=n_tok, lam_init=lam_init)
    grid_spec = pltpu.PrefetchScalarGridSpec(
        num_scalar_prefetch=1,
        grid=(nb, steps),
        in_specs=[
            pl.BlockSpec((4, HEAD_DIM), lambda b, j, pt: (0, 0)),
            pl.BlockSpec((1, V_HEAD_DIM), lambda b, j, pt: (0, 0)),
            per_b(QROWS), per_b(kn.shape[1]), per_b(vn.shape[1]),
        ] + [page_spec(p, width, PAGE_SIZE) for p in range(npg)]
          + [page_spec(p, PAGE_SIZE * N_HEADS, V_HEAD_DIM) for p in range(npg)],
        out_specs=pl.BlockSpec((None, QROWS, V_HEAD_DIM), lambda b, j, pt: (b, 0, 0)),
        scratch_shapes=[pltpu.VMEM((QROWS, 1), _F32), pltpu.VMEM((QROWS, 1), _F32),
                        pltpu.VMEM((QROWS, V_HEAD_DIM), _F32)],
    )
    return pl.pallas_call(
        kern,
        grid_spec=grid_spec,
        out_shape=jax.ShapeDtypeStruct((nb, QROWS, V_HEAD_DIM), _F32),
        compiler_params=_params(("arbitrary", "arbitrary")),
        name="attn_sample",
    )(page_table, lam_p, sg, qrep, kn, vn, *([cache_k] * npg), *([cache_v] * npg))


def _poolwin_kernel(u_ref, halo_ref, hist_ref, d_ref, *, tm, tiles_per_seq, start_pos):
    i = pl.program_id(0)
    t_in_seq = i % tiles_per_seq
    u = u_ref[...]
    halo = jnp.where(t_in_seq == 0, hist_ref[...], halo_ref[...])
    ext = jnp.concatenate([halo, u], axis=0)
    pos = start_pos + t_in_seq * tm + lax.broadcasted_iota(_I32, (tm, 1), 0)
    for g, w in enumerate(POOL_WINDOWS):
        sl = slice(g * POOL_GROUP_WIDTH, (g + 1) * POOL_GROUP_WIDTH)
        acc = ext[:, sl]
        span = 1
        while span < w:
            acc = acc + pltpu.roll(acc, span, 0)
            span *= 2
        cnt = jnp.minimum(pos + 1, w).astype(_F32)
        d_ref[:, sl] = (acc[HALO_ROWS:] / cnt - u[:, sl]).astype(d_ref.dtype)


def _poolwin(u, hist, tm, tiles_per_seq, start_pos):
    t = u.shape[0]
    halo_blocks = max(tm // HALO_ROWS, 1)
    kern = functools.partial(_poolwin_kernel, tm=tm, tiles_per_seq=tiles_per_seq, start_pos=start_pos)
    return pl.pallas_call(
        kern,
        grid=(t // tm,),
        in_specs=[
            pl.BlockSpec((tm, POOL_WIDTH), lambda i: (i, 0)),
            pl.BlockSpec((HALO_ROWS, POOL_WIDTH),
                         lambda i: (jnp.where(i % tiles_per_seq == 0, 0, i * halo_blocks - 1), 0)),
            pl.BlockSpec((None, HALO_ROWS, POOL_WIDTH), lambda i: (i // tiles_per_seq, 0, 0)),
        ],
        out_specs=pl.BlockSpec((tm, POOL_WIDTH), lambda i: (i, 0)),
        out_shape=jax.ShapeDtypeStruct((t, POOL_WIDTH), _BF16),
        compiler_params=_params(("arbitrary",)),
        name="poolwin",
    )(u, u, hist)


def _mix_kernel(x_ref, a_ref, d_ref, wp_ref, ps_ref, wo_ref, g2_ref, wrh_ref, wrl_ref, br_ref,
                cin_ref, x1_ref, h2_ref, route_ref, rect_ref, cout_ref, carry_ref, *, tm):
    i = pl.program_id(0)

    @pl.when(i == 0)
    def _():
        carry_ref[...] = cin_ref[...]

    d = d_ref[...]
    gw = POOL_GROUP_WIDTH
    py = jnp.concatenate([_dot(d[:, g * gw:(g + 1) * gw], wp_ref[g]) for g in range(len(POOL_WINDOWS))],
                         axis=1) * ps_ref[...]
    x1 = (x_ref[...] + _dot(a_ref[...], wo_ref[0:ATTN_WIDTH, :])
          + _dot(py.astype(_BF16), wo_ref[ATTN_WIDTH:, :]))
    x1_ref[...] = x1
    ms = jnp.mean(x1 * x1, axis=-1, keepdims=True)
    h2 = x1 * lax.rsqrt(ms + RMS_EPS) * g2_ref[...]
    h2_ref[...] = h2

    hh, hl = _split_bf16(h2)
    logits = _dot(hh, wrh_ref[...]) + _dot(hl, wrh_ref[...]) + _dot(hh, wrl_ref[...]) + br_ref[...]
    lane = lax.broadcasted_iota(_I32, (1, ROUTE_LANES), 1)
    big = jnp.int32(ROUTE_LANES)
    is_coarse = (lane >= N_EXPERTS) & (lane < N_EXPERTS + N_EXPERT_GROUPS)
    lc = jnp.where(is_coarse, logits, NEG_INF)
    mc = jnp.max(lc, axis=-1, keepdims=True)
    g_idx = jnp.min(jnp.where(lc == mc, lane, big), axis=-1, keepdims=True) - N_EXPERTS
    p_g = 1.0 / jnp.sum(jnp.exp(lc - mc), axis=-1, keepdims=True)
    in_group = (lane < N_EXPERTS) & ((lane >> (EXPERTS_PER_GROUP.bit_length() - 1)) == g_idx)
    lf = jnp.where(in_group, logits, NEG_INF)
    v1 = jnp.max(lf, axis=-1, keepdims=True)
    i1 = jnp.min(jnp.where(lf == v1, lane, big), axis=-1, keepdims=True)
    lf2 = jnp.where(lane == i1, NEG_INF, lf)
    v2 = jnp.max(lf2, axis=-1, keepdims=True)
    i2 = jnp.min(jnp.where(lf2 == v2, lane, big), axis=-1, keepdims=True)
    e21 = jnp.exp(v2 - v1)
    w1 = p_g / (1.0 + e21)
    w2 = p_g * e21 / (1.0 + e21)

    oh1 = lane == i1
    oh2 = lane == i2
    onehot = (oh1 | oh2).astype(_F32)
    r = lax.broadcasted_iota(_I32, (tm, tm), 0)
    col = lax.broadcasted_iota(_I32, (tm, tm), 1)
    lower = (col < r).astype(_BF16)
    before = _dot(lower, onehot.astype(_BF16)) + carry_ref[...]
    rank1 = jnp.sum(jnp.where(oh1, before, 0.0), axis=-1, keepdims=True)
    rank2 = jnp.sum(jnp.where(oh2, before, 0.0), axis=-1, keepdims=True)
    carry = carry_ref[...] + jnp.sum(onehot, axis=0, keepdims=True)
    carry_ref[...] = carry
    cout_ref[...] = carry

    rec = jnp.zeros((tm, ROUTE_LANES), _F32)
    for k, val in enumerate((i1.astype(_F32), i2.astype(_F32), w1, w2, rank1, rank2)):
        rec = jnp.where(lane == k, val, rec)
    route_ref[...] = rec
    rect_ref[...] = rec.T[0:REC_ROWS, :].astype(_I32)


def _mix(x, a, d, wp_bf, ps, wo_bf, g2, wrh, wrl, br, counts_in, tm):
    t = x.shape[0]
    row = lambda i: (i, 0)
    const = lambda i: (0, 0)
    kern = functools.partial(_mix_kernel, tm=tm)
    return pl.pallas_call(
        kern,
        grid=(t // tm,),
        in_specs=[
            pl.BlockSpec((tm, D_MODEL), row),
            pl.BlockSpec((tm, ATTN_WIDTH), row),
            pl.BlockSpec((tm, POOL_WIDTH), row),
            _whole_vmem(),
            pl.BlockSpec((1, POOL_WIDTH), const),
            _whole_vmem(),
            pl.BlockSpec((1, D_MODEL), const),
            _whole_vmem(),
            _whole_vmem(),
            pl.BlockSpec((1, ROUTE_LANES), const),
            pl.BlockSpec((1, ROUTE_LANES), const),
        ],
        out_specs=[
            pl.BlockSpec((tm, D_MODEL), row),
            pl.BlockSpec((tm, D_MODEL), row),
            pl.BlockSpec((tm, ROUTE_LANES), row),
            pl.BlockSpec((None, REC_ROWS, tm), lambda i: (i, 0, 0)),
            pl.BlockSpec((1, ROUTE_LANES), const),
        ],
        out_shape=[
            jax.ShapeDtypeStruct((t, D_MODEL), _F32),
            jax.ShapeDtypeStruct((t, D_MODEL), _F32),
            jax.ShapeDtypeStruct((t, ROUTE_LANES), _F32),
            jax.ShapeDtypeStruct((t // tm, REC_ROWS, tm), _I32),
            jax.ShapeDtypeStruct((1, ROUTE_LANES), _F32),
        ],
        scratch_shapes=[pltpu.VMEM((1, ROUTE_LANES), _F32)],
        compiler_params=_params(("arbitrary",)),
        name="mix",
    )(x, a, d, wp_bf, ps, wo_bf, g2, wrh, wrl, br, counts_in)


def _row_copy(src, dst, src_row, dst_row, sem):
    return pltpu.make_async_copy(src.at[pl.ds(src_row, 1), :], dst.at[pl.ds(dst_row, 1), :], sem)


def _dispatch_kernel(off_ref, zstart_ref, zpad_ref, rec_ref, h_ref, *rest, tm, zero_rows):
    if zero_rows:
        xs_ref, sem, zsem, zbuf = rest
    else:
        _, xs_ref, sem = rest
    i = pl.program_id(0)

    if zero_rows:
        def zero_fill(wait):
            def go(cp):
                cp.wait() if wait else cp.start()

            def body(e, c):
                def row(r, c2):
                    go(_row_copy(zbuf, xs_ref, 0, zstart_ref[e] + r, zsem))
                    return c2

                return lax.fori_loop(0, zpad_ref[e], row, c)

            lax.fori_loop(0, N_EXPERTS, body, 0)

            def tile(t, c):
                start = pl.multiple_of(t * zero_rows, zero_rows)
                go(pltpu.make_async_copy(zbuf, xs_ref.at[pl.ds(start, zero_rows), :], zsem))
                return c

            lax.fori_loop(zstart_ref[N_EXPERTS], xs_ref.shape[0] // zero_rows, tile, 0)

        @pl.when(i == 0)
        def _():
            zbuf[...] = jnp.zeros_like(zbuf)
            zero_fill(False)
            zero_fill(True)

    def body(r, c):
        for k in range(2):
            dst = off_ref[rec_ref[0, k, r]] + rec_ref[0, 4 + k, r]
            _row_copy(h_ref, xs_ref, r, dst, sem).start()
        return c

    lax.fori_loop(0, tm, body, 0, unroll=DMA_UNROLL)
    for _ in range(2):
        pltpu.make_async_copy(h_ref, xs_ref.at[pl.ds(0, tm), :], sem).wait()


def _dispatch(off, zstart, zpad, rect, h2, xs, n_rows, zero_rows):
    n_tiles, _, tm = rect.shape
    kern = functools.partial(_dispatch_kernel, tm=tm, zero_rows=zero_rows)
    in_specs = [
        pl.BlockSpec((1, REC_ROWS, tm), lambda i, o, z, p: (i, 0, 0), memory_space=pltpu.SMEM),
        pl.BlockSpec((tm, D_MODEL), lambda i, o, z, p: (i, 0)),
    ]
    scratch = [pltpu.SemaphoreType.DMA(())]
    args = [off, zstart, zpad, rect, h2]
    aliases = {}
    if zero_rows:
        scratch += [pltpu.SemaphoreType.DMA(()), pltpu.VMEM((zero_rows, D_MODEL), _F32)]
    else:
        in_specs.append(pl.BlockSpec(memory_space=pl.ANY))
        args.append(xs)
        aliases = {5: 0}
    grid_spec = pltpu.PrefetchScalarGridSpec(
        num_scalar_prefetch=3, grid=(n_tiles,), in_specs=in_specs,
        out_specs=pl.BlockSpec(memory_space=pl.ANY), scratch_shapes=scratch)
    return pl.pallas_call(
        kern,
        grid_spec=grid_spec,
        out_shape=jax.ShapeDtypeStruct((n_rows, D_MODEL), _F32),
        input_output_aliases=aliases,
        compiler_params=_params(("arbitrary",)),
        name="dispatch",
    )(*args)


def _experts_kernel(texp_ref, nused_ref, x_ref, wg_ref, wu_ref, wd_ref, y_ref, wgb, wub, wdb):
    i = pl.program_id(0)
    new_expert = (i == 0) | (texp_ref[i] != texp_ref[jnp.maximum(i - 1, 0)])

    @pl.when(new_expert)
    def _():
        wgb[...] = wg_ref[...].astype(_BF16)
        wub[...] = wu_ref[...].astype(_BF16)
        wdb[...] = wd_ref[...].astype(_BF16)

    @pl.when(i < nused_ref[0])
    def _():
        x = x_ref[...].astype(_BF16)
        hg = _dot(x, wgb[...])
        hu = _dot(x, wub[...])
        act = (hg * jax.nn.sigmoid(hg) * hu).astype(_BF16)
        y_ref[...] = _dot(act, wdb[...])

    @pl.when(i >= nused_ref[0])
    def _():
        y_ref[...] = jnp.zeros_like(y_ref)


def _experts(tile_expert, n_used, xs, w_gate, w_up, w_down, tm):
    n_tiles = tile_expert.shape[0]
    w_in_spec = pl.BlockSpec((None, D_MODEL, D_EXPERT), lambda i, te, nu: (te[i], 0, 0))
    grid_spec = pltpu.PrefetchScalarGridSpec(
        num_scalar_prefetch=2,
        grid=(n_tiles,),
        in_specs=[
            pl.BlockSpec((tm, D_MODEL), lambda i, te, nu: (jnp.minimum(i, nu[0] - 1), 0)),
            w_in_spec, w_in_spec,
            pl.BlockSpec((None, D_EXPERT, D_MODEL), lambda i, te, nu: (te[i], 0, 0)),
        ],
        out_specs=pl.BlockSpec((tm, D_MODEL), lambda i, te, nu: (i, 0)),
        scratch_shapes=[pltpu.VMEM((D_MODEL, D_EXPERT), _BF16), pltpu.VMEM((D_MODEL, D_EXPERT), _BF16),
                        pltpu.VMEM((D_EXPERT, D_MODEL), _BF16)],
    )
    return pl.pallas_call(
        _experts_kernel,
        grid_spec=grid_spec,
        out_shape=jax.ShapeDtypeStruct((n_tiles * tm, D_MODEL), _F32),
        compiler_params=_params(("arbitrary",)),
        name="experts",
    )(tile_expert, n_used, xs, w_gate, w_up, w_down)


def _combine_kernel(off_ref, rec_ref, recn_ref, x1_ref, route_ref, y_hbm, o_ref, r0, r1, sem, *, tm):
    i = pl.program_id(0)
    n = pl.num_programs(0)
    bufs = (r0, r1)

    def issue(rec, slot):
        def body(r, c):
            for k in range(2):
                src = off_ref[rec[0, k, r]] + rec[0, 4 + k, r]
                _row_copy(y_hbm, bufs[k].at[slot], src, r, sem.at[slot, k]).start()
            return c

        lax.fori_loop(0, tm, body, 0, unroll=DMA_UNROLL)

    @pl.when(i == 0)
    def _():
        issue(rec_ref, 0)

    @pl.when(i + 1 < n)
    def _():
        issue(recn_ref, (i + 1) % 2)

    slot = i % 2
    for k in range(2):
        pltpu.make_async_copy(y_hbm.at[pl.ds(0, tm), :], bufs[k].at[slot], sem.at[slot, k]).wait()
    rec = route_ref[...]
    o_ref[...] = x1_ref[...] + rec[:, 2:3] * r0[slot] + rec[:, 3:4] * r1[slot]


def _combine(off, rect, x1, route, y_sorted):
    n_tiles, _, tm = rect.shape
    t = x1.shape[0]
    kern = functools.partial(_combine_kernel, tm=tm)
    rec_spec = lambda nxt: pl.BlockSpec(
        (1, REC_ROWS, tm), lambda i, o: (jnp.minimum(i + nxt, n_tiles - 1), 0, 0), memory_space=pltpu.SMEM)
    grid_spec = pltpu.PrefetchScalarGridSpec(
        num_scalar_prefetch=1,
        grid=(n_tiles,),
        in_specs=[
            rec_spec(0), rec_spec(1),
            pl.BlockSpec((tm, D_MODEL), lambda i, o: (i, 0)),
            pl.BlockSpec((tm, ROUTE_LANES), lambda i, o: (i, 0)),
            pl.BlockSpec(memory_space=pl.ANY),
        ],
        out_specs=pl.BlockSpec((tm, D_MODEL), lambda i, o: (i, 0)),
        scratch_shapes=[pltpu.VMEM((2, tm, D_MODEL), _F32), pltpu.VMEM((2, tm, D_MODEL), _F32),
                        pltpu.SemaphoreType.DMA((2, 2))],
    )
    return pl.pallas_call(
        kern,
        grid_spec=grid_spec,
        out_shape=jax.ShapeDtypeStruct((t, D_MODEL), _F32),
        compiler_params=_params(("arbitrary",)),
        name="combine",
    )(off, rect, rect, x1, route, y_sorted)


def _rope_tables(pos):
    inv = ROPE_THETA ** (-jnp.arange(0, HEAD_DIM, 2, dtype=_F32) / HEAD_DIM)
    ang = pos.astype(_F32)[:, None] * inv[None, :]
    cos = jnp.cos(ang)
    sin = jnp.sin(ang)
    return jnp.tile(cos, (1, 4)), jnp.tile(jnp.concatenate([-sin, sin], axis=1), (1, 2))


def _head_sum_matrices():
    lane_head = jnp.arange(ATTN_WIDTH) // HEAD_DIM
    cols = jnp.arange(LANES)
    esum = (lane_head[:, None] == cols[None, :]).astype(_F32) / HEAD_DIM
    eexp = (cols[:, None] == lane_head[None, :]).astype(_F32)
    return esum.astype(_BF16), jnp.concatenate([eexp, eexp], axis=0).astype(_BF16)


def kernel(x_prompt, x_sample, cache_k, cache_v, state_pool, page_table, norm1_g, w_in, q_norm_g, k_norm_g,
           lambda_q1, lambda_k1, lambda_q2, lambda_k2, subln_g, w_pool, pool_scale, w_out, norm2_g, w_coarse,
           b_coarse, w_fine, b_fine, w_gate, w_up, w_down):
    depth = w_in.shape[0]
    assert depth == 1
    l = 0
    lam_init = 0.8 - 0.6 * math.exp(-0.3 * l)
    bp, sp, _ = x_prompt.shape
    bs, ts, _ = x_sample.shape
    tp, tsamp = bp * sp, bs * ts
    n_pages = page_table.shape[1]
    past = n_pages * PAGE_SIZE
    n_hc = 2 * N_HEADS

    w_in_bf = w_in[l].astype(_BF16)
    w_out_bf = w_out[l].astype(_BF16)
    w_pool_bf = w_pool[l].astype(_BF16)
    g1 = norm1_g[l][None, :]
    g2 = norm2_g[l][None, :]
    gq = jnp.tile(q_norm_g[l], n_hc)[None, :]
    gk = jnp.tile(k_norm_g[l], n_hc)[None, :]
    sg = subln_g[l][None, :]
    ps = pool_scale[l][None, :]
    lam_p = jnp.stack([lambda_q1[l], lambda_k1[l], lambda_q2[l], lambda_k2[l]])
    esum, eexp = _head_sum_matrices()
    pad = ROUTE_LANES - N_EXPERTS - N_EXPERT_GROUPS
    w_fine_flat = jnp.transpose(w_fine[l], (1, 0, 2)).reshape(D_MODEL, N_EXPERTS)
    w_route = jnp.concatenate([w_fine_flat, w_coarse[l], jnp.zeros((D_MODEL, pad), _F32)], axis=1)
    wrh, wrl = _split_bf16(w_route)
    b_route = jnp.concatenate([b_fine[l].reshape(-1), b_coarse[l], jnp.zeros((pad,), _F32)])[None, :]

    cos_p, sin_p = _rope_tables(jnp.arange(sp, dtype=jnp.int32))
    cos_s, sin_s = _rope_tables(past + jnp.arange(ts, dtype=jnp.int32))
    cos_s, sin_s = jnp.tile(cos_s, (bs, 1)), jnp.tile(sin_s, (bs, 1))
    xp2 = x_prompt.reshape(tp, D_MODEL)
    xs2 = x_sample.reshape(tsamp, D_MODEL)
    qp, kpt, kpt_bf, vp, vp_bf, up = _proj(xp2, g1, w_in_bf, gq, gk, cos_p, sin_p, esum, eexp, PROJ_TILE, sp)
    qs, ks, _, vs, _, us = _proj(xs2, g1, w_in_bf, gq, gk, cos_s, sin_s, esum, eexp, tsamp)

    shp = (bp, sp, ATTN_WIDTH)
    a_p = _attn_prompt(lam_p, sg, qp.reshape(shp), kpt_bf, vp_bf.reshape(shp), lam_init)
    a_p = a_p.reshape(tp, ATTN_WIDTH)

    q4 = qs.astype(_F32).reshape(bs, 1, 1, ts, ATTN_WIDTH)
    lane_hc = jnp.arange(ATTN_WIDTH) // HEAD_DIM
    head = jnp.arange(N_HEADS)[:, None, None, None]
    comp = jnp.arange(2)[None, :, None, None]
    keep = lane_hc[None, None, None, :] == head * 2 + comp
    qrep = jnp.where(keep[None], q4, 0.0).reshape(bs, QROWS, ATTN_WIDTH)
    new_rows = 8
    kn = jnp.pad(ks.reshape(bs, ts, ATTN_WIDTH), ((0, 0), (0, new_rows - ts), (0, 0)))
    vn = jnp.pad(vs.reshape(bs, ts, ATTN_WIDTH), ((0, 0), (0, new_rows - ts), (0, 0)))
    n_phys = cache_k.shape[1]
    ck = jnp.transpose(cache_k[l], (0, 2, 3, 1)).reshape(n_phys, ATTN_WIDTH, PAGE_SIZE)
    cv = cache_v[l].reshape(n_phys, PAGE_SIZE * N_HEADS, V_HEAD_DIM)
    a_s = _attn_sample(page_table, lam_p, sg, qrep, kn, vn, ck, cv, ts, lam_init)
    a_s = a_s.reshape(bs, N_HEADS, 2, ts, V_HEAD_DIM)[:, :, 0]
    a_s = a_s.transpose(0, 2, 1, 3).reshape(tsamp, ATTN_WIDTH).astype(_BF16)

    hist_p = jnp.zeros((bp, HALO_ROWS, POOL_WIDTH), _F32)
    d_p = _poolwin(up, hist_p, MIX_TILE, sp // MIX_TILE, 0)
    seq_rows = 8
    us_pad = jnp.pad(us.reshape(bs, ts, POOL_WIDTH), ((0, 0), (0, seq_rows - ts), (0, 0)))
    hist_s = jnp.pad(state_pool[l], ((0, 0), (HALO_ROWS - POOL_HIST, 0), (0, 0)))
    d_s = _poolwin(us_pad.reshape(bs * seq_rows, POOL_WIDTH), hist_s, seq_rows, 1, past)
    d_s = d_s.reshape(bs, seq_rows, POOL_WIDTH)[:, :ts].reshape(tsamp, POOL_WIDTH)

    counts0 = jnp.zeros((1, ROUTE_LANES), _F32)
    x1_p, h2_p, route_p, rect_p, counts_p = _mix(xp2, a_p, d_p, w_pool_bf, ps, w_out_bf, g2, wrh, wrl, b_route,
                                                 counts0, MIX_TILE)
    x1_s, h2_s, route_s, rect_s, counts = _mix(xs2, a_s, d_s, w_pool_bf, ps, w_out_bf, g2, wrh, wrl, b_route,
                                               counts_p, tsamp)

    tm = EXPERT_TILE
    t_all = tp + tsamp
    n_tiles = (2 * t_all + N_EXPERTS * (tm - 1) + tm - 1) // tm
    cnt = counts[0, :N_EXPERTS].astype(jnp.int32)
    padded = (cnt + tm - 1) // tm * tm
    ends = jnp.cumsum(padded)
    off = ends - padded
    n_used = (ends[-1] // tm).astype(jnp.int32)
    tile_ids = jnp.arange(n_tiles, dtype=jnp.int32)
    te = jnp.sum((ends[None, :] <= (tile_ids * tm)[:, None]).astype(jnp.int32), axis=1)
    te = jnp.minimum(te, N_EXPERTS - 1)
    last = jnp.sum(jnp.where(tile_ids == n_used - 1, te, 0))
    tile_expert = jnp.where(tile_ids < n_used, te, last)

    n_rows = n_tiles * tm
    zstart = off + counts_p[0, :N_EXPERTS].astype(jnp.int32)
    zinfo = (jnp.concatenate([zstart, n_used[None]]), jnp.concatenate([ends - zstart, jnp.zeros((1,), jnp.int32)]))
    xs = _dispatch(off, *zinfo, rect_p, h2_p, None, n_rows, tm)
    xs = _dispatch(off, *zinfo, rect_s, h2_s, xs, n_rows, 0)
    y_sorted = _experts(tile_expert, n_used.reshape(1), xs, w_gate[l], w_up[l], w_down[l], tm)
    y_p = _combine(off, rect_p, x1_p, route_p, y_sorted)
    y_s = _combine(off, rect_s, x1_s, route_s, y_sorted)

    new_k_p = jnp.transpose(kpt.reshape(bp, n_hc, HEAD_DIM, sp), (0, 3, 1, 2))
    new_pool_p = up.reshape(bp, sp, POOL_WIDTH)[:, sp - POOL_HIST:]
    new_pool_s = jnp.concatenate([state_pool[l], us.reshape(bs, ts, POOL_WIDTH)], axis=1)[:, -POOL_HIST:]
    return (y_p.reshape(bp, sp, D_MODEL),
            y_s.reshape(bs, ts, D_MODEL),
            new_k_p[None],
            vp.reshape(1, bp, sp, N_HEADS, V_HEAD_DIM),
            new_pool_p[None],
            ks.reshape(1, bs, ts, n_hc, HEAD_DIM),
            vs.reshape(1, bs, ts, N_HEADS, V_HEAD_DIM),
            new_pool_s[None])
```

```python
import functools
import math

import jax
import jax.numpy as jnp
from jax import lax
from jax.experimental import pallas as pl
from jax.experimental.pallas import tpu as pltpu

_F32 = jnp.float32
_BF16 = jnp.bfloat16
_I32 = jnp.int32

D_MODEL = 2048
ATTN_WIDTH = 1024
POOL_WIDTH = 1024
HEAD_DIM = 64
N_HEADS = 8
V_HEAD_DIM = 128
POOL_WINDOWS = (2, 4, 8, 16)
POOL_GROUP_WIDTH = 256
POOL_HIST = 15
HALO_ROWS = 16
N_EXPERTS = 32
EXPERTS_PER_GROUP = 8
N_EXPERT_GROUPS = 4
D_EXPERT = 256
ROPE_THETA = 10000.0
RMS_EPS = 1e-6
NEG_INF = -1e30
PAGE_SIZE = 128
LANES = 128
ROUTE_LANES = 128
REC_ROWS = 8
VMEM_LIMIT = 56 * 1024 * 1024

PROJ_TILE = 256
ATTN_TILE = 512
ATTN_WIDE = 4
MIX_TILE = 256
EXPERT_TILE = 256
PAGES_PER_STEP = 8
PAGE_RING_STEPS = 3
QROWS = 64
DMA_UNROLL = 8


def _dot(a, b):
    return jnp.dot(a, b, preferred_element_type=_F32)


def _split_bf16(x):
    hi = x.astype(_BF16)
    lo = (x - hi.astype(_F32)).astype(_BF16)
    return hi, lo


def _params(sem):
    return pltpu.CompilerParams(dimension_semantics=sem, vmem_limit_bytes=VMEM_LIMIT)


ROW_CHUNKS = D_MODEL // LANES


def _load_token_major(ref, n):
    return jnp.concatenate([ref[pl.ds(c, n, stride=ROW_CHUNKS), :] for c in range(ROW_CHUNKS)], axis=1)


def _store_token_major(ref, x):
    n = x.shape[0]
    for c in range(ROW_CHUNKS):
        ref[pl.ds(c, n, stride=ROW_CHUNKS), :] = x[:, c * LANES:(c + 1) * LANES]


def _whole_vmem():
    return pl.BlockSpec(memory_space=pltpu.VMEM)


def _proj_kernel(x_ref, g1_ref, w_ref, gq_ref, gk_ref, cos_ref, sin_ref, esum_ref, eexp_ref,
                 q_ref, kf_ref, kb_ref, vf_ref, vb_ref, u_ref, *, k_transposed):
    x = x_ref[...]
    ms = jnp.mean(x * x, axis=-1, keepdims=True)
    h = (x * lax.rsqrt(ms + RMS_EPS) * g1_ref[...]).astype(_BF16)
    cos = cos_ref[...]
    sin = sin_ref[...]
    lane = lax.broadcasted_iota(_I32, (1, LANES), 1)
    upper = (lane & (HEAD_DIM - 1)) >= HEAD_DIM // 2

    def normed_rope(z, g_ref, outs, transposed):
        zh, zl = _split_bf16(z * z)
        msq = _dot(zh, esum_ref[...]) + _dot(zl, esum_ref[...])
        rh, rl = _split_bf16(lax.rsqrt(msq + RMS_EPS))
        rb = _dot(jnp.concatenate([rh, rl], axis=1), eexp_ref[...])
        n = z * rb * g_ref[...]
        for j in range(ATTN_WIDTH // LANES):
            sl = slice(j * LANES, (j + 1) * LANES)
            nj = n[:, sl]
            swapped = jnp.where(upper, pltpu.roll(nj, HEAD_DIM // 2, 1),
                                pltpu.roll(nj, LANES - HEAD_DIM // 2, 1))
            o = nj * cos + swapped * sin
            if transposed:
                ot = o.T
                for ref, _ in outs:
                    ref[sl, :] = ot.astype(ref.dtype)
            else:
                for ref, scale in outs:
                    ref[:, sl] = (o * scale).astype(ref.dtype)

    a = ATTN_WIDTH
    normed_rope(_dot(h, w_ref[:, 0:a]), gq_ref, ((q_ref, HEAD_DIM ** -0.5),), False)
    normed_rope(_dot(h, w_ref[:, a:2 * a]), gk_ref, ((kf_ref, 1.0), (kb_ref, 1.0)), k_transposed)
    zv = _dot(h, w_ref[:, 2 * a:3 * a])
    vf_ref[...] = zv
    vb_ref[...] = zv.astype(_BF16)
    u_ref[...] = _dot(h, w_ref[:, 3 * a:])


def _proj(x, g1, w_bf, gq, gk, cos, sin, esum, eexp, tm, seq_len=None):
    t = x.shape[0]
    n_pos_tiles = cos.shape[0] // tm
    row = lambda i: (i, 0)
    const = lambda i: (0, 0)
    tok_spec = lambda w: pl.BlockSpec((tm, w), row)
    a = ATTN_WIDTH
    if seq_len is None:
        k_spec, k_shape = tok_spec(a), (t, a)
    else:
        tiles_per_seq = seq_len // tm
        k_spec = pl.BlockSpec((None, a, tm), lambda i: (i // tiles_per_seq, 0, i % tiles_per_seq))
        k_shape = (t // seq_len, a, seq_len)
    return pl.pallas_call(
        functools.partial(_proj_kernel, k_transposed=seq_len is not None),
        grid=(t // tm,),
        in_specs=[
            tok_spec(D_MODEL),
            pl.BlockSpec((1, D_MODEL), const),
            _whole_vmem(),
            pl.BlockSpec((1, a), const),
            pl.BlockSpec((1, a), const),
            pl.BlockSpec((tm, LANES), lambda i: (i % n_pos_tiles, 0)),
            pl.BlockSpec((tm, LANES), lambda i: (i % n_pos_tiles, 0)),
            _whole_vmem(),
            _whole_vmem(),
        ],
        out_specs=[tok_spec(a), k_spec, k_spec, tok_spec(a), tok_spec(a), tok_spec(POOL_WIDTH)],
        out_shape=[
            jax.ShapeDtypeStruct((t, a), _BF16),
            jax.ShapeDtypeStruct(k_shape, _F32),
            jax.ShapeDtypeStruct(k_shape, _BF16),
            jax.ShapeDtypeStruct((t, a), _F32),
            jax.ShapeDtypeStruct((t, a), _BF16),
            jax.ShapeDtypeStruct((t, POOL_WIDTH), _F32),
        ],
        compiler_params=_params(("arbitrary",)),
        name="proj",
    )(x, g1, w_bf, gq, gk, cos, sin, esum, eexp)


def _lambda(lam_ref, lam_init):
    lp = lam_ref[...]
    s1 = jnp.sum(lp[0:1] * lp[1:2], axis=-1, keepdims=True)
    s2 = jnp.sum(lp[2:3] * lp[3:4], axis=-1, keepdims=True)
    return jnp.exp(s1) - jnp.exp(s2) + lam_init


def _sub_norm(o, sg, lam_init):
    ms = jnp.mean(o * o, axis=-1, keepdims=True)
    return o * lax.rsqrt(ms + RMS_EPS) * sg * (1.0 - lam_init)


def _attn_prompt_kernel(lam_ref, sg_ref, q_ref, kt_ref, v_ref, o_ref, *, tile, wide, lam_init):
    qi = pl.program_id(2)
    q = q_ref[...]
    lane = lax.broadcasted_iota(_I32, (1, LANES), 1)
    zero = jnp.zeros_like(q)
    qc = (jnp.where(lane < HEAD_DIM, q, zero), jnp.where(lane >= HEAD_DIM, q, zero))

    def update(s, state, vblk):
        m, l, acc = state
        m_new = jnp.maximum(m, jnp.max(s, axis=-1, keepdims=True))
        alpha = jnp.exp(m - m_new)
        p = jnp.exp(s - m_new)
        l = alpha * l + jnp.sum(p, axis=-1, keepdims=True)
        acc = alpha * acc + _dot(p.astype(_BF16), vblk)
        return m_new, l, acc

    def block(first, carry, masked, width):
        start = pl.multiple_of(first * tile, tile)
        ktblk = kt_ref[:, pl.ds(start, width)]
        vblk = v_ref[pl.ds(start, width), :]
        out = []
        for c in range(2):
            s = _dot(qc[c], ktblk)
            if masked:
                r = lax.broadcasted_iota(_I32, (tile, width), 0)
                col = lax.broadcasted_iota(_I32, (tile, width), 1)
                s = jnp.where(col <= r + (width - tile), s, NEG_INF)
            out.append(update(s, carry[c], vblk))
        return tuple(out)

    init_one = (jnp.full((tile, 1), NEG_INF, _F32), jnp.zeros((tile, 1), _F32),
                jnp.zeros((tile, V_HEAD_DIM), _F32))
    n_full = qi // wide
    carry = lax.fori_loop(0, n_full, lambda j, c: block(j * wide, c, False, wide * tile), (init_one, init_one))
    lam = _lambda(lam_ref, lam_init)
    for last in range(1, wide + 1):
        @pl.when(qi + 1 - n_full * wide == last)
        def _(last=last):
            (_, l0, a0), (_, l1, a1) = block(n_full * wide, carry, True, last * tile)
            o = a0 / l0 - lam * (a1 / l1)
            o_ref[...] = _sub_norm(o, sg_ref[...], lam_init).astype(o_ref.dtype)


def _attn_prompt(lam_p, sg, q, kt, v, lam_init):
    b, s, _ = q.shape
    tile = ATTN_TILE
    kern = functools.partial(_attn_prompt_kernel, tile=tile, wide=ATTN_WIDE, lam_init=lam_init)
    return pl.pallas_call(
        kern,
        grid=(b, N_HEADS, s // tile),
        in_specs=[
            pl.BlockSpec((4, HEAD_DIM), lambda bi, h, i: (0, 0)),
            pl.BlockSpec((1, V_HEAD_DIM), lambda bi, h, i: (0, 0)),
            pl.BlockSpec((None, tile, LANES), lambda bi, h, i: (bi, i, h)),
            pl.BlockSpec((None, LANES, s), lambda bi, h, i: (bi, h, 0)),
            pl.BlockSpec((None, s, LANES), lambda bi, h, i: (bi, 0, h)),
        ],
        out_specs=pl.BlockSpec((None, tile, LANES), lambda bi, h, i: (bi, i, h)),
        out_shape=jax.ShapeDtypeStruct((b, s, ATTN_WIDTH), _BF16),
        compiler_params=_params(("arbitrary", "arbitrary", "arbitrary")),
        name="attn_prompt",
    )(lam_p, sg, q, kt, v)


def _attn_sample_kernel(pt_ref, lam_ref, sg_ref, q_ref, kn_ref, vn_ref, *rest, n_tok, lam_init, n_batch,
                        steps_per_b):
    ck_hbm, cv_hbm, o_ref, kbuf, vbuf, sem, m_ref, l_ref, acc_ref = rest
    npg = PAGES_PER_STEP
    n_steps = n_batch * steps_per_b
    j = pl.program_id(1)
    step = pl.program_id(0) * steps_per_b + j
    ahead = PAGE_RING_STEPS - 1

    def page_copies(s):
        if isinstance(s, int):
            b_s, j_s, slot_s = s // steps_per_b, s % steps_per_b, s % PAGE_RING_STEPS
        else:
            assert steps_per_b & (steps_per_b - 1) == 0
            b_s = lax.shift_right_logical(s, steps_per_b.bit_length() - 1)
            j_s = s & (steps_per_b - 1)
            slot_s = lax.rem(s, PAGE_RING_STEPS)
        first_page = j_s * npg
        base = slot_s * npg
        out = []
        for p in range(npg):
            page = pt_ref[b_s, first_page + p]
            out.append(pltpu.make_async_copy(ck_hbm.at[page], kbuf.at[base + p], sem.at[0, base + p]))
            out.append(pltpu.make_async_copy(cv_hbm.at[page], vbuf.at[base + p], sem.at[1, base + p]))
        return out

    @pl.when(step == 0)
    def _():
        for s in range(min(ahead, n_steps)):
            for cp in page_copies(s):
                cp.start()

    @pl.when(step + ahead < n_steps)
    def _():
        for cp in page_copies(step + ahead):
            cp.start()

    for cp in page_copies(step):
        cp.wait()
    base = lax.rem(step, PAGE_RING_STEPS) * npg
    k_refs = [kbuf.at[base + p] for p in range(npg)]
    v_refs = [vbuf.at[base + p] for p in range(npg)]
    q = q_ref[...]
    rows_per_head = 2 * n_tok
    head_rows = lambda h: slice(h * rows_per_head, (h + 1) * rows_per_head)

    @pl.when(j == 0)
    def _():
        kn = kn_ref[...]
        vn = vn_ref[...]
        tok = lax.broadcasted_iota(_I32, (QROWS, 1), 0) & (n_tok - 1)
        ss = [jnp.where(tok >= jn, jnp.sum(q * kn[jn:jn + 1, :], axis=-1, keepdims=True), NEG_INF)
              for jn in range(n_tok)]
        m = functools.reduce(jnp.maximum, ss)
        ps = [jnp.exp(s - m) for s in ss]
        m_ref[...] = m
        l_ref[...] = functools.reduce(jnp.add, ps)
        for h in range(N_HEADS):
            lanes = slice(h * V_HEAD_DIM, (h + 1) * V_HEAD_DIM)
            acc_ref[head_rows(h), :] = functools.reduce(
                jnp.add, [ps[jn][head_rows(h)] * vn[jn:jn + 1, lanes] for jn in range(n_tok)])

    ss = [_dot(q, kr[...]) for kr in k_refs]
    m_old = m_ref[...]
    m_new = m_old
    for s in ss:
        m_new = jnp.maximum(m_new, jnp.max(s, axis=-1, keepdims=True))
    alpha = jnp.exp(m_old - m_new)
    ps = [jnp.exp(s - m_new) for s in ss]
    l = alpha * l_ref[...] + functools.reduce(jnp.add, [jnp.sum(p, axis=-1, keepdims=True) for p in ps])
    m_ref[...] = m_new
    l_ref[...] = l
    for h in range(N_HEADS):
        pv = functools.reduce(jnp.add, [
            _dot(p[head_rows(h)], vr[pl.ds(h, PAGE_SIZE, stride=N_HEADS), :]) for p, vr in zip(ps, v_refs)])
        acc_ref[head_rows(h), :] = alpha[head_rows(h)] * acc_ref[head_rows(h), :] + pv

    @pl.when(j == pl.num_programs(1) - 1)
    def _():
        o = acc_ref[...] / l
        lam = _lambda(lam_ref, lam_init)
        d = o - lam * pltpu.roll(o, QROWS - n_tok, 0)
        o_ref[...] = _sub_norm(d, sg_ref[...], lam_init)


def _attn_sample(page_table, lam_p, sg, qrep, kn, vn, cache_k, cache_v, n_tok, lam_init):
    nb, n_pages = page_table.shape
    npg = PAGES_PER_STEP
    steps = n_pages // npg
    width = ATTN_WIDTH
    assert QROWS == 2 * N_HEADS * n_tok and cache_k.shape[1:] == (width, PAGE_SIZE)
    assert cache_v.shape[1:] == (PAGE_SIZE * N_HEADS, V_HEAD_DIM)

    per_b = lambda rows: pl.BlockSpec((None, rows, width), lambda b, j, pt: (b, 0, 0))
    kern = functools.partial(_attn_sample_kernel, n_tok=n_tok, lam_init=lam_init, n_batch=nb, steps_per_b=steps)
    ring = PAGE_RING_STEPS * npg
    grid_spec = pltpu.PrefetchScalarGridSpec(
        num_scalar_prefetch=1,
        grid=(nb, steps),
        in_specs=[
            pl.BlockSpec((4, HEAD_DIM), lambda b, j, pt: (0, 0)),
            pl.BlockSpec((1, V_HEAD_DIM), lambda b, j, pt: (0, 0)),
            per_b(QROWS), per_b(kn.shape[1]), per_b(vn.shape[1]),
            pl.BlockSpec(memory_space=pl.ANY), pl.BlockSpec(memory_space=pl.ANY),
        ],
        out_specs=pl.BlockSpec((None, QROWS, V_HEAD_DIM), lambda b, j, pt: (b, 0, 0)),
        scratch_shapes=[pltpu.VMEM((ring, width, PAGE_SIZE), _F32),
                        pltpu.VMEM((ring, PAGE_SIZE * N_HEADS, V_HEAD_DIM), _F32),
                        pltpu.SemaphoreType.DMA((2, ring)),
                        pltpu.VMEM((QROWS, 1), _F32), pltpu.VMEM((QROWS, 1), _F32),
                        pltpu.VMEM((QROWS, V_HEAD_DIM), _F32)],
    )
    return pl.pallas_call(
        kern,
        grid_spec=grid_spec,
        out_shape=jax.ShapeDtypeStruct((nb, QROWS, V_HEAD_DIM), _F32),
        compiler_params=_params(("arbitrary", "arbitrary")),
        name="attn_sample",
    )(page_table, lam_p, sg, qrep, kn, vn, cache_k, cache_v)


def _poolwin_kernel(u_ref, halo_ref, hist_ref, d_ref, *, tm, tiles_per_seq, start_pos):
    i = pl.program_id(0)
    t_in_seq = i % tiles_per_seq
    u = u_ref[...]
    halo = jnp.where(t_in_seq == 0, hist_ref[...], halo_ref[...])
    ext = jnp.concatenate([halo, u], axis=0)
    pos = start_pos + t_in_seq * tm + lax.broadcasted_iota(_I32, (tm, 1), 0)
    for g, w in enumerate(POOL_WINDOWS):
        sl = slice(g * POOL_GROUP_WIDTH, (g + 1) * POOL_GROUP_WIDTH)
        acc = ext[:, sl]
        span = 1
        while span < w:
            acc = acc + pltpu.roll(acc, span, 0)
            span *= 2
        cnt = jnp.minimum(pos + 1, w).astype(_F32)
        d_ref[:, sl] = (acc[HALO_ROWS:] / cnt - u[:, sl]).astype(d_ref.dtype)


def _poolwin(u, hist, tm, tiles_per_seq, start_pos):
    t = u.shape[0]
    halo_blocks = max(tm // HALO_ROWS, 1)
    kern = functools.partial(_poolwin_kernel, tm=tm, tiles_per_seq=tiles_per_seq, start_pos=start_pos)
    return pl.pallas_call(
        kern,
        grid=(t // tm,),
        in_specs=[
            pl.BlockSpec((tm, POOL_WIDTH), lambda i: (i, 0)),
            pl.BlockSpec((HALO_ROWS, POOL_WIDTH),
                         lambda i: (jnp.where(i % tiles_per_seq == 0, 0, i * halo_blocks - 1), 0)),
            pl.BlockSpec((None, HALO_ROWS, POOL_WIDTH), lambda i: (i // tiles_per_seq, 0, 0)),
        ],
        out_specs=pl.BlockSpec((tm, POOL_WIDTH), lambda i: (i, 0)),
        out_shape=jax.ShapeDtypeStruct((t, POOL_WIDTH), _BF16),
        compiler_params=_params(("arbitrary",)),
        name="poolwin",
    )(u, u, hist)


def _mix_kernel(x_ref, a_ref, d_ref, wp_ref, ps_ref, wo_ref, g2_ref, wrh_ref, wrl_ref, br_ref,
                cin_ref, x1_ref, h2_ref, route_ref, rect_ref, cout_ref, carry_ref, *, tm):
    i = pl.program_id(0)

    @pl.when(i == 0)
    def _():
        carry_ref[...] = cin_ref[...]

    d = d_ref[...]
    gw = POOL_GROUP_WIDTH
    py = jnp.concatenate([_dot(d[:, g * gw:(g + 1) * gw], wp_ref[g]) for g in range(len(POOL_WINDOWS))],
                         axis=1) * ps_ref[...]
    x1 = (x_ref[...] + _dot(a_ref[...], wo_ref[0:ATTN_WIDTH, :])
          + _dot(py.astype(_BF16), wo_ref[ATTN_WIDTH:, :]))
    x1_ref[...] = x1
    ms = jnp.mean(x1 * x1, axis=-1, keepdims=True)
    h2 = x1 * lax.rsqrt(ms + RMS_EPS) * g2_ref[...]
    _store_token_major(h2_ref, h2)

    hh, hl = _split_bf16(h2)
    logits = _dot(hh, wrh_ref[...]) + _dot(hl, wrh_ref[...]) + _dot(hh, wrl_ref[...]) + br_ref[...]
    lane = lax.broadcasted_iota(_I32, (1, ROUTE_LANES), 1)
    big = jnp.int32(ROUTE_LANES)
    is_coarse = (lane >= N_EXPERTS) & (lane < N_EXPERTS + N_EXPERT_GROUPS)
    lc = jnp.where(is_coarse, logits, NEG_INF)
    mc = jnp.max(lc, axis=-1, keepdims=True)
    g_idx = jnp.min(jnp.where(lc == mc, lane, big), axis=-1, keepdims=True) - N_EXPERTS
    p_g = 1.0 / jnp.sum(jnp.exp(lc - mc), axis=-1, keepdims=True)
    in_group = (lane < N_EXPERTS) & ((lane >> (EXPERTS_PER_GROUP.bit_length() - 1)) == g_idx)
    lf = jnp.where(in_group, logits, NEG_INF)
    v1 = jnp.max(lf, axis=-1, keepdims=True)
    i1 = jnp.min(jnp.where(lf == v1, lane, big), axis=-1, keepdims=True)
    lf2 = jnp.where(lane == i1, NEG_INF, lf)
    v2 = jnp.max(lf2, axis=-1, keepdims=True)
    i2 = jnp.min(jnp.where(lf2 == v2, lane, big), axis=-1, keepdims=True)
    e21 = jnp.exp(v2 - v1)
    w1 = p_g / (1.0 + e21)
    w2 = p_g * e21 / (1.0 + e21)

    oh1 = lane == i1
    oh2 = lane == i2
    onehot = (oh1 | oh2).astype(_F32)
    r = lax.broadcasted_iota(_I32, (tm, tm), 0)
    col = lax.broadcasted_iota(_I32, (tm, tm), 1)
    lower = (col < r).astype(_BF16)
    before = _dot(lower, onehot.astype(_BF16)) + carry_ref[...]
    rank1 = jnp.sum(jnp.where(oh1, before, 0.0), axis=-1, keepdims=True)
    rank2 = jnp.sum(jnp.where(oh2, before, 0.0), axis=-1, keepdims=True)
    carry = carry_ref[...] + jnp.sum(onehot, axis=0, keepdims=True)
    carry_ref[...] = carry
    cout_ref[...] = carry

    rec = jnp.zeros((tm, ROUTE_LANES), _F32)
    for k, val in enumerate((i1.astype(_F32), i2.astype(_F32), w1, w2, rank1, rank2)):
        rec = jnp.where(lane == k, val, rec)
    route_ref[...] = rec
    rect_ref[...] = rec.T[0:REC_ROWS, :].astype(_I32)


def _mix(x, a, d, wp_bf, ps, wo_bf, g2, wrh, wrl, br, counts_in, tm):
    t = x.shape[0]
    row = lambda i: (i, 0)
    const = lambda i: (0, 0)
    kern = functools.partial(_mix_kernel, tm=tm)
    return pl.pallas_call(
        kern,
        grid=(t // tm,),
        in_specs=[
            pl.BlockSpec((tm, D_MODEL), row),
            pl.BlockSpec((tm, ATTN_WIDTH), row),
            pl.BlockSpec((tm, POOL_WIDTH), row),
            _whole_vmem(),
            pl.BlockSpec((1, POOL_WIDTH), const),
            _whole_vmem(),
            pl.BlockSpec((1, D_MODEL), const),
            _whole_vmem(),
            _whole_vmem(),
            pl.BlockSpec((1, ROUTE_LANES), const),
            pl.BlockSpec((1, ROUTE_LANES), const),
        ],
        out_specs=[
            pl.BlockSpec((tm, D_MODEL), row),
            pl.BlockSpec((tm * ROW_CHUNKS, LANES), row),
            pl.BlockSpec((tm, ROUTE_LANES), row),
            pl.BlockSpec((None, REC_ROWS, tm), lambda i: (i, 0, 0)),
            pl.BlockSpec((1, ROUTE_LANES), const),
        ],
        out_shape=[
            jax.ShapeDtypeStruct((t, D_MODEL), _F32),
            jax.ShapeDtypeStruct((t * ROW_CHUNKS, LANES), _F32),
            jax.ShapeDtypeStruct((t, ROUTE_LANES), _F32),
            jax.ShapeDtypeStruct((t // tm, REC_ROWS, tm), _I32),
            jax.ShapeDtypeStruct((1, ROUTE_LANES), _F32),
        ],
        scratch_shapes=[pltpu.VMEM((1, ROUTE_LANES), _F32)],
        compiler_params=_params(("arbitrary",)),
        name="mix",
    )(x, a, d, wp_bf, ps, wo_bf, g2, wrh, wrl, br, counts_in)


def _tokens(ref, first, n):
    return ref.at[pl.ds(pl.multiple_of(first * ROW_CHUNKS, ROW_CHUNKS), n * ROW_CHUNKS), :]


def _row_copy(src, dst, src_tok, dst_tok, sem):
    return pltpu.make_async_copy(_tokens(src, src_tok, 1), _tokens(dst, dst_tok, 1), sem)


def _dispatch_kernel(off_ref, zstart_ref, zpad_ref, rec_ref, h_ref, *rest, tm, zero_rows):
    if zero_rows:
        xs_ref, sem, zsem, zbuf = rest
    else:
        _, xs_ref, sem = rest
    i = pl.program_id(0)

    if zero_rows:
        def zero_fill(wait):
            def go(cp):
                cp.wait() if wait else cp.start()

            def body(e, c):
                cursor = zstart_ref[e]
                pad = zpad_ref[e]
                run = zero_rows // 2
                while run >= 1:
                    @pl.when((pad & run) != 0)
                    def _(cursor=cursor, run=run):
                        go(pltpu.make_async_copy(_tokens(zbuf, 0, run), _tokens(xs_ref, cursor, run), zsem))

                    cursor = cursor + (pad & run)
                    run //= 2
                return c

            lax.fori_loop(0, N_EXPERTS, body, 0)

            def tile(t, c):
                rows = zero_rows // 2
                go(pltpu.make_async_copy(_tokens(zbuf, 0, rows),
                                         _tokens(xs_ref, zstart_ref[N_EXPERTS] + t * rows, rows), zsem))
                return c

            lax.fori_loop(0, zpad_ref[N_EXPERTS], tile, 0)

        @pl.when(i == 0)
        def _():
            zbuf[...] = jnp.zeros_like(zbuf)
            zero_fill(False)
            zero_fill(True)

    def body(r, c):
        for k in range(2):
            dst = off_ref[rec_ref[0, k, r]] + rec_ref[0, 4 + k, r]
            _row_copy(h_ref, xs_ref, r, dst, sem).start()
        return c

    lax.fori_loop(0, tm, body, 0, unroll=DMA_UNROLL)
    for _ in range(2):
        pltpu.make_async_copy(h_ref, _tokens(xs_ref, 0, tm), sem).wait()


def _dispatch(off, zstart, zpad, rect, h2, xs, n_rows, zero_rows):
    n_tiles, _, tm = rect.shape
    kern = functools.partial(_dispatch_kernel, tm=tm, zero_rows=zero_rows)
    in_specs = [
        pl.BlockSpec((1, REC_ROWS, tm), lambda i, o, z, p: (i, 0, 0), memory_space=pltpu.SMEM),
        pl.BlockSpec((tm * ROW_CHUNKS, LANES), lambda i, o, z, p: (i, 0)),
    ]
    scratch = [pltpu.SemaphoreType.DMA(())]
    args = [off, zstart, zpad, rect, h2]
    aliases = {}
    if zero_rows:
        scratch += [pltpu.SemaphoreType.DMA(()), pltpu.VMEM((zero_rows // 2 * ROW_CHUNKS, LANES), _F32)]
    else:
        in_specs.append(pl.BlockSpec(memory_space=pl.ANY))
        args.append(xs)
        aliases = {5: 0}
    grid_spec = pltpu.PrefetchScalarGridSpec(
        num_scalar_prefetch=3, grid=(n_tiles,), in_specs=in_specs,
        out_specs=pl.BlockSpec(memory_space=pl.ANY), scratch_shapes=scratch)
    return pl.pallas_call(
        kern,
        grid_spec=grid_spec,
        out_shape=jax.ShapeDtypeStruct((n_rows * ROW_CHUNKS, LANES), _F32),
        input_output_aliases=aliases,
        compiler_params=_params(("arbitrary",)),
        name="dispatch",
    )(*args)


def _experts_kernel(texp_ref, nused_ref, x_ref, wg_ref, wu_ref, wd_ref, y_ref, wgb, wub, wdb, *, tm):
    i = pl.program_id(0)
    new_expert = (i == 0) | (texp_ref[i] != texp_ref[jnp.maximum(i - 1, 0)])

    @pl.when(new_expert)
    def _():
        wgb[...] = wg_ref[...].astype(_BF16)
        wub[...] = wu_ref[...].astype(_BF16)
        wdb[...] = wd_ref[...].astype(_BF16)

    @pl.when(i < nused_ref[0])
    def _():
        x = _load_token_major(x_ref, tm).astype(_BF16)
        hg = _dot(x, wgb[...])
        hu = _dot(x, wub[...])
        act = (hg * jax.nn.sigmoid(hg) * hu).astype(_BF16)
        y_ref[...] = _dot(act, wdb[...])

    @pl.when(i >= nused_ref[0])
    def _():
        y_ref[...] = jnp.zeros_like(y_ref)


def _experts(tile_expert, n_used, xs, w_gate, w_up, w_down, tm):
    n_tiles = tile_expert.shape[0]
    w_in_spec = pl.BlockSpec((None, D_MODEL, D_EXPERT), lambda i, te, nu: (te[i], 0, 0))
    grid_spec = pltpu.PrefetchScalarGridSpec(
        num_scalar_prefetch=2,
        grid=(n_tiles,),
        in_specs=[
            pl.BlockSpec((tm * ROW_CHUNKS, LANES), lambda i, te, nu: (jnp.minimum(i, nu[0] - 1), 0)),
            w_in_spec, w_in_spec,
            pl.BlockSpec((None, D_EXPERT, D_MODEL), lambda i, te, nu: (te[i], 0, 0)),
        ],
        out_specs=pl.BlockSpec((tm, D_MODEL), lambda i, te, nu: (i, 0)),
        scratch_shapes=[pltpu.VMEM((D_MODEL, D_EXPERT), _BF16), pltpu.VMEM((D_MODEL, D_EXPERT), _BF16),
                        pltpu.VMEM((D_EXPERT, D_MODEL), _BF16)],
    )
    return pl.pallas_call(
        functools.partial(_experts_kernel, tm=tm),
        grid_spec=grid_spec,
        out_shape=jax.ShapeDtypeStruct((n_tiles * tm, D_MODEL), _F32),
        compiler_params=_params(("arbitrary",)),
        name="experts",
    )(tile_expert, n_used, xs, w_gate, w_up, w_down)


def _combine_kernel(off_ref, rec_ref, recn_ref, x1_ref, route_ref, y_hbm, o_ref, r0, r1, sem, *, tm):
    i = pl.program_id(0)
    n = pl.num_programs(0)
    bufs = (r0, r1)

    def issue(rec, slot):
        def body(r, c):
            for k in range(2):
                src = off_ref[rec[0, k, r]] + rec[0, 4 + k, r]
                pltpu.make_async_copy(y_hbm.at[pl.ds(src, 1), :], bufs[k].at[slot, pl.ds(r, 1), :],
                                      sem.at[slot, k]).start()
            return c

        lax.fori_loop(0, tm, body, 0, unroll=DMA_UNROLL)

    @pl.when(i == 0)
    def _():
        issue(rec_ref, 0)

    @pl.when(i + 1 < n)
    def _():
        issue(recn_ref, (i + 1) % 2)

    slot = i % 2
    for k in range(2):
        pltpu.make_async_copy(y_hbm.at[pl.ds(0, tm), :], bufs[k].at[slot], sem.at[slot, k]).wait()
    rec = route_ref[...]
    o_ref[...] = x1_ref[...] + rec[:, 2:3] * r0[slot] + rec[:, 3:4] * r1[slot]


def _combine(off, rect, x1, route, y_sorted):
    n_tiles, _, tm = rect.shape
    t = x1.shape[0]
    kern = functools.partial(_combine_kernel, tm=tm)
    rec_spec = lambda nxt: pl.BlockSpec(
        (1, REC_ROWS, tm), lambda i, o: (jnp.minimum(i + nxt, n_tiles - 1), 0, 0), memory_space=pltpu.SMEM)
    grid_spec = pltpu.PrefetchScalarGridSpec(
        num_scalar_prefetch=1,
        grid=(n_tiles,),
        in_specs=[
            rec_spec(0), rec_spec(1),
            pl.BlockSpec((tm, D_MODEL), lambda i, o: (i, 0)),
            pl.BlockSpec((tm, ROUTE_LANES), lambda i, o: (i, 0)),
            pl.BlockSpec(memory_space=pl.ANY),
        ],
        out_specs=pl.BlockSpec((tm, D_MODEL), lambda i, o: (i, 0)),
        scratch_shapes=[pltpu.VMEM((2, tm, D_MODEL), _F32), pltpu.VMEM((2, tm, D_MODEL), _F32),
                        pltpu.SemaphoreType.DMA((2, 2))],
    )
    return pl.pallas_call(
        kern,
        grid_spec=grid_spec,
        out_shape=jax.ShapeDtypeStruct((t, D_MODEL), _F32),
        compiler_params=_params(("arbitrary",)),
        name="combine",
    )(off, rect, rect, x1, route, y_sorted)


def _rope_tables(pos):
    inv = ROPE_THETA ** (-jnp.arange(0, HEAD_DIM, 2, dtype=_F32) / HEAD_DIM)
    ang = pos.astype(_F32)[:, None] * inv[None, :]
    cos = jnp.cos(ang)
    sin = jnp.sin(ang)
    return jnp.tile(cos, (1, 4)), jnp.tile(jnp.concatenate([-sin, sin], axis=1), (1, 2))


def _head_sum_matrices():
    lane_head = jnp.arange(ATTN_WIDTH) // HEAD_DIM
    cols = jnp.arange(LANES)
    esum = (lane_head[:, None] == cols[None, :]).astype(_F32) / HEAD_DIM
    eexp = (cols[:, None] == lane_head[None, :]).astype(_F32)
    return esum.astype(_BF16), jnp.concatenate([eexp, eexp], axis=0).astype(_BF16)


def kernel(x_prompt, x_sample, cache_k, cache_v, state_pool, page_table, norm1_g, w_in, q_norm_g, k_norm_g,
           lambda_q1, lambda_k1, lambda_q2, lambda_k2, subln_g, w_pool, pool_scale, w_out, norm2_g, w_coarse,
           b_coarse, w_fine, b_fine, w_gate, w_up, w_down):
    depth = w_in.shape[0]
    assert depth == 1
    l = 0
    lam_init = 0.8 - 0.6 * math.exp(-0.3 * l)
    bp, sp, _ = x_prompt.shape
    bs, ts, _ = x_sample.shape
    tp, tsamp = bp * sp, bs * ts
    n_pages = page_table.shape[1]
    past = n_pages * PAGE_SIZE
    n_hc = 2 * N_HEADS

    w_in_bf = w_in[l].astype(_BF16)
    w_out_bf = w_out[l].astype(_BF16)
    w_pool_bf = w_pool[l].astype(_BF16)
    g1 = norm1_g[l][None, :]
    g2 = norm2_g[l][None, :]
    gq = jnp.tile(q_norm_g[l], n_hc)[None, :]
    gk = jnp.tile(k_norm_g[l], n_hc)[None, :]
    sg = subln_g[l][None, :]
    ps = pool_scale[l][None, :]
    lam_p = jnp.stack([lambda_q1[l], lambda_k1[l], lambda_q2[l], lambda_k2[l]])
    esum, eexp = _head_sum_matrices()
    pad = ROUTE_LANES - N_EXPERTS - N_EXPERT_GROUPS
    w_fine_flat = jnp.transpose(w_fine[l], (1, 0, 2)).reshape(D_MODEL, N_EXPERTS)
    w_route = jnp.concatenate([w_fine_flat, w_coarse[l], jnp.zeros((D_MODEL, pad), _F32)], axis=1)
    wrh, wrl = _split_bf16(w_route)
    b_route = jnp.concatenate([b_fine[l].reshape(-1), b_coarse[l], jnp.zeros((pad,), _F32)])[None, :]

    cos_p, sin_p = _rope_tables(jnp.arange(sp, dtype=jnp.int32))
    cos_s, sin_s = _rope_tables(past + jnp.arange(ts, dtype=jnp.int32))
    cos_s, sin_s = jnp.tile(cos_s, (bs, 1)), jnp.tile(sin_s, (bs, 1))
    xp2 = x_prompt.reshape(tp, D_MODEL)
    xs2 = x_sample.reshape(tsamp, D_MODEL)
    qp, kpt, kpt_bf, vp, vp_bf, up = _proj(xp2, g1, w_in_bf, gq, gk, cos_p, sin_p, esum, eexp, PROJ_TILE, sp)
    qs, ks, _, vs, _, us = _proj(xs2, g1, w_in_bf, gq, gk, cos_s, sin_s, esum, eexp, tsamp)

    shp = (bp, sp, ATTN_WIDTH)
    a_p = _attn_prompt(lam_p, sg, qp.reshape(shp), kpt_bf, vp_bf.reshape(shp), lam_init)
    a_p = a_p.reshape(tp, ATTN_WIDTH)

    q4 = qs.astype(_F32).reshape(bs, 1, 1, ts, ATTN_WIDTH)
    lane_hc = jnp.arange(ATTN_WIDTH) // HEAD_DIM
    head = jnp.arange(N_HEADS)[:, None, None, None]
    comp = jnp.arange(2)[None, :, None, None]
    keep = lane_hc[None, None, None, :] == head * 2 + comp
    qrep = jnp.where(keep[None], q4, 0.0).reshape(bs, QROWS, ATTN_WIDTH)
    new_rows = 8
    kn = jnp.pad(ks.reshape(bs, ts, ATTN_WIDTH), ((0, 0), (0, new_rows - ts), (0, 0)))
    vn = jnp.pad(vs.reshape(bs, ts, ATTN_WIDTH), ((0, 0), (0, new_rows - ts), (0, 0)))
    n_phys = cache_k.shape[1]
    ck = jnp.transpose(cache_k[l], (0, 2, 3, 1)).reshape(n_phys, ATTN_WIDTH, PAGE_SIZE)
    cv = cache_v[l].reshape(n_phys, PAGE_SIZE * N_HEADS, V_HEAD_DIM)
    a_s = _attn_sample(page_table, lam_p, sg, qrep, kn, vn, ck, cv, ts, lam_init)
    a_s = a_s.reshape(bs, N_HEADS, 2, ts, V_HEAD_DIM)[:, :, 0]
    a_s = a_s.transpose(0, 2, 1, 3).reshape(tsamp, ATTN_WIDTH).astype(_BF16)

    hist_p = jnp.zeros((bp, HALO_ROWS, POOL_WIDTH), _F32)
    d_p = _poolwin(up, hist_p, MIX_TILE, sp // MIX_TILE, 0)
    seq_rows = 8
    us_pad = jnp.pad(us.reshape(bs, ts, POOL_WIDTH), ((0, 0), (0, seq_rows - ts), (0, 0)))
    hist_s = jnp.pad(state_pool[l], ((0, 0), (HALO_ROWS - POOL_HIST, 0), (0, 0)))
    d_s = _poolwin(us_pad.reshape(bs * seq_rows, POOL_WIDTH), hist_s, seq_rows, 1, past)
    d_s = d_s.reshape(bs, seq_rows, POOL_WIDTH)[:, :ts].reshape(tsamp, POOL_WIDTH)

    counts0 = jnp.zeros((1, ROUTE_LANES), _F32)
    x1_p, h2_p, route_p, rect_p, counts_p = _mix(xp2, a_p, d_p, w_pool_bf, ps, w_out_bf, g2, wrh, wrl, b_route,
                                                 counts0, MIX_TILE)
    x1_s, h2_s, route_s, rect_s, counts = _mix(xs2, a_s, d_s, w_pool_bf, ps, w_out_bf, g2, wrh, wrl, b_route,
                                               counts_p, tsamp)

    tm = EXPERT_TILE
    t_all = tp + tsamp
    n_tiles = (2 * t_all + N_EXPERTS * (tm - 1) + tm - 1) // tm
    cnt = counts[0, :N_EXPERTS].astype(jnp.int32)
    padded = (cnt + tm - 1) // tm * tm
    ends = jnp.cumsum(padded)
    off = ends - padded
    n_used = (ends[-1] // tm).astype(jnp.int32)
    tile_ids = jnp.arange(n_tiles, dtype=jnp.int32)
    te = jnp.sum((ends[None, :] <= (tile_ids * tm)[:, None]).astype(jnp.int32), axis=1)
    te = jnp.minimum(te, N_EXPERTS - 1)
    last = jnp.sum(jnp.where(tile_ids == n_used - 1, te, 0))
    tile_expert = jnp.where(tile_ids < n_used, te, last)

    n_rows = n_tiles * tm
    zstart = off + counts_p[0, :N_EXPERTS].astype(jnp.int32)
    zinfo = (jnp.concatenate([zstart, ends[-1:]]), jnp.concatenate([ends - zstart, (n_tiles - n_used)[None]]))
    assert tsamp <= tm
    xs = _dispatch(off, *zinfo, rect_p, h2_p, None, n_rows, 2 * tm)
    xs = _dispatch(off, *zinfo, rect_s, h2_s, xs, n_rows, 0)
    y_sorted = _experts(tile_expert, n_used.reshape(1), xs, w_gate[l], w_up[l], w_down[l], tm)
    y_p = _combine(off, rect_p, x1_p, route_p, y_sorted)
    y_s = _combine(off, rect_s, x1_s, route_s, y_sorted)

    new_k_p = jnp.transpose(kpt.reshape(bp, n_hc, HEAD_DIM, sp), (0, 3, 1, 2))
    new_pool_p = up.reshape(bp, sp, POOL_WIDTH)[:, sp - POOL_HIST:]
    new_pool_s = jnp.concatenate([state_pool[l], us.reshape(bs, ts, POOL_WIDTH)], axis=1)[:, -POOL_HIST:]
    return (y_p.reshape(bp, sp, D_MODEL),
            y_s.reshape(bs, ts, D_MODEL),
            new_k_p[None],
            vp.reshape(1, bp, sp, N_HEADS, V_HEAD_DIM),
            new_pool_p[None],
            ks.reshape(1, bs, ts, n_hc, HEAD_DIM),
            vs.reshape(1, bs, ts, N_HEADS, V_HEAD_DIM),
            new_pool_s[None])
```

```python
import functools
import math

import jax
import jax.numpy as jnp
from jax import lax
from jax.experimental import pallas as pl
from jax.experimental.pallas import tpu as pltpu

_F32 = jnp.float32
_BF16 = jnp.bfloat16
_I32 = jnp.int32

D_MODEL = 2048
ATTN_WIDTH = 1024
POOL_WIDTH = 1024
HEAD_DIM = 64
N_HEADS = 8
V_HEAD_DIM = 128
POOL_WINDOWS = (2, 4, 8, 16)
POOL_GROUP_WIDTH = 256
POOL_HIST = 15
HALO_ROWS = 16
N_EXPERTS = 32
EXPERTS_PER_GROUP = 8
N_EXPERT_GROUPS = 4
D_EXPERT = 256
ROPE_THETA = 10000.0
RMS_EPS = 1e-6
NEG_INF = -1e30
PAGE_SIZE = 128
LANES = 128
ROUTE_LANES = 128
REC_ROWS = 8
VMEM_LIMIT = 56 * 1024 * 1024

PROJ_TILE = 512
ATTN_TILE = 512
ATTN_WIDE = 4
MIX_TILE = 512
EXPERT_TILE = 256
QROWS = 64
DMA_UNROLL = 8


def _dot(a, b):
    return jnp.dot(a, b, preferred_element_type=_F32)


def _split_bf16(x):
    hi = x.astype(_BF16)
    lo = (x - hi.astype(_F32)).astype(_BF16)
    return hi, lo


def _params(sem):
    return pltpu.CompilerParams(dimension_semantics=sem, vmem_limit_bytes=VMEM_LIMIT)


ROW_CHUNKS = D_MODEL // LANES


def _load_token_major(ref, n):
    return jnp.concatenate([ref[pl.ds(c, n, stride=ROW_CHUNKS), :] for c in range(ROW_CHUNKS)], axis=1)


def _store_token_major(ref, x):
    n = x.shape[0]
    for c in range(ROW_CHUNKS):
        ref[pl.ds(c, n, stride=ROW_CHUNKS), :] = x[:, c * LANES:(c + 1) * LANES]


def _whole_vmem():
    return pl.BlockSpec(memory_space=pltpu.VMEM)


def _proj_kernel(x_ref, g1_ref, w_ref, gq_ref, gk_ref, cos_ref, sin_ref, esum_ref, eexp_ref,
                 q_ref, kf_ref, kb_ref, vf_ref, vb_ref, u_ref, *, k_transposed):
    x = x_ref[...]
    ms = jnp.mean(x * x, axis=-1, keepdims=True)
    h = (x * lax.rsqrt(ms + RMS_EPS) * g1_ref[...]).astype(_BF16)
    cos = cos_ref[...]
    sin = sin_ref[...]
    lane = lax.broadcasted_iota(_I32, (1, LANES), 1)
    upper = (lane & (HEAD_DIM - 1)) >= HEAD_DIM // 2

    def normed_rope(z, g_ref, outs, transposed):
        msq = _dot((z * z).astype(_BF16), esum_ref[...])
        rh, rl = _split_bf16(lax.rsqrt(msq + RMS_EPS))
        rb = _dot(jnp.concatenate([rh, rl], axis=1), eexp_ref[...])
        n = z * rb * g_ref[...]
        for j in range(ATTN_WIDTH // LANES):
            sl = slice(j * LANES, (j + 1) * LANES)
            nj = n[:, sl]
            swapped = jnp.where(upper, pltpu.roll(nj, HEAD_DIM // 2, 1),
                                pltpu.roll(nj, LANES - HEAD_DIM // 2, 1))
            o = nj * cos + swapped * sin
            if transposed:
                ot = o.T
                for ref, _ in outs:
                    ref[sl, :] = ot.astype(ref.dtype)
            else:
                for ref, scale in outs:
                    ref[:, sl] = (o * scale).astype(ref.dtype)

    a = ATTN_WIDTH
    normed_rope(_dot(h, w_ref[:, 0:a]), gq_ref, ((q_ref, HEAD_DIM ** -0.5),), False)
    normed_rope(_dot(h, w_ref[:, a:2 * a]), gk_ref, ((kf_ref, 1.0), (kb_ref, 1.0)), k_transposed)
    zv = _dot(h, w_ref[:, 2 * a:3 * a])
    vf_ref[...] = zv
    vb_ref[...] = zv.astype(_BF16)
    u_ref[...] = _dot(h, w_ref[:, 3 * a:])


def _proj(x, g1, w_bf, gq, gk, cos, sin, esum, eexp, tm, seq_len=None):
    t = x.shape[0]
    n_pos_tiles = cos.shape[0] // tm
    row = lambda i: (i, 0)
    const = lambda i: (0, 0)
    tok_spec = lambda w: pl.BlockSpec((tm, w), row)
    a = ATTN_WIDTH
    if seq_len is None:
        k_spec, k_shape = tok_spec(a), (t, a)
    else:
        tiles_per_seq = seq_len // tm
        k_spec = pl.BlockSpec((None, a, tm), lambda i: (i // tiles_per_seq, 0, i % tiles_per_seq))
        k_shape = (t // seq_len, a, seq_len)
    return pl.pallas_call(
        functools.partial(_proj_kernel, k_transposed=seq_len is not None),
        grid=(t // tm,),
        in_specs=[
            tok_spec(D_MODEL),
            pl.BlockSpec((1, D_MODEL), const),
            _whole_vmem(),
            pl.BlockSpec((1, a), const),
            pl.BlockSpec((1, a), const),
            pl.BlockSpec((tm, LANES), lambda i: (i % n_pos_tiles, 0)),
            pl.BlockSpec((tm, LANES), lambda i: (i % n_pos_tiles, 0)),
            _whole_vmem(),
            _whole_vmem(),
        ],
        out_specs=[tok_spec(a), k_spec, k_spec, tok_spec(a), tok_spec(a), tok_spec(POOL_WIDTH)],
        out_shape=[
            jax.ShapeDtypeStruct((t, a), _BF16),
            jax.ShapeDtypeStruct(k_shape, _F32),
            jax.ShapeDtypeStruct(k_shape, _BF16),
            jax.ShapeDtypeStruct((t, a), _F32),
            jax.ShapeDtypeStruct((t, a), _BF16),
            jax.ShapeDtypeStruct((t, POOL_WIDTH), _F32),
        ],
        compiler_params=_params(("arbitrary",)),
        name="proj",
    )(x, g1, w_bf, gq, gk, cos, sin, esum, eexp)


def _lambda(lam_ref, lam_init):
    lp = lam_ref[...]
    s1 = jnp.sum(lp[0:1] * lp[1:2], axis=-1, keepdims=True)
    s2 = jnp.sum(lp[2:3] * lp[3:4], axis=-1, keepdims=True)
    return jnp.exp(s1) - jnp.exp(s2) + lam_init


def _sub_norm(o, sg, lam_init):
    ms = jnp.mean(o * o, axis=-1, keepdims=True)
    return o * lax.rsqrt(ms + RMS_EPS) * sg * (1.0 - lam_init)


def _prompt_attention(lam_ref, sg_ref, q_ref, kt_ref, v_ref, o_ref, *, tile, wide, lam_init, side):
    qi = pl.program_id(2)
    q = q_ref[...]
    lane = lax.broadcasted_iota(_I32, (1, LANES), 1)
    zero = jnp.zeros_like(q)
    qc = (jnp.where(lane < HEAD_DIM, q, zero), jnp.where(lane >= HEAD_DIM, q, zero))

    def update(s, state, vblk):
        m, l, acc = state
        m_new = jnp.maximum(m, jnp.max(s, axis=-1, keepdims=True))
        alpha = jnp.exp(m - m_new)
        p = jnp.exp(s - m_new)
        l = alpha * l + jnp.sum(p, axis=-1, keepdims=True)
        acc = alpha * acc + _dot(p.astype(_BF16), vblk)
        return m_new, l, acc

    def block(first, carry, masked, width):
        start = pl.multiple_of(first * tile, tile)
        ktblk = kt_ref[:, pl.ds(start, width)]
        vblk = v_ref[pl.ds(start, width), :]
        out = []
        for c in range(2):
            s = _dot(qc[c], ktblk)
            if masked:
                r = lax.broadcasted_iota(_I32, (tile, width), 0)
                col = lax.broadcasted_iota(_I32, (tile, width), 1)
                s = jnp.where(col <= r + (width - tile), s, NEG_INF)
            out.append(update(s, carry[c], vblk))
        return tuple(out)

    init_one = (jnp.full((tile, 1), NEG_INF, _F32), jnp.zeros((tile, 1), _F32),
                jnp.zeros((tile, V_HEAD_DIM), _F32))
    n_full = qi // wide
    carry = lax.fori_loop(0, n_full, lambda j, c: block(j * wide, c, False, wide * tile), (init_one, init_one))
    lam = _lambda(lam_ref, lam_init)
    for last in range(1, wide + 1):
        @pl.when(qi + 1 - n_full * wide == last)
        def _(last=last):
            side_finish = side()
            (_, l0, a0), (_, l1, a1) = block(n_full * wide, carry, True, last * tile)
            o = a0 / l0 - lam * (a1 / l1)
            o_ref[...] = _sub_norm(o, sg_ref[...], lam_init).astype(o_ref.dtype)
            side_finish()


def _attn_kernel(pt_ref, lam_ref, sg_ref, q_ref, kt_ref, v_ref, qs_ref, kn_ref, vn_ref, ck_hbm, cv_hbm,
                 o_ref, os_ref, kbuf, vbuf, sem, m_ref, l_ref, acc_ref, *, tile, wide, lam_init, n_tok,
                 n_steps, pages_per_step, steps_per_elem):
    npg = pages_per_step
    step = (pl.program_id(0) * pl.num_programs(1) + pl.program_id(1)) * pl.num_programs(2) + pl.program_id(2)
    assert steps_per_elem & (steps_per_elem - 1) == 0
    part = step & (steps_per_elem - 1)

    def page_copies(s):
        elem = lax.shift_right_logical(s, steps_per_elem.bit_length() - 1)
        first_page = (s & (steps_per_elem - 1)) * npg
        base = (s & 1) * npg
        out = []
        for p in range(npg):
            page = pt_ref[elem, first_page + p]
            out.append(pltpu.make_async_copy(ck_hbm.at[page], kbuf.at[base + p], sem.at[0, base + p]))
            out.append(pltpu.make_async_copy(cv_hbm.at[page], vbuf.at[base + p], sem.at[1, base + p]))
        return out

    @pl.when(step == 0)
    def _():
        for cp in page_copies(step):
            cp.start()

    @pl.when(step + 1 < n_steps)
    def _():
        for cp in page_copies(step + 1):
            cp.start()

    def sample_pages():
        return _sample_pages_step(page_copies(step), (step & 1) * npg, part, lam_ref, sg_ref, qs_ref, kn_ref,
                                  vn_ref, os_ref, kbuf, vbuf, m_ref, l_ref, acc_ref, npg=npg, n_tok=n_tok,
                                  lam_init=lam_init, last_part=steps_per_elem - 1)

    _prompt_attention(lam_ref, sg_ref, q_ref, kt_ref, v_ref, o_ref, tile=tile, wide=wide, lam_init=lam_init,
                      side=sample_pages)


def _sample_pages_step(copies, base, part, lam_ref, sg_ref, q_ref, kn_ref, vn_ref, o_ref, kbuf, vbuf,
                       m_ref, l_ref, acc_ref, *, npg, n_tok, lam_init, last_part):
    for cp in copies:
        cp.wait()
    k_refs = [kbuf.at[base + p] for p in range(npg)]
    v_refs = [vbuf.at[base + p] for p in range(npg)]
    q = q_ref[...]
    rows_per_head = 2 * n_tok
    head_rows = lambda h: slice(h * rows_per_head, (h + 1) * rows_per_head)

    @pl.when(part == 0)
    def _():
        kn = kn_ref[...]
        vn = vn_ref[...]
        tok = lax.broadcasted_iota(_I32, (QROWS, 1), 0) & (n_tok - 1)
        ss = [jnp.where(tok >= jn, jnp.sum(q * kn[jn:jn + 1, :], axis=-1, keepdims=True), NEG_INF)
              for jn in range(n_tok)]
        m = functools.reduce(jnp.maximum, ss)
        ps = [jnp.exp(s - m) for s in ss]
        m_ref[...] = m
        l_ref[...] = functools.reduce(jnp.add, ps)
        for h in range(N_HEADS):
            lanes = slice(h * V_HEAD_DIM, (h + 1) * V_HEAD_DIM)
            acc_ref[head_rows(h), :] = functools.reduce(
                jnp.add, [ps[jn][head_rows(h)] * vn[jn:jn + 1, lanes] for jn in range(n_tok)])

    ss = [_dot(q, kr[...]) for kr in k_refs]
    m_old = m_ref[...]
    m_new = m_old
    for s in ss:
        m_new = jnp.maximum(m_new, jnp.max(s, axis=-1, keepdims=True))
    alpha = jnp.exp(m_old - m_new)
    ps = [jnp.exp(s - m_new) for s in ss]
    l = alpha * l_ref[...] + functools.reduce(jnp.add, [jnp.sum(p, axis=-1, keepdims=True) for p in ps])
    m_ref[...] = m_new
    l_ref[...] = l
    for h in range(N_HEADS):
        pv = functools.reduce(jnp.add, [
            _dot(p[head_rows(h)], vr[pl.ds(h, PAGE_SIZE, stride=N_HEADS), :]) for p, vr in zip(ps, v_refs)])
        acc_ref[head_rows(h), :] = alpha[head_rows(h)] * acc_ref[head_rows(h), :] + pv

    def finish():
        @pl.when(part == last_part)
        def _():
            o = acc_ref[...] / l_ref[...]
            lam = _lambda(lam_ref, lam_init)
            d = o - lam * pltpu.roll(o, QROWS - n_tok, 0)
            o_ref[...] = _sub_norm(d, sg_ref[...], lam_init)

    return finish


def _attn(page_table, lam_p, sg, q, kt, v, qrep, kn, vn, cache_k, cache_v, n_tok, lam_init):
    b, s, _ = q.shape
    tile = ATTN_TILE
    nq = s // tile
    n_steps = b * N_HEADS * nq
    nb, n_pages = page_table.shape
    assert (nb * n_pages) % n_steps == 0 and n_steps % nb == 0
    npg = nb * n_pages // n_steps
    steps_per_elem = n_steps // nb
    width = ATTN_WIDTH
    assert QROWS == 2 * N_HEADS * n_tok and cache_k.shape[1:] == (width, PAGE_SIZE)
    assert cache_v.shape[1:] == (PAGE_SIZE * N_HEADS, V_HEAD_DIM)

    elem = lambda bi, h, i: ((bi * N_HEADS + h) * nq + i) // steps_per_elem
    per_elem = lambda rows, cols: pl.BlockSpec((None, rows, cols), lambda bi, h, i, pt: (elem(bi, h, i), 0, 0))
    kern = functools.partial(_attn_kernel, tile=tile, wide=ATTN_WIDE, lam_init=lam_init, n_tok=n_tok,
                             n_steps=n_steps, pages_per_step=npg, steps_per_elem=steps_per_elem)
    grid_spec = pltpu.PrefetchScalarGridSpec(
        num_scalar_prefetch=1,
        grid=(b, N_HEADS, nq),
        in_specs=[
            pl.BlockSpec((4, HEAD_DIM), lambda bi, h, i, pt: (0, 0)),
            pl.BlockSpec((1, V_HEAD_DIM), lambda bi, h, i, pt: (0, 0)),
            pl.BlockSpec((None, tile, LANES), lambda bi, h, i, pt: (bi, i, h)),
            pl.BlockSpec((None, LANES, s), lambda bi, h, i, pt: (bi, h, 0)),
            pl.BlockSpec((None, s, LANES), lambda bi, h, i, pt: (bi, 0, h)),
            per_elem(QROWS, width), per_elem(kn.shape[1], width), per_elem(vn.shape[1], width),
            pl.BlockSpec(memory_space=pl.ANY), pl.BlockSpec(memory_space=pl.ANY),
        ],
        out_specs=[pl.BlockSpec((None, tile, LANES), lambda bi, h, i, pt: (bi, i, h)),
                   per_elem(QROWS, V_HEAD_DIM)],
        scratch_shapes=[pltpu.VMEM((2 * npg, width, PAGE_SIZE), _F32),
                        pltpu.VMEM((2 * npg, PAGE_SIZE * N_HEADS, V_HEAD_DIM), _F32),
                        pltpu.SemaphoreType.DMA((2, 2 * npg)),
                        pltpu.VMEM((QROWS, 1), _F32), pltpu.VMEM((QROWS, 1), _F32),
                        pltpu.VMEM((QROWS, V_HEAD_DIM), _F32)],
    )
    return pl.pallas_call(
        kern,
        grid_spec=grid_spec,
        out_shape=[jax.ShapeDtypeStruct((b, s, ATTN_WIDTH), _BF16),
                   jax.ShapeDtypeStruct((nb, QROWS, V_HEAD_DIM), _F32)],
        compiler_params=_params(("arbitrary", "arbitrary", "arbitrary")),
        name="attn",
    )(page_table, lam_p, sg, q, kt, v, qrep, kn, vn, cache_k, cache_v)


def _poolwin_kernel(u_ref, halo_ref, hist_ref, d_ref, *, tm, tiles_per_seq, start_pos):
    i = pl.program_id(0)
    t_in_seq = i % tiles_per_seq
    u = u_ref[...]
    halo = jnp.where(t_in_seq == 0, hist_ref[...], halo_ref[...])
    ext = jnp.concatenate([halo, u], axis=0)
    pos = start_pos + t_in_seq * tm + lax.broadcasted_iota(_I32, (tm, 1), 0)
    for g, w in enumerate(POOL_WINDOWS):
        sl = slice(g * POOL_GROUP_WIDTH, (g + 1) * POOL_GROUP_WIDTH)
        acc = ext[:, sl]
        span = 1
        while span < w:
            acc = acc + pltpu.roll(acc, span, 0)
            span *= 2
        cnt = jnp.minimum(pos + 1, w).astype(_F32)
        d_ref[:, sl] = (acc[HALO_ROWS:] / cnt - u[:, sl]).astype(d_ref.dtype)


def _poolwin(u, hist, tm, tiles_per_seq, start_pos):
    t = u.shape[0]
    halo_blocks = max(tm // HALO_ROWS, 1)
    kern = functools.partial(_poolwin_kernel, tm=tm, tiles_per_seq=tiles_per_seq, start_pos=start_pos)
    return pl.pallas_call(
        kern,
        grid=(t // tm,),
        in_specs=[
            pl.BlockSpec((tm, POOL_WIDTH), lambda i: (i, 0)),
            pl.BlockSpec((HALO_ROWS, POOL_WIDTH),
                         lambda i: (jnp.where(i % tiles_per_seq == 0, 0, i * halo_blocks - 1), 0)),
            pl.BlockSpec((None, HALO_ROWS, POOL_WIDTH), lambda i: (i // tiles_per_seq, 0, 0)),
        ],
        out_specs=pl.BlockSpec((tm, POOL_WIDTH), lambda i: (i, 0)),
        out_shape=jax.ShapeDtypeStruct((t, POOL_WIDTH), _BF16),
        compiler_params=_params(("arbitrary",)),
        name="poolwin",
    )(u, u, hist)


def _mix_kernel(x_ref, a_ref, d_ref, wp_ref, ps_ref, wo_ref, g2_ref, wrh_ref, wrl_ref, br_ref,
                cin_ref, x1_ref, h2_ref, route_ref, rect_ref, cout_ref, carry_ref, *, tm):
    i = pl.program_id(0)

    @pl.when(i == 0)
    def _():
        carry_ref[...] = cin_ref[...]

    d = d_ref[...]
    gw = POOL_GROUP_WIDTH
    py = jnp.concatenate([_dot(d[:, g * gw:(g + 1) * gw], wp_ref[g]) for g in range(len(POOL_WINDOWS))],
                         axis=1) * ps_ref[...]
    x1 = (x_ref[...] + _dot(a_ref[...], wo_ref[0:ATTN_WIDTH, :])
          + _dot(py.astype(_BF16), wo_ref[ATTN_WIDTH:, :]))
    x1_ref[...] = x1
    ms = jnp.mean(x1 * x1, axis=-1, keepdims=True)
    h2 = x1 * lax.rsqrt(ms + RMS_EPS) * g2_ref[...]
    _store_token_major(h2_ref, h2)

    hh, hl = _split_bf16(h2)
    logits = _dot(hh, wrh_ref[...]) + _dot(hl, wrh_ref[...]) + _dot(hh, wrl_ref[...]) + br_ref[...]
    lane = lax.broadcasted_iota(_I32, (1, ROUTE_LANES), 1)
    big = jnp.int32(ROUTE_LANES)
    is_coarse = (lane >= N_EXPERTS) & (lane < N_EXPERTS + N_EXPERT_GROUPS)
    lc = jnp.where(is_coarse, logits, NEG_INF)
    mc = jnp.max(lc, axis=-1, keepdims=True)
    g_idx = jnp.min(jnp.where(lc == mc, lane, big), axis=-1, keepdims=True) - N_EXPERTS
    p_g = 1.0 / jnp.sum(jnp.exp(lc - mc), axis=-1, keepdims=True)
    in_group = (lane < N_EXPERTS) & ((lane >> (EXPERTS_PER_GROUP.bit_length() - 1)) == g_idx)
    lf = jnp.where(in_group, logits, NEG_INF)
    v1 = jnp.max(lf, axis=-1, keepdims=True)
    i1 = jnp.min(jnp.where(lf == v1, lane, big), axis=-1, keepdims=True)
    lf2 = jnp.where(lane == i1, NEG_INF, lf)
    v2 = jnp.max(lf2, axis=-1, keepdims=True)
    i2 = jnp.min(jnp.where(lf2 == v2, lane, big), axis=-1, keepdims=True)
    e21 = jnp.exp(v2 - v1)
    w1 = p_g / (1.0 + e21)
    w2 = p_g * e21 / (1.0 + e21)

    oh1 = lane == i1
    oh2 = lane == i2
    onehot = (oh1 | oh2).astype(_F32)
    r = lax.broadcasted_iota(_I32, (tm, tm), 0)
    col = lax.broadcasted_iota(_I32, (tm, tm), 1)
    lower = (col < r).astype(_BF16)
    before = _dot(lower, onehot.astype(_BF16)) + carry_ref[...]
    rank1 = jnp.sum(jnp.where(oh1, before, 0.0), axis=-1, keepdims=True)
    rank2 = jnp.sum(jnp.where(oh2, before, 0.0), axis=-1, keepdims=True)
    carry = carry_ref[...] + jnp.sum(onehot, axis=0, keepdims=True)
    carry_ref[...] = carry
    cout_ref[...] = carry

    rec = jnp.zeros((tm, ROUTE_LANES), _F32)
    for k, val in enumerate((i1.astype(_F32), i2.astype(_F32), w1, w2, rank1, rank2)):
        rec = jnp.where(lane == k, val, rec)
    route_ref[...] = rec
    rect_ref[...] = rec.T[0:REC_ROWS, :].astype(_I32)


def _mix(x, a, d, wp_bf, ps, wo_bf, g2, wrh, wrl, br, counts_in, tm):
    t = x.shape[0]
    row = lambda i: (i, 0)
    const = lambda i: (0, 0)
    kern = functools.partial(_mix_kernel, tm=tm)
    return pl.pallas_call(
        kern,
        grid=(t // tm,),
        in_specs=[
            pl.BlockSpec((tm, D_MODEL), row),
            pl.BlockSpec((tm, ATTN_WIDTH), row),
            pl.BlockSpec((tm, POOL_WIDTH), row),
            _whole_vmem(),
            pl.BlockSpec((1, POOL_WIDTH), const),
            _whole_vmem(),
            pl.BlockSpec((1, D_MODEL), const),
            _whole_vmem(),
            _whole_vmem(),
            pl.BlockSpec((1, ROUTE_LANES), const),
            pl.BlockSpec((1, ROUTE_LANES), const),
        ],
        out_specs=[
            pl.BlockSpec((tm, D_MODEL), row),
            pl.BlockSpec((tm * ROW_CHUNKS, LANES), row),
            pl.BlockSpec((tm, ROUTE_LANES), row),
            pl.BlockSpec((None, REC_ROWS, tm), lambda i: (i, 0, 0)),
            pl.BlockSpec((1, ROUTE_LANES), const),
        ],
        out_shape=[
            jax.ShapeDtypeStruct((t, D_MODEL), _F32),
            jax.ShapeDtypeStruct((t * ROW_CHUNKS, LANES), _F32),
            jax.ShapeDtypeStruct((t, ROUTE_LANES), _F32),
            jax.ShapeDtypeStruct((t // tm, REC_ROWS, tm), _I32),
            jax.ShapeDtypeStruct((1, ROUTE_LANES), _F32),
        ],
        scratch_shapes=[pltpu.VMEM((1, ROUTE_LANES), _F32)],
        compiler_params=_params(("arbitrary",)),
        name="mix",
    )(x, a, d, wp_bf, ps, wo_bf, g2, wrh, wrl, br, counts_in)


def _tokens(ref, first, n):
    return ref.at[pl.ds(pl.multiple_of(first * ROW_CHUNKS, ROW_CHUNKS), n * ROW_CHUNKS), :]


def _row_copy(src, dst, src_tok, dst_tok, sem):
    return pltpu.make_async_copy(_tokens(src, src_tok, 1), _tokens(dst, dst_tok, 1), sem)


def _dispatch_kernel(off_ref, zstart_ref, zpad_ref, rec_ref, h_ref, *rest, tm, zero_rows):
    if zero_rows:
        xs_ref, sem, zsem, zbuf = rest
    else:
        _, xs_ref, sem = rest
    i = pl.program_id(0)

    if zero_rows:
        def zero_fill(wait):
            def go(cp):
                cp.wait() if wait else cp.start()

            def body(e, c):
                cursor = zstart_ref[e]
                pad = zpad_ref[e]
                run = zero_rows // 2
                while run >= 1:
                    @pl.when((pad & run) != 0)
                    def _(cursor=cursor, run=run):
                        go(pltpu.make_async_copy(_tokens(zbuf, 0, run), _tokens(xs_ref, cursor, run), zsem))

                    cursor = cursor + (pad & run)
                    run //= 2
                return c

            lax.fori_loop(0, N_EXPERTS, body, 0)

            def tile(t, c):
                rows = zero_rows // 2
                go(pltpu.make_async_copy(_tokens(zbuf, 0, rows),
                                         _tokens(xs_ref, zstart_ref[N_EXPERTS] + t * rows, rows), zsem))
                return c

            lax.fori_loop(0, zpad_ref[N_EXPERTS], tile, 0)

        @pl.when(i == 0)
        def _():
            zbuf[...] = jnp.zeros_like(zbuf)
            zero_fill(False)
            zero_fill(True)

    def body(r, c):
        for k in range(2):
            dst = off_ref[rec_ref[0, k, r]] + rec_ref[0, 4 + k, r]
            _row_copy(h_ref, xs_ref, r, dst, sem).start()
        return c

    lax.fori_loop(0, tm, body, 0, unroll=DMA_UNROLL)
    for _ in range(2):
        pltpu.make_async_copy(h_ref, _tokens(xs_ref, 0, tm), sem).wait()


def _dispatch(off, zstart, zpad, rect, h2, xs, n_rows, zero_rows):
    n_tiles, _, tm = rect.shape
    kern = functools.partial(_dispatch_kernel, tm=tm, zero_rows=zero_rows)
    in_specs = [
        pl.BlockSpec((1, REC_ROWS, tm), lambda i, o, z, p: (i, 0, 0), memory_space=pltpu.SMEM),
        pl.BlockSpec((tm * ROW_CHUNKS, LANES), lambda i, o, z, p: (i, 0)),
    ]
    scratch = [pltpu.SemaphoreType.DMA(())]
    args = [off, zstart, zpad, rect, h2]
    aliases = {}
    if zero_rows:
        scratch += [pltpu.SemaphoreType.DMA(()), pltpu.VMEM((zero_rows // 2 * ROW_CHUNKS, LANES), _F32)]
    else:
        in_specs.append(pl.BlockSpec(memory_space=pl.ANY))
        args.append(xs)
        aliases = {5: 0}
    grid_spec = pltpu.PrefetchScalarGridSpec(
        num_scalar_prefetch=3, grid=(n_tiles,), in_specs=in_specs,
        out_specs=pl.BlockSpec(memory_space=pl.ANY), scratch_shapes=scratch)
    return pl.pallas_call(
        kern,
        grid_spec=grid_spec,
        out_shape=jax.ShapeDtypeStruct((n_rows * ROW_CHUNKS, LANES), _F32),
        input_output_aliases=aliases,
        compiler_params=_params(("arbitrary",)),
        name="dispatch",
    )(*args)


def _experts_kernel(texp_ref, nused_ref, x_ref, wg_ref, wu_ref, wd_ref, y_ref, wgb, wub, wdb, *, tm):
    i = pl.program_id(0)
    new_expert = (i == 0) | (texp_ref[i] != texp_ref[jnp.maximum(i - 1, 0)])

    @pl.when(new_expert)
    def _():
        wgb[...] = wg_ref[...].astype(_BF16)
        wub[...] = wu_ref[...].astype(_BF16)
        wdb[...] = wd_ref[...].astype(_BF16)

    @pl.when(i < nused_ref[0])
    def _():
        x = _load_token_major(x_ref, tm).astype(_BF16)
        hg = _dot(x, wgb[...])
        hu = _dot(x, wub[...])
        act = (hg * jax.nn.sigmoid(hg) * hu).astype(_BF16)
        y_ref[...] = _dot(act, wdb[...])

    @pl.when(i >= nused_ref[0])
    def _():
        y_ref[...] = jnp.zeros_like(y_ref)


def _experts(tile_expert, n_used, xs, w_gate, w_up, w_down, tm):
    n_tiles = tile_expert.shape[0]
    w_in_spec = pl.BlockSpec((None, D_MODEL, D_EXPERT), lambda i, te, nu: (te[i], 0, 0))
    grid_spec = pltpu.PrefetchScalarGridSpec(
        num_scalar_prefetch=2,
        grid=(n_tiles,),
        in_specs=[
            pl.BlockSpec((tm * ROW_CHUNKS, LANES), lambda i, te, nu: (jnp.minimum(i, nu[0] - 1), 0)),
            w_in_spec, w_in_spec,
            pl.BlockSpec((None, D_EXPERT, D_MODEL), lambda i, te, nu: (te[i], 0, 0)),
        ],
        out_specs=pl.BlockSpec((tm, D_MODEL), lambda i, te, nu: (i, 0)),
        scratch_shapes=[pltpu.VMEM((D_MODEL, D_EXPERT), _BF16), pltpu.VMEM((D_MODEL, D_EXPERT), _BF16),
                        pltpu.VMEM((D_EXPERT, D_MODEL), _BF16)],
    )
    return pl.pallas_call(
        functools.partial(_experts_kernel, tm=tm),
        grid_spec=grid_spec,
        out_shape=jax.ShapeDtypeStruct((n_tiles * tm, D_MODEL), _F32),
        compiler_params=_params(("arbitrary",)),
        name="experts",
    )(tile_expert, n_used, xs, w_gate, w_up, w_down)


def _combine_kernel(off_ref, rec_ref, recn_ref, x1_ref, route_ref, y_hbm, o_ref, r0, r1, sem, *, tm):
    i = pl.program_id(0)
    n = pl.num_programs(0)
    bufs = (r0, r1)

    def issue(rec, slot):
        def body(r, c):
            for k in range(2):
                src = off_ref[rec[0, k, r]] + rec[0, 4 + k, r]
                pltpu.make_async_copy(y_hbm.at[pl.ds(src, 1), :], bufs[k].at[slot, pl.ds(r, 1), :],
                                      sem.at[slot, k]).start()
            return c

        lax.fori_loop(0, tm, body, 0, unroll=DMA_UNROLL)

    @pl.when(i == 0)
    def _():
        issue(rec_ref, 0)

    @pl.when(i + 1 < n)
    def _():
        issue(recn_ref, (i + 1) % 2)

    slot = i % 2
    for k in range(2):
        pltpu.make_async_copy(y_hbm.at[pl.ds(0, tm), :], bufs[k].at[slot], sem.at[slot, k]).wait()
    rec = route_ref[...]
    o_ref[...] = x1_ref[...] + rec[:, 2:3] * r0[slot] + rec[:, 3:4] * r1[slot]


def _combine(off, rect, x1, route, y_sorted):
    n_tiles, _, tm = rect.shape
    t = x1.shape[0]
    kern = functools.partial(_combine_kernel, tm=tm)
    rec_spec = lambda nxt: pl.BlockSpec(
        (1, REC_ROWS, tm), lambda i, o: (jnp.minimum(i + nxt, n_tiles - 1), 0, 0), memory_space=pltpu.SMEM)
    grid_spec = pltpu.PrefetchScalarGridSpec(
        num_scalar_prefetch=1,
        grid=(n_tiles,),
        in_specs=[
            rec_spec(0), rec_spec(1),
            pl.BlockSpec((tm, D_MODEL), lambda i, o: (i, 0)),
            pl.BlockSpec((tm, ROUTE_LANES), lambda i, o: (i, 0)),
            pl.BlockSpec(memory_space=pl.ANY),
        ],
        out_specs=pl.BlockSpec((tm, D_MODEL), lambda i, o: (i, 0)),
        scratch_shapes=[pltpu.VMEM((2, tm, D_MODEL), _F32), pltpu.VMEM((2, tm, D_MODEL), _F32),
                        pltpu.SemaphoreType.DMA((2, 2))],
    )
    return pl.pallas_call(
        kern,
        grid_spec=grid_spec,
        out_shape=jax.ShapeDtypeStruct((t, D_MODEL), _F32),
        compiler_params=_params(("arbitrary",)),
        name="combine",
    )(off, rect, rect, x1, route, y_sorted)


def _rope_tables(pos):
    inv = ROPE_THETA ** (-jnp.arange(0, HEAD_DIM, 2, dtype=_F32) / HEAD_DIM)
    ang = pos.astype(_F32)[:, None] * inv[None, :]
    cos = jnp.cos(ang)
    sin = jnp.sin(ang)
    return jnp.tile(cos, (1, 4)), jnp.tile(jnp.concatenate([-sin, sin], axis=1), (1, 2))


def _head_sum_matrices():
    lane_head = jnp.arange(ATTN_WIDTH) // HEAD_DIM
    cols = jnp.arange(LANES)
    esum = (lane_head[:, None] == cols[None, :]).astype(_F32) / HEAD_DIM
    eexp = (cols[:, None] == lane_head[None, :]).astype(_F32)
    return esum.astype(_BF16), jnp.concatenate([eexp, eexp], axis=0).astype(_BF16)


def kernel(x_prompt, x_sample, cache_k, cache_v, state_pool, page_table, norm1_g, w_in, q_norm_g, k_norm_g,
           lambda_q1, lambda_k1, lambda_q2, lambda_k2, subln_g, w_pool, pool_scale, w_out, norm2_g, w_coarse,
           b_coarse, w_fine, b_fine, w_gate, w_up, w_down):
    depth = w_in.shape[0]
    assert depth == 1
    l = 0
    lam_init = 0.8 - 0.6 * math.exp(-0.3 * l)
    bp, sp, _ = x_prompt.shape
    bs, ts, _ = x_sample.shape
    tp, tsamp = bp * sp, bs * ts
    n_pages = page_table.shape[1]
    past = n_pages * PAGE_SIZE
    n_hc = 2 * N_HEADS

    w_in_bf = w_in[l].astype(_BF16)
    w_out_bf = w_out[l].astype(_BF16)
    w_pool_bf = w_pool[l].astype(_BF16)
    g1 = norm1_g[l][None, :]
    g2 = norm2_g[l][None, :]
    gq = jnp.tile(q_norm_g[l], n_hc)[None, :]
    gk = jnp.tile(k_norm_g[l], n_hc)[None, :]
    sg = subln_g[l][None, :]
    ps = pool_scale[l][None, :]
    lam_p = jnp.stack([lambda_q1[l], lambda_k1[l], lambda_q2[l], lambda_k2[l]])
    esum, eexp = _head_sum_matrices()
    pad = ROUTE_LANES - N_EXPERTS - N_EXPERT_GROUPS
    w_fine_flat = jnp.transpose(w_fine[l], (1, 0, 2)).reshape(D_MODEL, N_EXPERTS)
    w_route = jnp.concatenate([w_fine_flat, w_coarse[l], jnp.zeros((D_MODEL, pad), _F32)], axis=1)
    wrh, wrl = _split_bf16(w_route)
    b_route = jnp.concatenate([b_fine[l].reshape(-1), b_coarse[l], jnp.zeros((pad,), _F32)])[None, :]

    cos_p, sin_p = _rope_tables(jnp.arange(sp, dtype=jnp.int32))
    cos_s, sin_s = _rope_tables(past + jnp.arange(ts, dtype=jnp.int32))
    cos_s, sin_s = jnp.tile(cos_s, (bs, 1)), jnp.tile(sin_s, (bs, 1))
    xp2 = x_prompt.reshape(tp, D_MODEL)
    xs2 = x_sample.reshape(tsamp, D_MODEL)
    qp, kpt, kpt_bf, vp, vp_bf, up = _proj(xp2, g1, w_in_bf, gq, gk, cos_p, sin_p, esum, eexp, PROJ_TILE, sp)
    qs, ks, _, vs, _, us = _proj(xs2, g1, w_in_bf, gq, gk, cos_s, sin_s, esum, eexp, tsamp)

    q4 = qs.astype(_F32).reshape(bs, 1, 1, ts, ATTN_WIDTH)
    lane_hc = jnp.arange(ATTN_WIDTH) // HEAD_DIM
    head = jnp.arange(N_HEADS)[:, None, None, None]
    comp = jnp.arange(2)[None, :, None, None]
    keep = lane_hc[None, None, None, :] == head * 2 + comp
    qrep = jnp.where(keep[None], q4, 0.0).reshape(bs, QROWS, ATTN_WIDTH)
    new_rows = 8
    kn = jnp.pad(ks.reshape(bs, ts, ATTN_WIDTH), ((0, 0), (0, new_rows - ts), (0, 0)))
    vn = jnp.pad(vs.reshape(bs, ts, ATTN_WIDTH), ((0, 0), (0, new_rows - ts), (0, 0)))
    n_phys = cache_k.shape[1]
    ck = jnp.transpose(cache_k[l], (0, 2, 3, 1)).reshape(n_phys, ATTN_WIDTH, PAGE_SIZE)
    cv = cache_v[l].reshape(n_phys, PAGE_SIZE * N_HEADS, V_HEAD_DIM)
    shp = (bp, sp, ATTN_WIDTH)
    a_p, a_s = _attn(page_table, lam_p, sg, qp.reshape(shp), kpt_bf, vp_bf.reshape(shp), qrep, kn, vn, ck, cv,
                     ts, lam_init)
    a_p = a_p.reshape(tp, ATTN_WIDTH)
    a_s = a_s.reshape(bs, N_HEADS, 2, ts, V_HEAD_DIM)[:, :, 0]
    a_s = a_s.transpose(0, 2, 1, 3).reshape(tsamp, ATTN_WIDTH).astype(_BF16)

    hist_p = jnp.zeros((bp, HALO_ROWS, POOL_WIDTH), _F32)
    d_p = _poolwin(up, hist_p, MIX_TILE, sp // MIX_TILE, 0)
    seq_rows = 8
    us_pad = jnp.pad(us.reshape(bs, ts, POOL_WIDTH), ((0, 0), (0, seq_rows - ts), (0, 0)))
    hist_s = jnp.pad(state_pool[l], ((0, 0), (HALO_ROWS - POOL_HIST, 0), (0, 0)))
    d_s = _poolwin(us_pad.reshape(bs * seq_rows, POOL_WIDTH), hist_s, seq_rows, 1, past)
    d_s = d_s.reshape(bs, seq_rows, POOL_WIDTH)[:, :ts].reshape(tsamp, POOL_WIDTH)

    counts0 = jnp.zeros((1, ROUTE_LANES), _F32)
    x1_p, h2_p, route_p, rect_p, counts_p = _mix(xp2, a_p, d_p, w_pool_bf, ps, w_out_bf, g2, wrh, wrl, b_route,
                                                 counts0, MIX_TILE)
    x1_s, h2_s, route_s, rect_s, counts = _mix(xs2, a_s, d_s, w_pool_bf, ps, w_out_bf, g2, wrh, wrl, b_route,
                                               counts_p, tsamp)

    tm = EXPERT_TILE
    t_all = tp + tsamp
    n_tiles = (2 * t_all + N_EXPERTS * (tm - 1) + tm - 1) // tm
    cnt = counts[0, :N_EXPERTS].astype(jnp.int32)
    padded = (cnt + tm - 1) // tm * tm
    ends = jnp.cumsum(padded)
    off = ends - padded
    n_used = (ends[-1] // tm).astype(jnp.int32)
    tile_ids = jnp.arange(n_tiles, dtype=jnp.int32)
    te = jnp.sum((ends[None, :] <= (tile_ids * tm)[:, None]).astype(jnp.int32), axis=1)
    te = jnp.minimum(te, N_EXPERTS - 1)
    last = jnp.sum(jnp.where(tile_ids == n_used - 1, te, 0))
    tile_expert = jnp.where(tile_ids < n_used, te, last)

    n_rows = n_tiles * tm
    zstart = off + counts_p[0, :N_EXPERTS].astype(jnp.int32)
    zinfo = (jnp.concatenate([zstart, ends[-1:]]), jnp.concatenate([ends - zstart, (n_tiles - n_used)[None]]))
    assert tsamp <= tm
    xs = _dispatch(off, *zinfo, rect_p, h2_p, None, n_rows, 2 * tm)
    xs = _dispatch(off, *zinfo, rect_s, h2_s, xs, n_rows, 0)
    y_sorted = _experts(tile_expert, n_used.reshape(1), xs, w_gate[l], w_up[l], w_down[l], tm)
    y_p = _combine(off, rect_p, x1_p, route_p, y_sorted)
    y_s = _combine(off, rect_s, x1_s, route_s, y_sorted)

    new_k_p = jnp.transpose(kpt.reshape(bp, n_hc, HEAD_DIM, sp), (0, 3, 1, 2))
    new_pool_p = up.reshape(bp, sp, POOL_WIDTH)[:, sp - POOL_HIST:]
    new_pool_s = jnp.concatenate([state_pool[l], us.reshape(bs, ts, POOL_WIDTH)], axis=1)[:, -POOL_HIST:]
    return (y_p.reshape(bp, sp, D_MODEL),
            y_s.reshape(bs, ts, D_MODEL),
            new_k_p[None],
            vp.reshape(1, bp, sp, N_HEADS, V_HEAD_DIM),
            new_pool_p[None],
            ks.reshape(1, bs, ts, n_hc, HEAD_DIM),
            vs.reshape(1, bs, ts, N_HEADS, V_HEAD_DIM),
            new_pool_s[None])
```

```python
import functools
import math

import jax
import jax.numpy as jnp
from jax import lax
from jax.experimental import pallas as pl
from jax.experimental.pallas import tpu as pltpu

_F32 = jnp.float32
_BF16 = jnp.bfloat16
_I32 = jnp.int32

D_MODEL = 2048
ATTN_WIDTH = 1024
POOL_WIDTH = 1024
HEAD_DIM = 64
N_HEADS = 8
V_HEAD_DIM = 128
POOL_WINDOWS = (2, 4, 8, 16)
POOL_GROUP_WIDTH = 256
POOL_HIST = 15
HALO_ROWS = 16
N_EXPERTS = 32
EXPERTS_PER_GROUP = 8
N_EXPERT_GROUPS = 4
D_EXPERT = 256
ROPE_THETA = 10000.0
RMS_EPS = 1e-6
NEG_INF = -1e30
PAGE_SIZE = 128
LANES = 128
ROUTE_LANES = 128
REC_ROWS = 8
VMEM_LIMIT = 56 * 1024 * 1024

PROJ_TILE = 512
ATTN_TILE = 512
ATTN_WIDE = 4
MIX_TILE = 512
EXPERT_TILE = 256
QROWS = 64
DMA_UNROLL = 8


def _dot(a, b):
    return jnp.dot(a, b, preferred_element_type=_F32)


def _split_bf16(x):
    hi = x.astype(_BF16)
    lo = (x - hi.astype(_F32)).astype(_BF16)
    return hi, lo


def _params(sem):
    return pltpu.CompilerParams(dimension_semantics=sem, vmem_limit_bytes=VMEM_LIMIT)


ROW_CHUNKS = D_MODEL // LANES


def _load_token_major(ref, n):
    return jnp.concatenate([ref[pl.ds(c, n, stride=ROW_CHUNKS), :] for c in range(ROW_CHUNKS)], axis=1)


def _store_token_major(ref, x):
    n = x.shape[0]
    for c in range(ROW_CHUNKS):
        ref[pl.ds(c, n, stride=ROW_CHUNKS), :] = x[:, c * LANES:(c + 1) * LANES]


def _whole_vmem():
    return pl.BlockSpec(memory_space=pltpu.VMEM)


def _proj_kernel(x_ref, g1_ref, w_ref, gq_ref, gk_ref, cos_ref, sin_ref, esum_ref, eexp_ref,
                 q_ref, kf_ref, kb_ref, vf_ref, vb_ref, u_ref, *, k_transposed):
    x = x_ref[...]
    ms = jnp.mean(x * x, axis=-1, keepdims=True)
    h = (x * lax.rsqrt(ms + RMS_EPS) * g1_ref[...]).astype(_BF16)
    cos = cos_ref[...]
    sin = sin_ref[...]
    lane = lax.broadcasted_iota(_I32, (1, LANES), 1)
    upper = (lane & (HEAD_DIM - 1)) >= HEAD_DIM // 2

    def normed_rope(z, g_ref, outs, transposed):
        msq = _dot((z * z).astype(_BF16), esum_ref[...])
        rh, rl = _split_bf16(lax.rsqrt(msq + RMS_EPS))
        rb = _dot(jnp.concatenate([rh, rl], axis=1), eexp_ref[...])
        n = z * rb * g_ref[...]
        for j in range(ATTN_WIDTH // LANES):
            sl = slice(j * LANES, (j + 1) * LANES)
            nj = n[:, sl]
            swapped = jnp.where(upper, pltpu.roll(nj, HEAD_DIM // 2, 1),
                                pltpu.roll(nj, LANES - HEAD_DIM // 2, 1))
            o = nj * cos + swapped * sin
            if transposed:
                ot = o.T
                for ref, _ in outs:
                    ref[sl, :] = ot.astype(ref.dtype)
            else:
                for ref, scale in outs:
                    ref[:, sl] = (o * scale).astype(ref.dtype)

    a = ATTN_WIDTH
    normed_rope(_dot(h, w_ref[:, 0:a]), gq_ref, ((q_ref, HEAD_DIM ** -0.5),), False)
    normed_rope(_dot(h, w_ref[:, a:2 * a]), gk_ref, ((kf_ref, 1.0), (kb_ref, 1.0)), k_transposed)
    zv = _dot(h, w_ref[:, 2 * a:3 * a])
    vf_ref[...] = zv
    vb_ref[...] = zv.astype(_BF16)
    u_ref[...] = _dot(h, w_ref[:, 3 * a:])


def _proj(x, g1, w_bf, gq, gk, cos, sin, esum, eexp, tm, seq_len=None):
    t = x.shape[0]
    n_pos_tiles = cos.shape[0] // tm
    row = lambda i: (i, 0)
    const = lambda i: (0, 0)
    tok_spec = lambda w: pl.BlockSpec((tm, w), row)
    a = ATTN_WIDTH
    if seq_len is None:
        k_spec, k_shape = tok_spec(a), (t, a)
    else:
        tiles_per_seq = seq_len // tm
        k_spec = pl.BlockSpec((None, a, tm), lambda i: (i // tiles_per_seq, 0, i % tiles_per_seq))
        k_shape = (t // seq_len, a, seq_len)
    return pl.pallas_call(
        functools.partial(_proj_kernel, k_transposed=seq_len is not None),
        grid=(t // tm,),
        in_specs=[
            tok_spec(D_MODEL),
            pl.BlockSpec((1, D_MODEL), const),
            _whole_vmem(),
            pl.BlockSpec((1, a), const),
            pl.BlockSpec((1, a), const),
            pl.BlockSpec((tm, LANES), lambda i: (i % n_pos_tiles, 0)),
            pl.BlockSpec((tm, LANES), lambda i: (i % n_pos_tiles, 0)),
            _whole_vmem(),
            _whole_vmem(),
        ],
        out_specs=[tok_spec(a), k_spec, k_spec, tok_spec(a), tok_spec(a), tok_spec(POOL_WIDTH)],
        out_shape=[
            jax.ShapeDtypeStruct((t, a), _BF16),
            jax.ShapeDtypeStruct(k_shape, _F32),
            jax.ShapeDtypeStruct(k_shape, _BF16),
            jax.ShapeDtypeStruct((t, a), _F32),
            jax.ShapeDtypeStruct((t, a), _BF16),
            jax.ShapeDtypeStruct((t, POOL_WIDTH), _F32),
        ],
        compiler_params=_params(("arbitrary",)),
        name="proj",
    )(x, g1, w_bf, gq, gk, cos, sin, esum, eexp)


def _lambda(lam_ref, lam_init):
    lp = lam_ref[...]
    s1 = jnp.sum(lp[0:1] * lp[1:2], axis=-1, keepdims=True)
    s2 = jnp.sum(lp[2:3] * lp[3:4], axis=-1, keepdims=True)
    return jnp.exp(s1) - jnp.exp(s2) + lam_init


def _sub_norm(o, sg, lam_init):
    ms = jnp.mean(o * o, axis=-1, keepdims=True)
    return o * lax.rsqrt(ms + RMS_EPS) * sg * (1.0 - lam_init)


def _prompt_attention(lam_ref, sg_ref, q_ref, kt_ref, v_ref, o_ref, *, tile, wide, lam_init, side):
    qi = pl.program_id(2)
    q = q_ref[...]
    lane = lax.broadcasted_iota(_I32, (1, LANES), 1)
    zero = jnp.zeros_like(q)
    qc = (jnp.where(lane < HEAD_DIM, q, zero), jnp.where(lane >= HEAD_DIM, q, zero))

    def update(s, state, vblk):
        m, l, acc = state
        m_new = jnp.maximum(m, jnp.max(s, axis=-1, keepdims=True))
        alpha = jnp.exp(m - m_new)
        p = jnp.exp(s - m_new)
        l = alpha * l + jnp.sum(p, axis=-1, keepdims=True)
        acc = alpha * acc + _dot(p.astype(_BF16), vblk)
        return m_new, l, acc

    def block(first, carry, masked, width):
        start = pl.multiple_of(first * tile, tile)
        ktblk = kt_ref[:, pl.ds(start, width)]
        vblk = v_ref[pl.ds(start, width), :]
        out = []
        for c in range(2):
            s = _dot(qc[c], ktblk)
            if masked:
                r = lax.broadcasted_iota(_I32, (tile, width), 0)
                col = lax.broadcasted_iota(_I32, (tile, width), 1)
                s = jnp.where(col <= r + (width - tile), s, NEG_INF)
            out.append(update(s, carry[c], vblk))
        return tuple(out)

    init_one = (jnp.full((tile, 1), NEG_INF, _F32), jnp.zeros((tile, 1), _F32),
                jnp.zeros((tile, V_HEAD_DIM), _F32))
    n_full = qi // wide
    carry = lax.fori_loop(0, n_full, lambda j, c: block(j * wide, c, False, wide * tile), (init_one, init_one))
    lam = _lambda(lam_ref, lam_init)
    for last in range(1, wide + 1):
        @pl.when(qi + 1 - n_full * wide == last)
        def _(last=last):
            side_finish = side()
            (_, l0, a0), (_, l1, a1) = block(n_full * wide, carry, True, last * tile)
            o = a0 / l0 - lam * (a1 / l1)
            o_ref[...] = _sub_norm(o, sg_ref[...], lam_init).astype(o_ref.dtype)
            side_finish()


def _attn_kernel(pt_ref, lam_ref, sg_ref, q_ref, kt_ref, v_ref, qs_ref, kn_ref, vn_ref, spread_ref, ck_hbm, cv_hbm,
                 o_ref, os_ref, kbuf, vbuf, sem, m_ref, l_ref, acc_ref, *, tile, wide, lam_init, n_tok,
                 n_steps, pages_per_step, steps_per_elem):
    npg = pages_per_step
    step = (pl.program_id(0) * pl.num_programs(1) + pl.program_id(1)) * pl.num_programs(2) + pl.program_id(2)
    assert steps_per_elem & (steps_per_elem - 1) == 0
    part = step & (steps_per_elem - 1)

    def page_copies(s):
        elem = lax.shift_right_logical(s, steps_per_elem.bit_length() - 1)
        first_page = (s & (steps_per_elem - 1)) * npg
        base = (s & 1) * npg
        out = []
        for p in range(npg):
            page = pt_ref[elem, first_page + p]
            out.append(pltpu.make_async_copy(ck_hbm.at[page], kbuf.at[base + p], sem.at[0, base + p]))
            out.append(pltpu.make_async_copy(cv_hbm.at[page], vbuf.at[base + p], sem.at[1, base + p]))
        return out

    @pl.when(step == 0)
    def _():
        for cp in page_copies(step):
            cp.start()

    @pl.when(step + 1 < n_steps)
    def _():
        for cp in page_copies(step + 1):
            cp.start()

    def sample_pages():
        return _sample_pages_step(page_copies(step), (step & 1) * npg, part, lam_ref, sg_ref, qs_ref, kn_ref,
                                  vn_ref, spread_ref, os_ref, kbuf, vbuf, m_ref, l_ref, acc_ref, npg=npg, n_tok=n_tok,
                                  lam_init=lam_init, last_part=steps_per_elem - 1)

    _prompt_attention(lam_ref, sg_ref, q_ref, kt_ref, v_ref, o_ref, tile=tile, wide=wide, lam_init=lam_init,
                      side=sample_pages)


def _sample_pages_step(copies, base, part, lam_ref, sg_ref, q_ref, kn_ref, vn_ref, spread_ref, o_ref, kbuf, vbuf,
                       m_ref, l_ref, acc_ref, *, npg, n_tok, lam_init, last_part):
    for cp in copies:
        cp.wait()
    k_refs = [kbuf.at[base + p] for p in range(npg)]
    v_refs = [vbuf.at[base + p] for p in range(npg)]
    q = q_ref[...]
    rows_per_head = 2 * n_tok
    head_rows = lambda h: slice(h * rows_per_head, (h + 1) * rows_per_head)

    @pl.when(part == 0)
    def _():
        kn = kn_ref[...]
        vn = vn_ref[...]
        tok = lax.broadcasted_iota(_I32, (QROWS, 1), 0) & (n_tok - 1)
        ss = [jnp.where(tok >= jn, jnp.sum(q * kn[jn:jn + 1, :], axis=-1, keepdims=True), NEG_INF)
              for jn in range(n_tok)]
        m = functools.reduce(jnp.maximum, ss)
        ps = [jnp.exp(s - m) for s in ss]
        m_ref[...] = m
        l_ref[...] = functools.reduce(jnp.add, ps)
        for h in range(N_HEADS):
            lanes = slice(h * V_HEAD_DIM, (h + 1) * V_HEAD_DIM)
            acc_ref[head_rows(h), :] = functools.reduce(
                jnp.add, [ps[jn][head_rows(h)] * vn[jn:jn + 1, lanes] for jn in range(n_tok)])

    ss = [_dot(q, kr[...]) for kr in k_refs]
    m_old = m_ref[...]
    m_new = m_old
    for s in ss:
        m_new = jnp.maximum(m_new, jnp.max(s, axis=-1, keepdims=True))
    alpha = jnp.exp(m_old - m_new)
    ps = [jnp.exp(s - m_new) for s in ss]
    l = alpha * l_ref[...] + functools.reduce(jnp.add, [jnp.sum(p, axis=-1, keepdims=True) for p in ps])
    m_ref[...] = m_new
    l_ref[...] = l
    spread = _dot(jnp.concatenate(ps, axis=0).astype(_BF16), spread_ref[...])
    row_head = lax.broadcasted_iota(_I32, (QROWS, 1), 0) >> (rows_per_head.bit_length() - 1)
    own_head = (lax.broadcasted_iota(_I32, (1, PAGE_SIZE * N_HEADS), 1) & (N_HEADS - 1)) == row_head
    pv = functools.reduce(jnp.add, [
        _dot(jnp.where(own_head, spread[p * QROWS:(p + 1) * QROWS], 0.0).astype(_BF16), vr[...].astype(_BF16))
        for p, vr in enumerate(v_refs)])
    acc_ref[...] = alpha * acc_ref[...] + pv

    def finish():
        @pl.when(part == last_part)
        def _():
            o = acc_ref[...] / l_ref[...]
            lam = _lambda(lam_ref, lam_init)
            d = o - lam * pltpu.roll(o, QROWS - n_tok, 0)
            o_ref[...] = _sub_norm(d, sg_ref[...], lam_init)

    return finish


def _attn(page_table, lam_p, sg, q, kt, v, qrep, kn, vn, cache_k, cache_v, n_tok, lam_init):
    b, s, _ = q.shape
    tile = ATTN_TILE
    nq = s // tile
    n_steps = b * N_HEADS * nq
    nb, n_pages = page_table.shape
    assert (nb * n_pages) % n_steps == 0 and n_steps % nb == 0
    npg = nb * n_pages // n_steps
    steps_per_elem = n_steps // nb
    width = ATTN_WIDTH
    assert QROWS == 2 * N_HEADS * n_tok and cache_k.shape[1:] == (width, PAGE_SIZE)
    assert cache_v.shape[1:] == (PAGE_SIZE * N_HEADS, V_HEAD_DIM)

    spread = (jnp.arange(PAGE_SIZE)[:, None] == jnp.arange(PAGE_SIZE * N_HEADS)[None, :] // N_HEADS).astype(_BF16)
    elem = lambda bi, h, i: ((bi * N_HEADS + h) * nq + i) // steps_per_elem
    per_elem = lambda rows, cols: pl.BlockSpec((None, rows, cols), lambda bi, h, i, pt: (elem(bi, h, i), 0, 0))
    kern = functools.partial(_attn_kernel, tile=tile, wide=ATTN_WIDE, lam_init=lam_init, n_tok=n_tok,
                             n_steps=n_steps, pages_per_step=npg, steps_per_elem=steps_per_elem)
    grid_spec = pltpu.PrefetchScalarGridSpec(
        num_scalar_prefetch=1,
        grid=(b, N_HEADS, nq),
        in_specs=[
            pl.BlockSpec((4, HEAD_DIM), lambda bi, h, i, pt: (0, 0)),
            pl.BlockSpec((1, V_HEAD_DIM), lambda bi, h, i, pt: (0, 0)),
            pl.BlockSpec((None, tile, LANES), lambda bi, h, i, pt: (bi, i, h)),
            pl.BlockSpec((None, LANES, s), lambda bi, h, i, pt: (bi, h, 0)),
            pl.BlockSpec((None, s, LANES), lambda bi, h, i, pt: (bi, 0, h)),
            per_elem(QROWS, width), per_elem(kn.shape[1], width), per_elem(vn.shape[1], width),
            _whole_vmem(),
            pl.BlockSpec(memory_space=pl.ANY), pl.BlockSpec(memory_space=pl.ANY),
        ],
        out_specs=[pl.BlockSpec((None, tile, LANES), lambda bi, h, i, pt: (bi, i, h)),
                   per_elem(QROWS, V_HEAD_DIM)],
        scratch_shapes=[pltpu.VMEM((2 * npg, width, PAGE_SIZE), _F32),
                        pltpu.VMEM((2 * npg, PAGE_SIZE * N_HEADS, V_HEAD_DIM), _F32),
                        pltpu.SemaphoreType.DMA((2, 2 * npg)),
                        pltpu.VMEM((QROWS, 1), _F32), pltpu.VMEM((QROWS, 1), _F32),
                        pltpu.VMEM((QROWS, V_HEAD_DIM), _F32)],
    )
    return pl.pallas_call(
        kern,
        grid_spec=grid_spec,
        out_shape=[jax.ShapeDtypeStruct((b, s, ATTN_WIDTH), _BF16),
                   jax.ShapeDtypeStruct((nb, QROWS, V_HEAD_DIM), _F32)],
        compiler_params=_params(("arbitrary", "arbitrary", "arbitrary")),
        name="attn",
    )(page_table, lam_p, sg, q, kt, v, qrep, kn, vn, spread, cache_k, cache_v)


def _poolwin_kernel(u_ref, halo_ref, hist_ref, d_ref, *, tm, tiles_per_seq, start_pos):
    i = pl.program_id(0)
    t_in_seq = i % tiles_per_seq
    u = u_ref[...]
    halo = jnp.where(t_in_seq == 0, hist_ref[...], halo_ref[...])
    ext = jnp.concatenate([halo, u], axis=0)
    pos = start_pos + t_in_seq * tm + lax.broadcasted_iota(_I32, (tm, 1), 0)
    for g, w in enumerate(POOL_WINDOWS):
        sl = slice(g * POOL_GROUP_WIDTH, (g + 1) * POOL_GROUP_WIDTH)
        acc = ext[:, sl]
        span = 1
        while span < w:
            acc = acc + pltpu.roll(acc, span, 0)
            span *= 2
        cnt = jnp.minimum(pos + 1, w).astype(_F32)
        d_ref[:, sl] = (acc[HALO_ROWS:] / cnt - u[:, sl]).astype(d_ref.dtype)


def _poolwin(u, hist, tm, tiles_per_seq, start_pos):
    t = u.shape[0]
    halo_blocks = max(tm // HALO_ROWS, 1)
    kern = functools.partial(_poolwin_kernel, tm=tm, tiles_per_seq=tiles_per_seq, start_pos=start_pos)
    return pl.pallas_call(
        kern,
        grid=(t // tm,),
        in_specs=[
            pl.BlockSpec((tm, POOL_WIDTH), lambda i: (i, 0)),
            pl.BlockSpec((HALO_ROWS, POOL_WIDTH),
                         lambda i: (jnp.where(i % tiles_per_seq == 0, 0, i * halo_blocks - 1), 0)),
            pl.BlockSpec((None, HALO_ROWS, POOL_WIDTH), lambda i: (i // tiles_per_seq, 0, 0)),
        ],
        out_specs=pl.BlockSpec((tm, POOL_WIDTH), lambda i: (i, 0)),
        out_shape=jax.ShapeDtypeStruct((t, POOL_WIDTH), _BF16),
        compiler_params=_params(("arbitrary",)),
        name="poolwin",
    )(u, u, hist)


def _mix_kernel(x_ref, a_ref, d_ref, wp_ref, ps_ref, wo_ref, g2_ref, wrh_ref, wrl_ref, br_ref,
                cin_ref, x1_ref, h2_ref, route_ref, rect_ref, cout_ref, carry_ref, *, tm):
    i = pl.program_id(0)

    @pl.when(i == 0)
    def _():
        carry_ref[...] = cin_ref[...]

    d = d_ref[...]
    gw = POOL_GROUP_WIDTH
    py = jnp.concatenate([_dot(d[:, g * gw:(g + 1) * gw], wp_ref[g]) for g in range(len(POOL_WINDOWS))],
                         axis=1) * ps_ref[...]
    x1 = (x_ref[...] + _dot(a_ref[...], wo_ref[0:ATTN_WIDTH, :])
          + _dot(py.astype(_BF16), wo_ref[ATTN_WIDTH:, :]))
    x1_ref[...] = x1
    ms = jnp.mean(x1 * x1, axis=-1, keepdims=True)
    h2 = x1 * lax.rsqrt(ms + RMS_EPS) * g2_ref[...]
    _store_token_major(h2_ref, h2)

    hh, hl = _split_bf16(h2)
    logits = _dot(hh, wrh_ref[...]) + _dot(hl, wrh_ref[...]) + _dot(hh, wrl_ref[...]) + br_ref[...]
    lane = lax.broadcasted_iota(_I32, (1, ROUTE_LANES), 1)
    big = jnp.int32(ROUTE_LANES)
    is_coarse = (lane >= N_EXPERTS) & (lane < N_EXPERTS + N_EXPERT_GROUPS)
    lc = jnp.where(is_coarse, logits, NEG_INF)
    mc = jnp.max(lc, axis=-1, keepdims=True)
    g_idx = jnp.min(jnp.where(lc == mc, lane, big), axis=-1, keepdims=True) - N_EXPERTS
    p_g = 1.0 / jnp.sum(jnp.exp(lc - mc), axis=-1, keepdims=True)
    in_group = (lane < N_EXPERTS) & ((lane >> (EXPERTS_PER_GROUP.bit_length() - 1)) == g_idx)
    lf = jnp.where(in_group, logits, NEG_INF)
    v1 = jnp.max(lf, axis=-1, keepdims=True)
    i1 = jnp.min(jnp.where(lf == v1, lane, big), axis=-1, keepdims=True)
    lf2 = jnp.where(lane == i1, NEG_INF, lf)
    v2 = jnp.max(lf2, axis=-1, keepdims=True)
    i2 = jnp.min(jnp.where(lf2 == v2, lane, big), axis=-1, keepdims=True)
    e21 = jnp.exp(v2 - v1)
    w1 = p_g / (1.0 + e21)
    w2 = p_g * e21 / (1.0 + e21)

    oh1 = lane == i1
    oh2 = lane == i2
    onehot = (oh1 | oh2).astype(_F32)
    r = lax.broadcasted_iota(_I32, (tm, tm), 0)
    col = lax.broadcasted_iota(_I32, (tm, tm), 1)
    lower = (col < r).astype(_BF16)
    before = _dot(lower, onehot.astype(_BF16)) + carry_ref[...]
    rank1 = jnp.sum(jnp.where(oh1, before, 0.0), axis=-1, keepdims=True)
    rank2 = jnp.sum(jnp.where(oh2, before, 0.0), axis=-1, keepdims=True)
    carry = carry_ref[...] + jnp.sum(onehot, axis=0, keepdims=True)
    carry_ref[...] = carry
    cout_ref[...] = carry

    rec = jnp.zeros((tm, ROUTE_LANES), _F32)
    for k, val in enumerate((i1.astype(_F32), i2.astype(_F32), w1, w2, rank1, rank2)):
        rec = jnp.where(lane == k, val, rec)
    route_ref[...] = rec
    rect_ref[...] = rec.T[0:REC_ROWS, :].astype(_I32)


def _mix(x, a, d, wp_bf, ps, wo_bf, g2, wrh, wrl, br, counts_in, tm):
    t = x.shape[0]
    row = lambda i: (i, 0)
    const = lambda i: (0, 0)
    kern = functools.partial(_mix_kernel, tm=tm)
    return pl.pallas_call(
        kern,
        grid=(t // tm,),
        in_specs=[
            pl.BlockSpec((tm, D_MODEL), row),
            pl.BlockSpec((tm, ATTN_WIDTH), row),
            pl.BlockSpec((tm, POOL_WIDTH), row),
            _whole_vmem(),
            pl.BlockSpec((1, POOL_WIDTH), const),
            _whole_vmem(),
            pl.BlockSpec((1, D_MODEL), const),
            _whole_vmem(),
            _whole_vmem(),
            pl.BlockSpec((1, ROUTE_LANES), const),
            pl.BlockSpec((1, ROUTE_LANES), const),
        ],
        out_specs=[
            pl.BlockSpec((tm, D_MODEL), row),
            pl.BlockSpec((tm * ROW_CHUNKS, LANES), row),
            pl.BlockSpec((tm, ROUTE_LANES), row),
            pl.BlockSpec((None, REC_ROWS, tm), lambda i: (i, 0, 0)),
            pl.BlockSpec((1, ROUTE_LANES), const),
        ],
        out_shape=[
            jax.ShapeDtypeStruct((t, D_MODEL), _F32),
            jax.ShapeDtypeStruct((t * ROW_CHUNKS, LANES), _F32),
            jax.ShapeDtypeStruct((t, ROUTE_LANES), _F32),
            jax.ShapeDtypeStruct((t // tm, REC_ROWS, tm), _I32),
            jax.ShapeDtypeStruct((1, ROUTE_LANES), _F32),
        ],
        scratch_shapes=[pltpu.VMEM((1, ROUTE_LANES), _F32)],
        compiler_params=_params(("arbitrary",)),
        name="mix",
    )(x, a, d, wp_bf, ps, wo_bf, g2, wrh, wrl, br, counts_in)


def _tokens(ref, first, n):
    return ref.at[pl.ds(pl.multiple_of(first * ROW_CHUNKS, ROW_CHUNKS), n * ROW_CHUNKS), :]


def _row_copy(src, dst, src_tok, dst_tok, sem):
    return pltpu.make_async_copy(_tokens(src, src_tok, 1), _tokens(dst, dst_tok, 1), sem)


def _dispatch_kernel(off_ref, zstart_ref, zpad_ref, rec_ref, h_ref, *rest, tm, zero_rows):
    if zero_rows:
        xs_ref, sem, zsem, zbuf = rest
    else:
        _, xs_ref, sem = rest
    i = pl.program_id(0)

    if zero_rows:
        def zero_fill(wait):
            def go(cp):
                cp.wait() if wait else cp.start()

            def body(e, c):
                cursor = zstart_ref[e]
                pad = zpad_ref[e]
                run = zero_rows // 2
                while run >= 1:
                    @pl.when((pad & run) != 0)
                    def _(cursor=cursor, run=run):
                        go(pltpu.make_async_copy(_tokens(zbuf, 0, run), _tokens(xs_ref, cursor, run), zsem))

                    cursor = cursor + (pad & run)
                    run //= 2
                return c

            lax.fori_loop(0, N_EXPERTS, body, 0)

            def tile(t, c):
                rows = zero_rows // 2
                go(pltpu.make_async_copy(_tokens(zbuf, 0, rows),
                                         _tokens(xs_ref, zstart_ref[N_EXPERTS] + t * rows, rows), zsem))
                return c

            lax.fori_loop(0, zpad_ref[N_EXPERTS], tile, 0)

        @pl.when(i == 0)
        def _():
            zbuf[...] = jnp.zeros_like(zbuf)
            zero_fill(False)
            zero_fill(True)

    def body(r, c):
        for k in range(2):
            dst = off_ref[rec_ref[0, k, r]] + rec_ref[0, 4 + k, r]
            _row_copy(h_ref, xs_ref, r, dst, sem).start()
        return c

    lax.fori_loop(0, tm, body, 0, unroll=DMA_UNROLL)
    for _ in range(2):
        pltpu.make_async_copy(h_ref, _tokens(xs_ref, 0, tm), sem).wait()


def _dispatch(off, zstart, zpad, rect, h2, xs, n_rows, zero_rows):
    n_tiles, _, tm = rect.shape
    kern = functools.partial(_dispatch_kernel, tm=tm, zero_rows=zero_rows)
    in_specs = [
        pl.BlockSpec((1, REC_ROWS, tm), lambda i, o, z, p: (i, 0, 0), memory_space=pltpu.SMEM),
        pl.BlockSpec((tm * ROW_CHUNKS, LANES), lambda i, o, z, p: (i, 0)),
    ]
    scratch = [pltpu.SemaphoreType.DMA(())]
    args = [off, zstart, zpad, rect, h2]
    aliases = {}
    if zero_rows:
        scratch += [pltpu.SemaphoreType.DMA(()), pltpu.VMEM((zero_rows // 2 * ROW_CHUNKS, LANES), _F32)]
    else:
        in_specs.append(pl.BlockSpec(memory_space=pl.ANY))
        args.append(xs)
        aliases = {5: 0}
    grid_spec = pltpu.PrefetchScalarGridSpec(
        num_scalar_prefetch=3, grid=(n_tiles,), in_specs=in_specs,
        out_specs=pl.BlockSpec(memory_space=pl.ANY), scratch_shapes=scratch)
    return pl.pallas_call(
        kern,
        grid_spec=grid_spec,
        out_shape=jax.ShapeDtypeStruct((n_rows * ROW_CHUNKS, LANES), _F32),
        input_output_aliases=aliases,
        compiler_params=_params(("arbitrary",)),
        name="dispatch",
    )(*args)


def _experts_kernel(texp_ref, nused_ref, x_ref, wg_ref, wu_ref, wd_ref, y_ref, wgb, wub, wdb, *, tm):
    i = pl.program_id(0)
    new_expert = (i == 0) | (texp_ref[i] != texp_ref[jnp.maximum(i - 1, 0)])

    @pl.when(new_expert)
    def _():
        wgb[...] = wg_ref[...].astype(_BF16)
        wub[...] = wu_ref[...].astype(_BF16)
        wdb[...] = wd_ref[...].astype(_BF16)

    @pl.when(i < nused_ref[0])
    def _():
        x = _load_token_major(x_ref, tm).astype(_BF16)
        hg = _dot(x, wgb[...])
        hu = _dot(x, wub[...])
        act = (hg * jax.nn.sigmoid(hg) * hu).astype(_BF16)
        y_ref[...] = _dot(act, wdb[...])

    @pl.when(i >= nused_ref[0])
    def _():
        y_ref[...] = jnp.zeros_like(y_ref)


def _experts(tile_expert, n_used, xs, w_gate, w_up, w_down, tm):
    n_tiles = tile_expert.shape[0]
    w_in_spec = pl.BlockSpec((None, D_MODEL, D_EXPERT), lambda i, te, nu: (te[i], 0, 0))
    grid_spec = pltpu.PrefetchScalarGridSpec(
        num_scalar_prefetch=2,
        grid=(n_tiles,),
        in_specs=[
            pl.BlockSpec((tm * ROW_CHUNKS, LANES), lambda i, te, nu: (jnp.minimum(i, nu[0] - 1), 0)),
            w_in_spec, w_in_spec,
            pl.BlockSpec((None, D_EXPERT, D_MODEL), lambda i, te, nu: (te[i], 0, 0)),
        ],
        out_specs=pl.BlockSpec((tm, D_MODEL), lambda i, te, nu: (i, 0)),
        scratch_shapes=[pltpu.VMEM((D_MODEL, D_EXPERT), _BF16), pltpu.VMEM((D_MODEL, D_EXPERT), _BF16),
                        pltpu.VMEM((D_EXPERT, D_MODEL), _BF16)],
    )
    return pl.pallas_call(
        functools.partial(_experts_kernel, tm=tm),
        grid_spec=grid_spec,
        out_shape=jax.ShapeDtypeStruct((n_tiles * tm, D_MODEL), _F32),
        compiler_params=_params(("arbitrary",)),
        name="experts",
    )(tile_expert, n_used, xs, w_gate, w_up, w_down)


def _combine_kernel(off_ref, rec_ref, recn_ref, x1_ref, route_ref, y_hbm, o_ref, r0, r1, sem, *, tm):
    i = pl.program_id(0)
    n = pl.num_programs(0)
    bufs = (r0, r1)

    def issue(rec, slot):
        def body(r, c):
            for k in range(2):
                src = off_ref[rec[0, k, r]] + rec[0, 4 + k, r]
                pltpu.make_async_copy(y_hbm.at[pl.ds(src, 1), :], bufs[k].at[slot, pl.ds(r, 1), :],
                                      sem.at[slot, k]).start()
            return c

        lax.fori_loop(0, tm, body, 0, unroll=DMA_UNROLL)

    @pl.when(i == 0)
    def _():
        issue(rec_ref, 0)

    @pl.when(i + 1 < n)
    def _():
        issue(recn_ref, (i + 1) % 2)

    slot = i % 2
    for k in range(2):
        pltpu.make_async_copy(y_hbm.at[pl.ds(0, tm), :], bufs[k].at[slot], sem.at[slot, k]).wait()
    rec = route_ref[...]
    o_ref[...] = x1_ref[...] + rec[:, 2:3] * r0[slot] + rec[:, 3:4] * r1[slot]


def _combine(off, rect, x1, route, y_sorted):
    n_tiles, _, tm = rect.shape
    t = x1.shape[0]
    kern = functools.partial(_combine_kernel, tm=tm)
    rec_spec = lambda nxt: pl.BlockSpec(
        (1, REC_ROWS, tm), lambda i, o: (jnp.minimum(i + nxt, n_tiles - 1), 0, 0), memory_space=pltpu.SMEM)
    grid_spec = pltpu.PrefetchScalarGridSpec(
        num_scalar_prefetch=1,
        grid=(n_tiles,),
        in_specs=[
            rec_spec(0), rec_spec(1),
            pl.BlockSpec((tm, D_MODEL), lambda i, o: (i, 0)),
            pl.BlockSpec((tm, ROUTE_LANES), lambda i, o: (i, 0)),
            pl.BlockSpec(memory_space=pl.ANY),
        ],
        out_specs=pl.BlockSpec((tm, D_MODEL), lambda i, o: (i, 0)),
        scratch_shapes=[pltpu.VMEM((2, tm, D_MODEL), _F32), pltpu.VMEM((2, tm, D_MODEL), _F32),
                        pltpu.SemaphoreType.DMA((2, 2))],
    )
    return pl.pallas_call(
        kern,
        grid_spec=grid_spec,
        out_shape=jax.ShapeDtypeStruct((t, D_MODEL), _F32),
        compiler_params=_params(("arbitrary",)),
        name="combine",
    )(off, rect, rect, x1, route, y_sorted)


def _rope_tables(pos):
    inv = ROPE_THETA ** (-jnp.arange(0, HEAD_DIM, 2, dtype=_F32) / HEAD_DIM)
    ang = pos.astype(_F32)[:, None] * inv[None, :]
    cos = jnp.cos(ang)
    sin = jnp.sin(ang)
    return jnp.tile(cos, (1, 4)), jnp.tile(jnp.concatenate([-sin, sin], axis=1), (1, 2))


def _head_sum_matrices():
    lane_head = jnp.arange(ATTN_WIDTH) // HEAD_DIM
    cols = jnp.arange(LANES)
    esum = (lane_head[:, None] == cols[None, :]).astype(_F32) / HEAD_DIM
    eexp = (cols[:, None] == lane_head[None, :]).astype(_F32)
    return esum.astype(_BF16), jnp.concatenate([eexp, eexp], axis=0).astype(_BF16)


def kernel(x_prompt, x_sample, cache_k, cache_v, state_pool, page_table, norm1_g, w_in, q_norm_g, k_norm_g,
           lambda_q1, lambda_k1, lambda_q2, lambda_k2, subln_g, w_pool, pool_scale, w_out, norm2_g, w_coarse,
           b_coarse, w_fine, b_fine, w_gate, w_up, w_down):
    depth = w_in.shape[0]
    assert depth == 1
    l = 0
    lam_init = 0.8 - 0.6 * math.exp(-0.3 * l)
    bp, sp, _ = x_prompt.shape
    bs, ts, _ = x_sample.shape
    tp, tsamp = bp * sp, bs * ts
    n_pages = page_table.shape[1]
    past = n_pages * PAGE_SIZE
    n_hc = 2 * N_HEADS

    w_in_bf = w_in[l].astype(_BF16)
    w_out_bf = w_out[l].astype(_BF16)
    w_pool_bf = w_pool[l].astype(_BF16)
    g1 = norm1_g[l][None, :]
    g2 = norm2_g[l][None, :]
    gq = jnp.tile(q_norm_g[l], n_hc)[None, :]
    gk = jnp.tile(k_norm_g[l], n_hc)[None, :]
    sg = subln_g[l][None, :]
    ps = pool_scale[l][None, :]
    lam_p = jnp.stack([lambda_q1[l], lambda_k1[l], lambda_q2[l], lambda_k2[l]])
    esum, eexp = _head_sum_matrices()
    pad = ROUTE_LANES - N_EXPERTS - N_EXPERT_GROUPS
    w_fine_flat = jnp.transpose(w_fine[l], (1, 0, 2)).reshape(D_MODEL, N_EXPERTS)
    w_route = jnp.concatenate([w_fine_flat, w_coarse[l], jnp.zeros((D_MODEL, pad), _F32)], axis=1)
    wrh, wrl = _split_bf16(w_route)
    b_route = jnp.concatenate([b_fine[l].reshape(-1), b_coarse[l], jnp.zeros((pad,), _F32)])[None, :]

    cos_p, sin_p = _rope_tables(jnp.arange(sp, dtype=jnp.int32))
    cos_s, sin_s = _rope_tables(past + jnp.arange(ts, dtype=jnp.int32))
    cos_s, sin_s = jnp.tile(cos_s, (bs, 1)), jnp.tile(sin_s, (bs, 1))
    xp2 = x_prompt.reshape(tp, D_MODEL)
    xs2 = x_sample.reshape(tsamp, D_MODEL)
    qp, kpt, kpt_bf, vp, vp_bf, up = _proj(xp2, g1, w_in_bf, gq, gk, cos_p, sin_p, esum, eexp, PROJ_TILE, sp)
    qs, ks, _, vs, _, us = _proj(xs2, g1, w_in_bf, gq, gk, cos_s, sin_s, esum, eexp, tsamp)

    q4 = qs.astype(_F32).reshape(bs, 1, 1, ts, ATTN_WIDTH)
    lane_hc = jnp.arange(ATTN_WIDTH) // HEAD_DIM
    head = jnp.arange(N_HEADS)[:, None, None, None]
    comp = jnp.arange(2)[None, :, None, None]
    keep = lane_hc[None, None, None, :] == head * 2 + comp
    qrep = jnp.where(keep[None], q4, 0.0).reshape(bs, QROWS, ATTN_WIDTH)
    new_rows = 8
    kn = jnp.pad(ks.reshape(bs, ts, ATTN_WIDTH), ((0, 0), (0, new_rows - ts), (0, 0)))
    vn = jnp.pad(vs.reshape(bs, ts, ATTN_WIDTH), ((0, 0), (0, new_rows - ts), (0, 0)))
    n_phys = cache_k.shape[1]
    ck = jnp.transpose(cache_k[l], (0, 2, 3, 1)).reshape(n_phys, ATTN_WIDTH, PAGE_SIZE)
    cv = cache_v[l].reshape(n_phys, PAGE_SIZE * N_HEADS, V_HEAD_DIM)
    shp = (bp, sp, ATTN_WIDTH)
    a_p, a_s = _attn(page_table, lam_p, sg, qp.reshape(shp), kpt_bf, vp_bf.reshape(shp), qrep, kn, vn, ck, cv,
                     ts, lam_init)
    a_p = a_p.reshape(tp, ATTN_WIDTH)
    a_s = a_s.reshape(bs, N_HEADS, 2, ts, V_HEAD_DIM)[:, :, 0]
    a_s = a_s.transpose(0, 2, 1, 3).reshape(tsamp, ATTN_WIDTH).astype(_BF16)

    hist_p = jnp.zeros((bp, HALO_ROWS, POOL_WIDTH), _F32)
    d_p = _poolwin(up, hist_p, MIX_TILE, sp // MIX_TILE, 0)
    seq_rows = 8
    us_pad = jnp.pad(us.reshape(bs, ts, POOL_WIDTH), ((0, 0), (0, seq_rows - ts), (0, 0)))
    hist_s = jnp.pad(state_pool[l], ((0, 0), (HALO_ROWS - POOL_HIST, 0), (0, 0)))
    d_s = _poolwin(us_pad.reshape(bs * seq_rows, POOL_WIDTH), hist_s, seq_rows, 1, past)
    d_s = d_s.reshape(bs, seq_rows, POOL_WIDTH)[:, :ts].reshape(tsamp, POOL_WIDTH)

    counts0 = jnp.zeros((1, ROUTE_LANES), _F32)
    x1_p, h2_p, route_p, rect_p, counts_p = _mix(xp2, a_p, d_p, w_pool_bf, ps, w_out_bf, g2, wrh, wrl, b_route,
                                                 counts0, MIX_TILE)
    x1_s, h2_s, route_s, rect_s, counts = _mix(xs2, a_s, d_s, w_pool_bf, ps, w_out_bf, g2, wrh, wrl, b_route,
                                               counts_p, tsamp)

    tm = EXPERT_TILE
    t_all = tp + tsamp
    n_tiles = (2 * t_all + N_EXPERTS * (tm - 1) + tm - 1) // tm
    cnt = counts[0, :N_EXPERTS].astype(jnp.int32)
    padded = (cnt + tm - 1) // tm * tm
    ends = jnp.cumsum(padded)
    off = ends - padded
    n_used = (ends[-1] // tm).astype(jnp.int32)
    tile_ids = jnp.arange(n_tiles, dtype=jnp.int32)
    te = jnp.sum((ends[None, :] <= (tile_ids * tm)[:, None]).astype(jnp.int32), axis=1)
    te = jnp.minimum(te, N_EXPERTS - 1)
    last = jnp.sum(jnp.where(tile_ids == n_used - 1, te, 0))
    tile_expert = jnp.where(tile_ids < n_used, te, last)

    n_rows = n_tiles * tm
    zstart = off + counts_p[0, :N_EXPERTS].astype(jnp.int32)
    zinfo = (jnp.concatenate([zstart, ends[-1:]]), jnp.concatenate([ends - zstart, (n_tiles - n_used)[None]]))
    assert tsamp <= tm
    xs = _dispatch(off, *zinfo, rect_p, h2_p, None, n_rows, 2 * tm)
    xs = _dispatch(off, *zinfo, rect_s, h2_s, xs, n_rows, 0)
    y_sorted = _experts(tile_expert, n_used.reshape(1), xs, w_gate[l], w_up[l], w_down[l], tm)
    y_p = _combine(off, rect_p, x1_p, route_p, y_sorted)
    y_s = _combine(off, rect_s, x1_s, route_s, y_sorted)

    new_k_p = jnp.transpose(kpt.reshape(bp, n_hc, HEAD_DIM, sp), (0, 3, 1, 2))
    new_pool_p = up.reshape(bp, sp, POOL_WIDTH)[:, sp - POOL_HIST:]
    new_pool_s = jnp.concatenate([state_pool[l], us.reshape(bs, ts, POOL_WIDTH)], axis=1)[:, -POOL_HIST:]
    return (y_p.reshape(bp, sp, D_MODEL),
            y_s.reshape(bs, ts, D_MODEL),
            new_k_p[None],
            vp.reshape(1, bp, sp, N_HEADS, V_HEAD_DIM),
            new_pool_p[None],
            ks.reshape(1, bs, ts, n_hc, HEAD_DIM),
            vs.reshape(1, bs, ts, N_HEADS, V_HEAD_DIM),
            new_pool_s[None])
```

```python
import functools
import math

import jax
import jax.numpy as jnp
from jax import lax
from jax.experimental import pallas as pl
from jax.experimental.pallas import tpu as pltpu

_F32 = jnp.float32
_BF16 = jnp.bfloat16
_I32 = jnp.int32

D_MODEL = 2048
ATTN_WIDTH = 1024
POOL_WIDTH = 1024
HEAD_DIM = 64
N_HEADS = 8
V_HEAD_DIM = 128
POOL_WINDOWS = (2, 4, 8, 16)
POOL_GROUP_WIDTH = 256
POOL_HIST = 15
HALO_ROWS = 16
N_EXPERTS = 32
EXPERTS_PER_GROUP = 8
N_EXPERT_GROUPS = 4
D_EXPERT = 256
ROPE_THETA = 10000.0
RMS_EPS = 1e-6
NEG_INF = -1e30
PAGE_SIZE = 128
LANES = 128
ROUTE_LANES = 128
REC_ROWS = 8
VMEM_LIMIT = 56 * 1024 * 1024

PROJ_TILE = 512
ATTN_TILE = 512
ATTN_WIDE = 4
MIX_TILE = 512
EXPERT_TILE = 256
QROWS = 64
DMA_UNROLL = 8


def _dot(a, b):
    return jnp.dot(a, b, preferred_element_type=_F32)


def _split_bf16(x):
    hi = x.astype(_BF16)
    lo = (x - hi.astype(_F32)).astype(_BF16)
    return hi, lo


def _params(sem):
    return pltpu.CompilerParams(dimension_semantics=sem, vmem_limit_bytes=VMEM_LIMIT)


ROW_CHUNKS = D_MODEL // LANES


def _load_token_major(ref, n):
    return jnp.concatenate([ref[pl.ds(c, n, stride=ROW_CHUNKS), :] for c in range(ROW_CHUNKS)], axis=1)


def _store_token_major(ref, x):
    n = x.shape[0]
    for c in range(ROW_CHUNKS):
        ref[pl.ds(c, n, stride=ROW_CHUNKS), :] = x[:, c * LANES:(c + 1) * LANES]


def _whole_vmem():
    return pl.BlockSpec(memory_space=pltpu.VMEM)


def _proj_kernel(x_ref, g1_ref, w_ref, gq_ref, gk_ref, cos_ref, sin_ref, esum_ref, eexp_ref,
                 q_ref, kf_ref, kb_ref, vf_ref, vb_ref, u_ref, *, k_transposed):
    x = x_ref[...]
    ms = jnp.mean(x * x, axis=-1, keepdims=True)
    h = (x * lax.rsqrt(ms + RMS_EPS) * g1_ref[...]).astype(_BF16)
    cos = cos_ref[...]
    sin = sin_ref[...]
    lane = lax.broadcasted_iota(_I32, (1, LANES), 1)
    upper = (lane & (HEAD_DIM - 1)) >= HEAD_DIM // 2

    def normed_rope(z, g_ref, outs, transposed):
        msq = _dot((z * z).astype(_BF16), esum_ref[...])
        rh, rl = _split_bf16(lax.rsqrt(msq + RMS_EPS))
        rb = _dot(jnp.concatenate([rh, rl], axis=1), eexp_ref[...])
        n = z * rb * g_ref[...]
        for j in range(ATTN_WIDTH // LANES):
            sl = slice(j * LANES, (j + 1) * LANES)
            nj = n[:, sl]
            swapped = jnp.where(upper, pltpu.roll(nj, HEAD_DIM // 2, 1),
                                pltpu.roll(nj, LANES - HEAD_DIM // 2, 1))
            o = nj * cos + swapped * sin
            if transposed:
                ot = o.T
                for ref, _ in outs:
                    ref[sl, :] = ot.astype(ref.dtype)
            else:
                for ref, scale in outs:
                    ref[:, sl] = (o * scale).astype(ref.dtype)

    a = ATTN_WIDTH
    normed_rope(_dot(h, w_ref[:, 0:a]), gq_ref, ((q_ref, HEAD_DIM ** -0.5),), False)
    normed_rope(_dot(h, w_ref[:, a:2 * a]), gk_ref, ((kf_ref, 1.0), (kb_ref, 1.0)), k_transposed)
    zv = _dot(h, w_ref[:, 2 * a:3 * a])
    vf_ref[...] = zv
    vb_ref[...] = zv.astype(_BF16)
    u_ref[...] = _dot(h, w_ref[:, 3 * a:])


def _proj(x, g1, w_bf, gq, gk, cos, sin, esum, eexp, tm, seq_len=None):
    t = x.shape[0]
    n_pos_tiles = cos.shape[0] // tm
    row = lambda i: (i, 0)
    const = lambda i: (0, 0)
    tok_spec = lambda w: pl.BlockSpec((tm, w), row)
    a = ATTN_WIDTH
    if seq_len is None:
        k_spec, k_shape = tok_spec(a), (t, a)
    else:
        tiles_per_seq = seq_len // tm
        k_spec = pl.BlockSpec((None, a, tm), lambda i: (i // tiles_per_seq, 0, i % tiles_per_seq))
        k_shape = (t // seq_len, a, seq_len)
    return pl.pallas_call(
        functools.partial(_proj_kernel, k_transposed=seq_len is not None),
        grid=(t // tm,),
        in_specs=[
            tok_spec(D_MODEL),
            pl.BlockSpec((1, D_MODEL), const),
            _whole_vmem(),
            pl.BlockSpec((1, a), const),
            pl.BlockSpec((1, a), const),
            pl.BlockSpec((tm, LANES), lambda i: (i % n_pos_tiles, 0)),
            pl.BlockSpec((tm, LANES), lambda i: (i % n_pos_tiles, 0)),
            _whole_vmem(),
            _whole_vmem(),
        ],
        out_specs=[tok_spec(a), k_spec, k_spec, tok_spec(a), tok_spec(a), tok_spec(POOL_WIDTH)],
        out_shape=[
            jax.ShapeDtypeStruct((t, a), _BF16),
            jax.ShapeDtypeStruct(k_shape, _F32),
            jax.ShapeDtypeStruct(k_shape, _BF16),
            jax.ShapeDtypeStruct((t, a), _F32),
            jax.ShapeDtypeStruct((t, a), _BF16),
            jax.ShapeDtypeStruct((t, POOL_WIDTH), _F32),
        ],
        compiler_params=_params(("arbitrary",)),
        name="proj",
    )(x, g1, w_bf, gq, gk, cos, sin, esum, eexp)


def _lambda(lam_ref, lam_init):
    lp = lam_ref[...]
    s1 = jnp.sum(lp[0:1] * lp[1:2], axis=-1, keepdims=True)
    s2 = jnp.sum(lp[2:3] * lp[3:4], axis=-1, keepdims=True)
    return jnp.exp(s1) - jnp.exp(s2) + lam_init


def _sub_norm(o, sg, lam_init):
    ms = jnp.mean(o * o, axis=-1, keepdims=True)
    return o * lax.rsqrt(ms + RMS_EPS) * sg * (1.0 - lam_init)


def _prompt_attention(lam_ref, sg_ref, q_ref, kt_ref, v_ref, o_ref, *, tile, wide, lam_init, side):
    qi = pl.program_id(2)
    q = q_ref[...]
    lane = lax.broadcasted_iota(_I32, (1, LANES), 1)
    zero = jnp.zeros_like(q)
    qc = (jnp.where(lane < HEAD_DIM, q, zero), jnp.where(lane >= HEAD_DIM, q, zero))

    def update(s, state, vblk):
        m, l, acc = state
        m_new = jnp.maximum(m, jnp.max(s, axis=-1, keepdims=True))
        alpha = jnp.exp(m - m_new)
        p = jnp.exp(s - m_new)
        l = alpha * l + jnp.sum(p, axis=-1, keepdims=True)
        acc = alpha * acc + _dot(p.astype(_BF16), vblk)
        return m_new, l, acc

    def block(first, carry, masked, width):
        start = pl.multiple_of(first * tile, tile)
        ktblk = kt_ref[:, pl.ds(start, width)]
        vblk = v_ref[pl.ds(start, width), :]
        out = []
        for c in range(2):
            s = _dot(qc[c], ktblk)
            if masked:
                r = lax.broadcasted_iota(_I32, (tile, width), 0)
                col = lax.broadcasted_iota(_I32, (tile, width), 1)
                s = jnp.where(col <= r + (width - tile), s, NEG_INF)
            out.append(update(s, carry[c], vblk))
        return tuple(out)

    init_one = (jnp.full((tile, 1), NEG_INF, _F32), jnp.zeros((tile, 1), _F32),
                jnp.zeros((tile, V_HEAD_DIM), _F32))
    n_full = qi // wide
    carry = lax.fori_loop(0, n_full, lambda j, c: block(j * wide, c, False, wide * tile), (init_one, init_one))
    lam = _lambda(lam_ref, lam_init)
    for last in range(1, wide + 1):
        @pl.when(qi + 1 - n_full * wide == last)
        def _(last=last):
            side_finish = side()
            (_, l0, a0), (_, l1, a1) = block(n_full * wide, carry, True, last * tile)
            o = a0 / l0 - lam * (a1 / l1)
            o_ref[...] = _sub_norm(o, sg_ref[...], lam_init).astype(o_ref.dtype)
            side_finish()


def _attn_kernel(pt_ref, lam_ref, sg_ref, q_ref, kt_ref, v_ref, qs_ref, kn_ref, vn_ref, spread_ref, ck_hbm, cv_hbm,
                 o_ref, os_ref, kbuf, vbuf, sem, qbd_ref, m_ref, l_ref, acc_ref, *, tile, wide, lam_init, n_tok,
                 n_steps, pages_per_step, steps_per_elem):
    npg = pages_per_step
    step = (pl.program_id(0) * pl.num_programs(1) + pl.program_id(1)) * pl.num_programs(2) + pl.program_id(2)
    assert steps_per_elem & (steps_per_elem - 1) == 0
    part = step & (steps_per_elem - 1)

    def page_copies(s):
        elem = lax.shift_right_logical(s, steps_per_elem.bit_length() - 1)
        first_page = (s & (steps_per_elem - 1)) * npg
        base = (s & 1) * npg
        out = []
        for p in range(npg):
            page = pt_ref[elem, first_page + p]
            out.append(pltpu.make_async_copy(ck_hbm.at[page], kbuf.at[base + p], sem.at[0, base + p]))
            out.append(pltpu.make_async_copy(cv_hbm.at[page], vbuf.at[base + p], sem.at[1, base + p]))
        return out

    @pl.when(step == 0)
    def _():
        for cp in page_copies(step):
            cp.start()

    @pl.when(step + 1 < n_steps)
    def _():
        for cp in page_copies(step + 1):
            cp.start()

    def sample_pages():
        return _sample_pages_step(page_copies(step), (step & 1) * npg, part, lam_ref, sg_ref, qs_ref, kn_ref,
                                  vn_ref, spread_ref, os_ref, kbuf, vbuf, qbd_ref, m_ref, l_ref, acc_ref, npg=npg,
                                  n_tok=n_tok,
                                  lam_init=lam_init, last_part=steps_per_elem - 1)

    _prompt_attention(lam_ref, sg_ref, q_ref, kt_ref, v_ref, o_ref, tile=tile, wide=wide, lam_init=lam_init,
                      side=sample_pages)


def _sample_pages_step(copies, base, part, lam_ref, sg_ref, q_ref, kn_ref, vn_ref, spread_ref, o_ref, kbuf, vbuf,
                       qbd_ref, m_ref, l_ref, acc_ref, *, npg, n_tok, lam_init, last_part):
    for cp in copies:
        cp.wait()
    k_refs = [kbuf.at[base + p] for p in range(npg)]
    v_refs = [vbuf.at[base + p] for p in range(npg)]
    rows_per_head = 2 * n_tok
    head_rows = lambda h: slice(h * rows_per_head, (h + 1) * rows_per_head)

    @pl.when(part == 0)
    def _():
        q_tok = q_ref[...]
        lane_hc = lax.broadcasted_iota(_I32, (1, ATTN_WIDTH), 1) >> (HEAD_DIM.bit_length() - 1)
        for hc in range(2 * N_HEADS):
            qbd_ref[hc * n_tok:(hc + 1) * n_tok, :] = jnp.where(lane_hc == hc, q_tok, 0.0)
        q = qbd_ref[...]
        kn = kn_ref[...]
        vn = vn_ref[...]
        tok = lax.broadcasted_iota(_I32, (QROWS, 1), 0) & (n_tok - 1)
        ss = [jnp.where(tok >= jn, jnp.sum(q * kn[jn:jn + 1, :], axis=-1, keepdims=True), NEG_INF)
              for jn in range(n_tok)]
        m = functools.reduce(jnp.maximum, ss)
        ps = [jnp.exp(s - m) for s in ss]
        m_ref[...] = m
        l_ref[...] = functools.reduce(jnp.add, ps)
        for h in range(N_HEADS):
            lanes = slice(h * V_HEAD_DIM, (h + 1) * V_HEAD_DIM)
            acc_ref[head_rows(h), :] = functools.reduce(
                jnp.add, [ps[jn][head_rows(h)] * vn[jn:jn + 1, lanes] for jn in range(n_tok)])

    assert npg % 2 == 0
    q = qbd_ref[...]
    ss = []
    for a in range(0, npg, 2):
        pair = _dot(q, jnp.concatenate([k_refs[a][...], k_refs[a + 1][...]], axis=1))
        ss += [pair[:, :PAGE_SIZE], pair[:, PAGE_SIZE:]]
    m_old = m_ref[...]
    m_new = m_old
    for s in ss:
        m_new = jnp.maximum(m_new, jnp.max(s, axis=-1, keepdims=True))
    alpha = jnp.exp(m_old - m_new)
    ps = [jnp.exp(s - m_new) for s in ss]
    l = alpha * l_ref[...] + functools.reduce(jnp.add, [jnp.sum(p, axis=-1, keepdims=True) for p in ps])
    m_ref[...] = m_new
    l_ref[...] = l
    spread = _dot(jnp.concatenate(ps, axis=0).astype(_BF16), spread_ref[...])
    row_head = lax.broadcasted_iota(_I32, (QROWS, 1), 0) >> (rows_per_head.bit_length() - 1)
    own_head = (lax.broadcasted_iota(_I32, (1, PAGE_SIZE * N_HEADS), 1) & (N_HEADS - 1)) == row_head
    pv = functools.reduce(jnp.add, [
        _dot(jnp.where(own_head, spread[p * QROWS:(p + 1) * QROWS], 0.0).astype(_BF16), vr[...].astype(_BF16))
        for p, vr in enumerate(v_refs)])
    acc_ref[...] = alpha * acc_ref[...] + pv

    def finish():
        @pl.when(part == last_part)
        def _():
            o = acc_ref[...] / l_ref[...]
            lam = _lambda(lam_ref, lam_init)
            d = o - lam * pltpu.roll(o, QROWS - n_tok, 0)
            o_ref[...] = _sub_norm(d, sg_ref[...], lam_init)

    return finish


def _attn(page_table, lam_p, sg, q, kt, v, qrep, kn, vn, cache_k, cache_v, n_tok, lam_init):
    b, s, _ = q.shape
    tile = ATTN_TILE
    nq = s // tile
    n_steps = b * N_HEADS * nq
    nb, n_pages = page_table.shape
    assert (nb * n_pages) % n_steps == 0 and n_steps % nb == 0
    npg = nb * n_pages // n_steps
    steps_per_elem = n_steps // nb
    width = ATTN_WIDTH
    assert QROWS == 2 * N_HEADS * n_tok and cache_k.shape[1:] == (width, PAGE_SIZE)
    assert cache_v.shape[1:] == (PAGE_SIZE * N_HEADS, V_HEAD_DIM)

    spread = (jnp.arange(PAGE_SIZE)[:, None] == jnp.arange(PAGE_SIZE * N_HEADS)[None, :] // N_HEADS).astype(_BF16)
    elem = lambda bi, h, i: ((bi * N_HEADS + h) * nq + i) // steps_per_elem
    per_elem = lambda rows, cols: pl.BlockSpec((None, rows, cols), lambda bi, h, i, pt: (elem(bi, h, i), 0, 0))
    kern = functools.partial(_attn_kernel, tile=tile, wide=ATTN_WIDE, lam_init=lam_init, n_tok=n_tok,
                             n_steps=n_steps, pages_per_step=npg, steps_per_elem=steps_per_elem)
    grid_spec = pltpu.PrefetchScalarGridSpec(
        num_scalar_prefetch=1,
        grid=(b, N_HEADS, nq),
        in_specs=[
            pl.BlockSpec((4, HEAD_DIM), lambda bi, h, i, pt: (0, 0)),
            pl.BlockSpec((1, V_HEAD_DIM), lambda bi, h, i, pt: (0, 0)),
            pl.BlockSpec((None, tile, LANES), lambda bi, h, i, pt: (bi, i, h)),
            pl.BlockSpec((None, LANES, s), lambda bi, h, i, pt: (bi, h, 0)),
            pl.BlockSpec((None, s, LANES), lambda bi, h, i, pt: (bi, 0, h)),
            per_elem(n_tok, width), per_elem(kn.shape[1], width), per_elem(vn.shape[1], width),
            _whole_vmem(),
            pl.BlockSpec(memory_space=pl.ANY), pl.BlockSpec(memory_space=pl.ANY),
        ],
        out_specs=[pl.BlockSpec((None, tile, LANES), lambda bi, h, i, pt: (bi, i, h)),
                   per_elem(QROWS, V_HEAD_DIM)],
        scratch_shapes=[pltpu.VMEM((2 * npg, width, PAGE_SIZE), _F32),
                        pltpu.VMEM((2 * npg, PAGE_SIZE * N_HEADS, V_HEAD_DIM), _F32),
                        pltpu.SemaphoreType.DMA((2, 2 * npg)),
                        pltpu.VMEM((QROWS, width), _F32),
                        pltpu.VMEM((QROWS, 1), _F32), pltpu.VMEM((QROWS, 1), _F32),
                        pltpu.VMEM((QROWS, V_HEAD_DIM), _F32)],
    )
    return pl.pallas_call(
        kern,
        grid_spec=grid_spec,
        out_shape=[jax.ShapeDtypeStruct((b, s, ATTN_WIDTH), _BF16),
                   jax.ShapeDtypeStruct((nb, QROWS, V_HEAD_DIM), _F32)],
        compiler_params=_params(("arbitrary", "arbitrary", "arbitrary")),
        name="attn",
    )(page_table, lam_p, sg, q, kt, v, qrep, kn, vn, spread, cache_k, cache_v)


def _poolwin_kernel(u_ref, halo_ref, hist_ref, d_ref, *, tm, tiles_per_seq, start_pos):
    i = pl.program_id(0)
    t_in_seq = i % tiles_per_seq
    u = u_ref[...]
    halo = jnp.where(t_in_seq == 0, hist_ref[...], halo_ref[...])
    ext = jnp.concatenate([halo, u], axis=0)
    pos = start_pos + t_in_seq * tm + lax.broadcasted_iota(_I32, (tm, 1), 0)
    for g, w in enumerate(POOL_WINDOWS):
        sl = slice(g * POOL_GROUP_WIDTH, (g + 1) * POOL_GROUP_WIDTH)
        acc = ext[:, sl]
        span = 1
        while span < w:
            acc = acc + pltpu.roll(acc, span, 0)
            span *= 2
        cnt = jnp.minimum(pos + 1, w).astype(_F32)
        d_ref[:, sl] = (acc[HALO_ROWS:] / cnt - u[:, sl]).astype(d_ref.dtype)


def _poolwin(u, hist, tm, tiles_per_seq, start_pos):
    t = u.shape[0]
    halo_blocks = max(tm // HALO_ROWS, 1)
    kern = functools.partial(_poolwin_kernel, tm=tm, tiles_per_seq=tiles_per_seq, start_pos=start_pos)
    return pl.pallas_call(
        kern,
        grid=(t // tm,),
        in_specs=[
            pl.BlockSpec((tm, POOL_WIDTH), lambda i: (i, 0)),
            pl.BlockSpec((HALO_ROWS, POOL_WIDTH),
                         lambda i: (jnp.where(i % tiles_per_seq == 0, 0, i * halo_blocks - 1), 0)),
            pl.BlockSpec((None, HALO_ROWS, POOL_WIDTH), lambda i: (i // tiles_per_seq, 0, 0)),
        ],
        out_specs=pl.BlockSpec((tm, POOL_WIDTH), lambda i: (i, 0)),
        out_shape=jax.ShapeDtypeStruct((t, POOL_WIDTH), _BF16),
        compiler_params=_params(("arbitrary",)),
        name="poolwin",
    )(u, u, hist)


def _mix_kernel(x_ref, a_ref, d_ref, wp_ref, ps_ref, wo_ref, g2_ref, wrh_ref, wrl_ref, br_ref,
                cin_ref, x1_ref, h2_ref, route_ref, rect_ref, cout_ref, carry_ref, *, tm):
    i = pl.program_id(0)

    @pl.when(i == 0)
    def _():
        carry_ref[...] = cin_ref[...]

    d = d_ref[...]
    gw = POOL_GROUP_WIDTH
    py = jnp.concatenate([_dot(d[:, g * gw:(g + 1) * gw], wp_ref[g]) for g in range(len(POOL_WINDOWS))],
                         axis=1) * ps_ref[...]
    x1 = (x_ref[...] + _dot(a_ref[...], wo_ref[0:ATTN_WIDTH, :])
          + _dot(py.astype(_BF16), wo_ref[ATTN_WIDTH:, :]))
    x1_ref[...] = x1
    ms = jnp.mean(x1 * x1, axis=-1, keepdims=True)
    h2 = x1 * lax.rsqrt(ms + RMS_EPS) * g2_ref[...]
    _store_token_major(h2_ref, h2)

    hh, hl = _split_bf16(h2)
    both = _dot(hh, wrl_ref[...])
    logits = both[:, :ROUTE_LANES] + both[:, ROUTE_LANES:] + _dot(hl, wrh_ref[...]) + br_ref[...]
    lane = lax.broadcasted_iota(_I32, (1, ROUTE_LANES), 1)
    big = jnp.int32(ROUTE_LANES)
    is_coarse = (lane >= N_EXPERTS) & (lane < N_EXPERTS + N_EXPERT_GROUPS)
    lc = jnp.where(is_coarse, logits, NEG_INF)
    mc = jnp.max(lc, axis=-1, keepdims=True)
    g_idx = jnp.min(jnp.where(lc == mc, lane, big), axis=-1, keepdims=True) - N_EXPERTS
    p_g = 1.0 / jnp.sum(jnp.exp(lc - mc), axis=-1, keepdims=True)
    in_group = (lane < N_EXPERTS) & ((lane >> (EXPERTS_PER_GROUP.bit_length() - 1)) == g_idx)
    lf = jnp.where(in_group, logits, NEG_INF)
    v1 = jnp.max(lf, axis=-1, keepdims=True)
    i1 = jnp.min(jnp.where(lf == v1, lane, big), axis=-1, keepdims=True)
    lf2 = jnp.where(lane == i1, NEG_INF, lf)
    v2 = jnp.max(lf2, axis=-1, keepdims=True)
    i2 = jnp.min(jnp.where(lf2 == v2, lane, big), axis=-1, keepdims=True)
    e21 = jnp.exp(v2 - v1)
    w1 = p_g / (1.0 + e21)
    w2 = p_g * e21 / (1.0 + e21)

    oh1 = lane == i1
    oh2 = lane == i2
    onehot = (oh1 | oh2).astype(_F32)
    r = lax.broadcasted_iota(_I32, (tm, tm), 0)
    col = lax.broadcasted_iota(_I32, (tm, tm), 1)
    lower = (col < r).astype(_BF16)
    before = _dot(lower, onehot.astype(_BF16)) + carry_ref[...]
    rank1 = jnp.sum(jnp.where(oh1, before, 0.0), axis=-1, keepdims=True)
    rank2 = jnp.sum(jnp.where(oh2, before, 0.0), axis=-1, keepdims=True)
    carry = carry_ref[...] + jnp.sum(onehot, axis=0, keepdims=True)
    carry_ref[...] = carry
    cout_ref[...] = carry

    rec = jnp.zeros((tm, ROUTE_LANES), _F32)
    for k, val in enumerate((i1.astype(_F32), i2.astype(_F32), w1, w2, rank1, rank2)):
        rec = jnp.where(lane == k, val, rec)
    route_ref[...] = rec
    rect_ref[...] = rec.T[0:REC_ROWS, :].astype(_I32)


def _mix(x, a, d, wp_bf, ps, wo_bf, g2, wrh, wrl, br, counts_in, tm):
    t = x.shape[0]
    row = lambda i: (i, 0)
    const = lambda i: (0, 0)
    kern = functools.partial(_mix_kernel, tm=tm)
    return pl.pallas_call(
        kern,
        grid=(t // tm,),
        in_specs=[
            pl.BlockSpec((tm, D_MODEL), row),
            pl.BlockSpec((tm, ATTN_WIDTH), row),
            pl.BlockSpec((tm, POOL_WIDTH), row),
            _whole_vmem(),
            pl.BlockSpec((1, POOL_WIDTH), const),
            _whole_vmem(),
            pl.BlockSpec((1, D_MODEL), const),
            _whole_vmem(),
            _whole_vmem(),
            pl.BlockSpec((1, ROUTE_LANES), const),
            pl.BlockSpec((1, ROUTE_LANES), const),
        ],
        out_specs=[
            pl.BlockSpec((tm, D_MODEL), row),
            pl.BlockSpec((tm * ROW_CHUNKS, LANES), row),
            pl.BlockSpec((tm, ROUTE_LANES), row),
            pl.BlockSpec((None, REC_ROWS, tm), lambda i: (i, 0, 0)),
            pl.BlockSpec((1, ROUTE_LANES), const),
        ],
        out_shape=[
            jax.ShapeDtypeStruct((t, D_MODEL), _F32),
            jax.ShapeDtypeStruct((t * ROW_CHUNKS, LANES), _F32),
            jax.ShapeDtypeStruct((t, ROUTE_LANES), _F32),
            jax.ShapeDtypeStruct((t // tm, REC_ROWS, tm), _I32),
            jax.ShapeDtypeStruct((1, ROUTE_LANES), _F32),
        ],
        scratch_shapes=[pltpu.VMEM((1, ROUTE_LANES), _F32)],
        compiler_params=_params(("arbitrary",)),
        name="mix",
    )(x, a, d, wp_bf, ps, wo_bf, g2, wrh, wrl, br, counts_in)


def _tokens(ref, first, n):
    return ref.at[pl.ds(pl.multiple_of(first * ROW_CHUNKS, ROW_CHUNKS), n * ROW_CHUNKS), :]


def _row_copy(src, dst, src_tok, dst_tok, sem):
    return pltpu.make_async_copy(_tokens(src, src_tok, 1), _tokens(dst, dst_tok, 1), sem)


def _dispatch_kernel(off_ref, zstart_ref, zpad_ref, rec_ref, h_ref, *rest, tm, zero_rows):
    if zero_rows:
        xs_ref, sem, zsem, zbuf = rest
    else:
        _, xs_ref, sem = rest
    i = pl.program_id(0)

    if zero_rows:
        def zero_fill(wait):
            def go(cp):
                cp.wait() if wait else cp.start()

            def body(e, c):
                cursor = zstart_ref[e]
                pad = zpad_ref[e]
                run = zero_rows // 2
                while run >= 1:
                    @pl.when((pad & run) != 0)
                    def _(cursor=cursor, run=run):
                        go(pltpu.make_async_copy(_tokens(zbuf, 0, run), _tokens(xs_ref, cursor, run), zsem))

                    cursor = cursor + (pad & run)
                    run //= 2
                return c

            lax.fori_loop(0, N_EXPERTS, body, 0)

            def tile(t, c):
                rows = zero_rows // 2
                go(pltpu.make_async_copy(_tokens(zbuf, 0, rows),
                                         _tokens(xs_ref, zstart_ref[N_EXPERTS] + t * rows, rows), zsem))
                return c

            lax.fori_loop(0, zpad_ref[N_EXPERTS], tile, 0)

        @pl.when(i == 0)
        def _():
            zbuf[...] = jnp.zeros_like(zbuf)
            zero_fill(False)
            zero_fill(True)

    def body(r, c):
        for k in range(2):
            dst = off_ref[rec_ref[0, k, r]] + rec_ref[0, 4 + k, r]
            _row_copy(h_ref, xs_ref, r, dst, sem).start()
        return c

    lax.fori_loop(0, tm, body, 0, unroll=DMA_UNROLL)
    for _ in range(2):
        pltpu.make_async_copy(h_ref, _tokens(xs_ref, 0, tm), sem).wait()


def _dispatch(off, zstart, zpad, rect, h2, xs, n_rows, zero_rows):
    n_tiles, _, tm = rect.shape
    kern = functools.partial(_dispatch_kernel, tm=tm, zero_rows=zero_rows)
    in_specs = [
        pl.BlockSpec((1, REC_ROWS, tm), lambda i, o, z, p: (i, 0, 0), memory_space=pltpu.SMEM),
        pl.BlockSpec((tm * ROW_CHUNKS, LANES), lambda i, o, z, p: (i, 0)),
    ]
    scratch = [pltpu.SemaphoreType.DMA(())]
    args = [off, zstart, zpad, rect, h2]
    aliases = {}
    if zero_rows:
        scratch += [pltpu.SemaphoreType.DMA(()), pltpu.VMEM((zero_rows // 2 * ROW_CHUNKS, LANES), _F32)]
    else:
        in_specs.append(pl.BlockSpec(memory_space=pl.ANY))
        args.append(xs)
        aliases = {5: 0}
    grid_spec = pltpu.PrefetchScalarGridSpec(
        num_scalar_prefetch=3, grid=(n_tiles,), in_specs=in_specs,
        out_specs=pl.BlockSpec(memory_space=pl.ANY), scratch_shapes=scratch)
    return pl.pallas_call(
        kern,
        grid_spec=grid_spec,
        out_shape=jax.ShapeDtypeStruct((n_rows * ROW_CHUNKS, LANES), _F32),
        input_output_aliases=aliases,
        compiler_params=_params(("arbitrary",)),
        name="dispatch",
    )(*args)


def _experts_kernel(texp_ref, nused_ref, x_ref, wg_ref, wu_ref, wd_ref, y_ref, wgb, wub, wdb, *, tm):
    i = pl.program_id(0)
    new_expert = (i == 0) | (texp_ref[i] != texp_ref[jnp.maximum(i - 1, 0)])

    @pl.when(new_expert)
    def _():
        wgb[...] = wg_ref[...].astype(_BF16)
        wub[...] = wu_ref[...].astype(_BF16)
        wdb[...] = wd_ref[...].astype(_BF16)

    @pl.when(i < nused_ref[0])
    def _():
        x = _load_token_major(x_ref, tm).astype(_BF16)
        hg = _dot(x, wgb[...])
        hu = _dot(x, wub[...])
        act = (hg * jax.nn.sigmoid(hg) * hu).astype(_BF16)
        y_ref[...] = _dot(act, wdb[...])

    @pl.when(i >= nused_ref[0])
    def _():
        y_ref[...] = jnp.zeros_like(y_ref)


def _experts(tile_expert, n_used, xs, w_gate, w_up, w_down, tm):
    n_tiles = tile_expert.shape[0]
    w_in_spec = pl.BlockSpec((None, D_MODEL, D_EXPERT), lambda i, te, nu: (te[i], 0, 0))
    grid_spec = pltpu.PrefetchScalarGridSpec(
        num_scalar_prefetch=2,
        grid=(n_tiles,),
        in_specs=[
            pl.BlockSpec((tm * ROW_CHUNKS, LANES), lambda i, te, nu: (jnp.minimum(i, nu[0] - 1), 0)),
            w_in_spec, w_in_spec,
            pl.BlockSpec((None, D_EXPERT, D_MODEL), lambda i, te, nu: (te[i], 0, 0)),
        ],
        out_specs=pl.BlockSpec((tm, D_MODEL), lambda i, te, nu: (i, 0)),
        scratch_shapes=[pltpu.VMEM((D_MODEL, D_EXPERT), _BF16), pltpu.VMEM((D_MODEL, D_EXPERT), _BF16),
                        pltpu.VMEM((D_EXPERT, D_MODEL), _BF16)],
    )
    return pl.pallas_call(
        functools.partial(_experts_kernel, tm=tm),
        grid_spec=grid_spec,
        out_shape=jax.ShapeDtypeStruct((n_tiles * tm, D_MODEL), _F32),
        compiler_params=_params(("arbitrary",)),
        name="experts",
    )(tile_expert, n_used, xs, w_gate, w_up, w_down)


def _combine_kernel(off_ref, rec_ref, recn_ref, x1_ref, route_ref, y_hbm, o_ref, r0, r1, sem, *, tm):
    i = pl.program_id(0)
    n = pl.num_programs(0)
    bufs = (r0, r1)

    def issue(rec, slot):
        def body(r, c):
            for k in range(2):
                src = off_ref[rec[0, k, r]] + rec[0, 4 + k, r]
                pltpu.make_async_copy(y_hbm.at[pl.ds(src, 1), :], bufs[k].at[slot, pl.ds(r, 1), :],
                                      sem.at[slot, k]).start()
            return c

        lax.fori_loop(0, tm, body, 0, unroll=DMA_UNROLL)

    @pl.when(i == 0)
    def _():
        issue(rec_ref, 0)

    @pl.when(i + 1 < n)
    def _():
        issue(recn_ref, (i + 1) % 2)

    slot = i % 2
    for k in range(2):
        pltpu.make_async_copy(y_hbm.at[pl.ds(0, tm), :], bufs[k].at[slot], sem.at[slot, k]).wait()
    rec = route_ref[...]
    o_ref[...] = x1_ref[...] + rec[:, 2:3] * r0[slot] + rec[:, 3:4] * r1[slot]


def _combine(off, rect, x1, route, y_sorted):
    n_tiles, _, tm = rect.shape
    t = x1.shape[0]
    kern = functools.partial(_combine_kernel, tm=tm)
    rec_spec = lambda nxt: pl.BlockSpec(
        (1, REC_ROWS, tm), lambda i, o: (jnp.minimum(i + nxt, n_tiles - 1), 0, 0), memory_space=pltpu.SMEM)
    grid_spec = pltpu.PrefetchScalarGridSpec(
        num_scalar_prefetch=1,
        grid=(n_tiles,),
        in_specs=[
            rec_spec(0), rec_spec(1),
            pl.BlockSpec((tm, D_MODEL), lambda i, o: (i, 0)),
            pl.BlockSpec((tm, ROUTE_LANES), lambda i, o: (i, 0)),
            pl.BlockSpec(memory_space=pl.ANY),
        ],
        out_specs=pl.BlockSpec((tm, D_MODEL), lambda i, o: (i, 0)),
        scratch_shapes=[pltpu.VMEM((2, tm, D_MODEL), _F32), pltpu.VMEM((2, tm, D_MODEL), _F32),
                        pltpu.SemaphoreType.DMA((2, 2))],
    )
    return pl.pallas_call(
        kern,
        grid_spec=grid_spec,
        out_shape=jax.ShapeDtypeStruct((t, D_MODEL), _F32),
        compiler_params=_params(("arbitrary",)),
        name="combine",
    )(off, rect, rect, x1, route, y_sorted)


def _rope_tables(pos):
    inv = ROPE_THETA ** (-jnp.arange(0, HEAD_DIM, 2, dtype=_F32) / HEAD_DIM)
    ang = pos.astype(_F32)[:, None] * inv[None, :]
    cos = jnp.cos(ang)
    sin = jnp.sin(ang)
    return jnp.tile(cos, (1, 4)), jnp.tile(jnp.concatenate([-sin, sin], axis=1), (1, 2))


def _head_sum_matrices():
    lane_head = jnp.arange(ATTN_WIDTH) // HEAD_DIM
    cols = jnp.arange(LANES)
    esum = (lane_head[:, None] == cols[None, :]).astype(_F32) / HEAD_DIM
    eexp = (cols[:, None] == lane_head[None, :]).astype(_F32)
    return esum.astype(_BF16), jnp.concatenate([eexp, eexp], axis=0).astype(_BF16)


def kernel(x_prompt, x_sample, cache_k, cache_v, state_pool, page_table, norm1_g, w_in, q_norm_g, k_norm_g,
           lambda_q1, lambda_k1, lambda_q2, lambda_k2, subln_g, w_pool, pool_scale, w_out, norm2_g, w_coarse,
           b_coarse, w_fine, b_fine, w_gate, w_up, w_down):
    depth = w_in.shape[0]
    assert depth == 1
    l = 0
    lam_init = 0.8 - 0.6 * math.exp(-0.3 * l)
    bp, sp, _ = x_prompt.shape
    bs, ts, _ = x_sample.shape
    tp, tsamp = bp * sp, bs * ts
    n_pages = page_table.shape[1]
    past = n_pages * PAGE_SIZE
    n_hc = 2 * N_HEADS

    w_in_bf = w_in[l].astype(_BF16)
    w_out_bf = w_out[l].astype(_BF16)
    w_pool_bf = w_pool[l].astype(_BF16)
    g1 = norm1_g[l][None, :]
    g2 = norm2_g[l][None, :]
    gq = jnp.tile(q_norm_g[l], n_hc)[None, :]
    gk = jnp.tile(k_norm_g[l], n_hc)[None, :]
    sg = subln_g[l][None, :]
    ps = pool_scale[l][None, :]
    lam_p = jnp.stack([lambda_q1[l], lambda_k1[l], lambda_q2[l], lambda_k2[l]])
    esum, eexp = _head_sum_matrices()
    pad = ROUTE_LANES - N_EXPERTS - N_EXPERT_GROUPS
    w_fine_flat = jnp.transpose(w_fine[l], (1, 0, 2)).reshape(D_MODEL, N_EXPERTS)
    w_route = jnp.concatenate([w_fine_flat, w_coarse[l], jnp.zeros((D_MODEL, pad), _F32)], axis=1)
    wrh, wrl = _split_bf16(w_route)
    wrl = jnp.concatenate([wrh, wrl], axis=1)
    b_route = jnp.concatenate([b_fine[l].reshape(-1), b_coarse[l], jnp.zeros((pad,), _F32)])[None, :]

    cos_p, sin_p = _rope_tables(jnp.arange(sp, dtype=jnp.int32))
    cos_s, sin_s = _rope_tables(past + jnp.arange(ts, dtype=jnp.int32))
    cos_s, sin_s = jnp.tile(cos_s, (bs, 1)), jnp.tile(sin_s, (bs, 1))
    xp2 = x_prompt.reshape(tp, D_MODEL)
    xs2 = x_sample.reshape(tsamp, D_MODEL)
    qp, kpt, kpt_bf, vp, vp_bf, up = _proj(xp2, g1, w_in_bf, gq, gk, cos_p, sin_p, esum, eexp, PROJ_TILE, sp)
    qs, ks, _, vs, _, us = _proj(xs2, g1, w_in_bf, gq, gk, cos_s, sin_s, esum, eexp, tsamp)

    qrep = qs.astype(_F32).reshape(bs, ts, ATTN_WIDTH)
    new_rows = 8
    kn = jnp.pad(ks.reshape(bs, ts, ATTN_WIDTH), ((0, 0), (0, new_rows - ts), (0, 0)))
    vn = jnp.pad(vs.reshape(bs, ts, ATTN_WIDTH), ((0, 0), (0, new_rows - ts), (0, 0)))
    n_phys = cache_k.shape[1]
    ck = jnp.transpose(cache_k[l], (0, 2, 3, 1)).reshape(n_phys, ATTN_WIDTH, PAGE_SIZE)
    cv = cache_v[l].reshape(n_phys, PAGE_SIZE * N_HEADS, V_HEAD_DIM)
    shp = (bp, sp, ATTN_WIDTH)
    a_p, a_s = _attn(page_table, lam_p, sg, qp.reshape(shp), kpt_bf, vp_bf.reshape(shp), qrep, kn, vn, ck, cv,
                     ts, lam_init)
    a_p = a_p.reshape(tp, ATTN_WIDTH)
    a_s = a_s.reshape(bs, N_HEADS, 2, ts, V_HEAD_DIM)[:, :, 0]
    a_s = a_s.transpose(0, 2, 1, 3).reshape(tsamp, ATTN_WIDTH).astype(_BF16)

    hist_p = jnp.zeros((bp, HALO_ROWS, POOL_WIDTH), _F32)
    d_p = _poolwin(up, hist_p, MIX_TILE, sp // MIX_TILE, 0)
    seq_rows = 8
    us_pad = jnp.pad(us.reshape(bs, ts, POOL_WIDTH), ((0, 0), (0, seq_rows - ts), (0, 0)))
    hist_s = jnp.pad(state_pool[l], ((0, 0), (HALO_ROWS - POOL_HIST, 0), (0, 0)))
    d_s = _poolwin(us_pad.reshape(bs * seq_rows, POOL_WIDTH), hist_s, seq_rows, 1, past)
    d_s = d_s.reshape(bs, seq_rows, POOL_WIDTH)[:, :ts].reshape(tsamp, POOL_WIDTH)

    counts0 = jnp.zeros((1, ROUTE_LANES), _F32)
    x1_p, h2_p, route_p, rect_p, counts_p = _mix(xp2, a_p, d_p, w_pool_bf, ps, w_out_bf, g2, wrh, wrl, b_route,
                                                 counts0, MIX_TILE)
    x1_s, h2_s, route_s, rect_s, counts = _mix(xs2, a_s, d_s, w_pool_bf, ps, w_out_bf, g2, wrh, wrl, b_route,
                                               counts_p, tsamp)

    tm = EXPERT_TILE
    t_all = tp + tsamp
    n_tiles = (2 * t_all + N_EXPERTS * (tm - 1) + tm - 1) // tm
    cnt = counts[0, :N_EXPERTS].astype(jnp.int32)
    padded = (cnt + tm - 1) // tm * tm
    ends = jnp.cumsum(padded)
    off = ends - padded
    n_used = (ends[-1] // tm).astype(jnp.int32)
    tile_ids = jnp.arange(n_tiles, dtype=jnp.int32)
    te = jnp.sum((ends[None, :] <= (tile_ids * tm)[:, None]).astype(jnp.int32), axis=1)
    te = jnp.minimum(te, N_EXPERTS - 1)
    last = jnp.sum(jnp.where(tile_ids == n_used - 1, te, 0))
    tile_expert = jnp.where(tile_ids < n_used, te, last)

    n_rows = n_tiles * tm
    zstart = off + counts_p[0, :N_EXPERTS].astype(jnp.int32)
    zinfo = (jnp.concatenate([zstart, ends[-1:]]), jnp.concatenate([ends - zstart, (n_tiles - n_used)[None]]))
    assert tsamp <= tm
    xs = _dispatch(off, *zinfo, rect_p, h2_p, None, n_rows, 2 * tm)
    xs = _dispatch(off, *zinfo, rect_s, h2_s, xs, n_rows, 0)
    y_sorted = _experts(tile_expert, n_used.reshape(1), xs, w_gate[l], w_up[l], w_down[l], tm)
    y_p = _combine(off, rect_p, x1_p, route_p, y_sorted)
    y_s = _combine(off, rect_s, x1_s, route_s, y_sorted)

    new_k_p = jnp.transpose(kpt.reshape(bp, n_hc, HEAD_DIM, sp), (0, 3, 1, 2))
    new_pool_p = up.reshape(bp, sp, POOL_WIDTH)[:, sp - POOL_HIST:]
    new_pool_s = jnp.concatenate([state_pool[l], us.reshape(bs, ts, POOL_WIDTH)], axis=1)[:, -POOL_HIST:]
    return (y_p.reshape(bp, sp, D_MODEL),
            y_s.reshape(bs, ts, D_MODEL),
            new_k_p[None],
            vp.reshape(1, bp, sp, N_HEADS, V_HEAD_DIM),
            new_pool_p[None],
            ks.reshape(1, bs, ts, n_hc, HEAD_DIM),
            vs.reshape(1, bs, ts, N_HEADS, V_HEAD_DIM),
            new_pool_s[None])
```

```python
import functools
import math

import jax
import jax.numpy as jnp
from jax import lax
from jax.experimental import pallas as pl
from jax.experimental.pallas import tpu as pltpu

_F32 = jnp.float32
_BF16 = jnp.bfloat16
_I32 = jnp.int32

D_MODEL = 2048
ATTN_WIDTH = 1024
POOL_WIDTH = 1024
HEAD_DIM = 64
N_HEADS = 8
V_HEAD_DIM = 128
POOL_WINDOWS = (2, 4, 8, 16)
POOL_GROUP_WIDTH = 256
POOL_HIST = 15
HALO_ROWS = 16
N_EXPERTS = 32
EXPERTS_PER_GROUP = 8
N_EXPERT_GROUPS = 4
D_EXPERT = 256
ROPE_THETA = 10000.0
RMS_EPS = 1e-6
NEG_INF = -1e30
PAGE_SIZE = 128
LANES = 128
ROUTE_LANES = 128
REC_ROWS = 8
VMEM_LIMIT = 56 * 1024 * 1024

PROJ_TILE = 512
ATTN_TILE = 512
ATTN_WIDE = 4
MIX_TILE = 512
EXPERT_TILE = 256
QROWS = 64
DMA_UNROLL = 8


def _dot(a, b):
    return jnp.dot(a, b, preferred_element_type=_F32)


def _split_bf16(x):
    hi = x.astype(_BF16)
    lo = (x - hi.astype(_F32)).astype(_BF16)
    return hi, lo


def _params(sem):
    return pltpu.CompilerParams(dimension_semantics=sem, vmem_limit_bytes=VMEM_LIMIT)


ROW_CHUNKS = D_MODEL // LANES


def _load_token_major(ref, n):
    return jnp.concatenate([ref[pl.ds(c, n, stride=ROW_CHUNKS), :] for c in range(ROW_CHUNKS)], axis=1)


def _store_token_major(ref, x):
    n = x.shape[0]
    for c in range(ROW_CHUNKS):
        ref[pl.ds(c, n, stride=ROW_CHUNKS), :] = x[:, c * LANES:(c + 1) * LANES]


def _whole_vmem():
    return pl.BlockSpec(memory_space=pltpu.VMEM)


def _proj_kernel(x_ref, g1_ref, w_ref, gq_ref, gk_ref, cos_ref, sin_ref, esum_ref, eexp_ref,
                 q_ref, kf_ref, kb_ref, vf_ref, vb_ref, u_ref, *, k_transposed):
    x = x_ref[...]
    ms = jnp.mean(x * x, axis=-1, keepdims=True)
    h = (x * lax.rsqrt(ms + RMS_EPS) * g1_ref[...]).astype(_BF16)
    cos = cos_ref[...]
    sin = sin_ref[...]
    lane = lax.broadcasted_iota(_I32, (1, LANES), 1)
    upper = (lane & (HEAD_DIM - 1)) >= HEAD_DIM // 2

    def normed_rope(z, g_ref, outs, transposed):
        msq = _dot((z * z).astype(_BF16), esum_ref[...])
        rh, rl = _split_bf16(lax.rsqrt(msq + RMS_EPS))
        rb = _dot(jnp.concatenate([rh, rl], axis=1), eexp_ref[...])
        n = z * rb * g_ref[...]
        for j in range(ATTN_WIDTH // LANES):
            sl = slice(j * LANES, (j + 1) * LANES)
            nj = n[:, sl]
            swapped = jnp.where(upper, pltpu.roll(nj, HEAD_DIM // 2, 1),
                                pltpu.roll(nj, LANES - HEAD_DIM // 2, 1))
            o = nj * cos + swapped * sin
            if transposed:
                ot = o.T
                for ref, _ in outs:
                    ref[sl, :] = ot.astype(ref.dtype)
            else:
                for ref, scale in outs:
                    ref[:, sl] = (o * scale).astype(ref.dtype)

    a = ATTN_WIDTH
    normed_rope(_dot(h, w_ref[:, 0:a]), gq_ref, ((q_ref, HEAD_DIM ** -0.5),), False)
    normed_rope(_dot(h, w_ref[:, a:2 * a]), gk_ref, ((kf_ref, 1.0), (kb_ref, 1.0)), k_transposed)
    zv = _dot(h, w_ref[:, 2 * a:3 * a])
    vf_ref[...] = zv
    vb_ref[...] = zv.astype(_BF16)
    u_ref[...] = _dot(h, w_ref[:, 3 * a:])


def _proj(x, g1, w_bf, gq, gk, cos, sin, esum, eexp, tm, seq_len=None):
    t = x.shape[0]
    n_pos_tiles = cos.shape[0] // tm
    row = lambda i: (i, 0)
    const = lambda i: (0, 0)
    tok_spec = lambda w: pl.BlockSpec((tm, w), row)
    a = ATTN_WIDTH
    if seq_len is None:
        k_spec, k_shape = tok_spec(a), (t, a)
    else:
        tiles_per_seq = seq_len // tm
        k_spec = pl.BlockSpec((None, a, tm), lambda i: (i // tiles_per_seq, 0, i % tiles_per_seq))
        k_shape = (t // seq_len, a, seq_len)
    return pl.pallas_call(
        functools.partial(_proj_kernel, k_transposed=seq_len is not None),
        grid=(t // tm,),
        in_specs=[
            tok_spec(D_MODEL),
            pl.BlockSpec((1, D_MODEL), const),
            _whole_vmem(),
            pl.BlockSpec((1, a), const),
            pl.BlockSpec((1, a), const),
            pl.BlockSpec((tm, LANES), lambda i: (i % n_pos_tiles, 0)),
            pl.BlockSpec((tm, LANES), lambda i: (i % n_pos_tiles, 0)),
            _whole_vmem(),
            _whole_vmem(),
        ],
        out_specs=[tok_spec(a), k_spec, k_spec, tok_spec(a), tok_spec(a), tok_spec(POOL_WIDTH)],
        out_shape=[
            jax.ShapeDtypeStruct((t, a), _BF16),
            jax.ShapeDtypeStruct(k_shape, _F32),
            jax.ShapeDtypeStruct(k_shape, _BF16),
            jax.ShapeDtypeStruct((t, a), _F32),
            jax.ShapeDtypeStruct((t, a), _BF16),
            jax.ShapeDtypeStruct((t, POOL_WIDTH), _F32),
        ],
        compiler_params=_params(("arbitrary",)),
        name="proj",
    )(x, g1, w_bf, gq, gk, cos, sin, esum, eexp)


def _lambda(lam_ref, lam_init):
    lp = lam_ref[...]
    s1 = jnp.sum(lp[0:1] * lp[1:2], axis=-1, keepdims=True)
    s2 = jnp.sum(lp[2:3] * lp[3:4], axis=-1, keepdims=True)
    return jnp.exp(s1) - jnp.exp(s2) + lam_init


def _sub_norm(o, sg, lam_init):
    ms = jnp.mean(o * o, axis=-1, keepdims=True)
    return o * lax.rsqrt(ms + RMS_EPS) * sg * (1.0 - lam_init)


def _prompt_attention(lam_ref, sg_ref, q_ref, kt_ref, v_ref, o_ref, *, tile, wide, lam_init, side):
    qi = pl.program_id(2)
    q = q_ref[...]
    lane = lax.broadcasted_iota(_I32, (1, LANES), 1)
    zero = jnp.zeros_like(q)
    qc = (jnp.where(lane < HEAD_DIM, q, zero), jnp.where(lane >= HEAD_DIM, q, zero))

    def update(s, state, vblk):
        m, l, acc = state
        m_new = jnp.maximum(m, jnp.max(s, axis=-1, keepdims=True))
        alpha = jnp.exp(m - m_new)
        p = jnp.exp(s - m_new)
        l = alpha * l + jnp.sum(p, axis=-1, keepdims=True)
        acc = alpha * acc + _dot(p.astype(_BF16), vblk)
        return m_new, l, acc

    def block(first, carry, masked, width):
        start = pl.multiple_of(first * tile, tile)
        ktblk = kt_ref[:, pl.ds(start, width)]
        vblk = v_ref[pl.ds(start, width), :]
        out = []
        for c in range(2):
            s = _dot(qc[c], ktblk)
            if masked:
                r = lax.broadcasted_iota(_I32, (tile, width), 0)
                col = lax.broadcasted_iota(_I32, (tile, width), 1)
                s = jnp.where(col <= r + (width - tile), s, NEG_INF)
            out.append(update(s, carry[c], vblk))
        return tuple(out)

    init_one = (jnp.full((tile, 1), NEG_INF, _F32), jnp.zeros((tile, 1), _F32),
                jnp.zeros((tile, V_HEAD_DIM), _F32))
    n_full = qi // wide
    carry = lax.fori_loop(0, n_full, lambda j, c: block(j * wide, c, False, wide * tile), (init_one, init_one))
    lam = _lambda(lam_ref, lam_init)
    for last in range(1, wide + 1):
        @pl.when(qi + 1 - n_full * wide == last)
        def _(last=last):
            side_finish = side()
            (_, l0, a0), (_, l1, a1) = block(n_full * wide, carry, True, last * tile)
            o = a0 / l0 - lam * (a1 / l1)
            o_ref[...] = _sub_norm(o, sg_ref[...], lam_init).astype(o_ref.dtype)
            side_finish()


def _attn_kernel(pt_ref, lam_ref, sg_ref, q_ref, kt_ref, v_ref, qs_ref, kn_ref, vn_ref, spread_ref, ck_hbm, cv_hbm,
                 o_ref, os_ref, kbuf, vbuf, sem, qbd_ref, m_ref, l_ref, acc_ref, *, tile, wide, lam_init, n_tok,
                 n_steps, pages_per_step, steps_per_elem):
    npg = pages_per_step
    step = (pl.program_id(0) * pl.num_programs(1) + pl.program_id(1)) * pl.num_programs(2) + pl.program_id(2)
    assert steps_per_elem & (steps_per_elem - 1) == 0
    part = step & (steps_per_elem - 1)

    def page_copies(s):
        elem = lax.shift_right_logical(s, steps_per_elem.bit_length() - 1)
        first_page = (s & (steps_per_elem - 1)) * npg
        base = (s & 1) * npg
        out = []
        for p in range(npg):
            page = pt_ref[elem, first_page + p]
            out.append(pltpu.make_async_copy(ck_hbm.at[page], kbuf.at[base + p], sem.at[0, base + p]))
            out.append(pltpu.make_async_copy(cv_hbm.at[page], vbuf.at[base + p], sem.at[1, base + p]))
        return out

    @pl.when(step == 0)
    def _():
        for cp in page_copies(step):
            cp.start()

    @pl.when(step + 1 < n_steps)
    def _():
        for cp in page_copies(step + 1):
            cp.start()

    def sample_pages():
        return _sample_pages_step(page_copies(step), (step & 1) * npg, part, lam_ref, sg_ref, qs_ref, kn_ref,
                                  vn_ref, spread_ref, os_ref, kbuf, vbuf, qbd_ref, m_ref, l_ref, acc_ref, npg=npg,
                                  n_tok=n_tok,
                                  lam_init=lam_init, last_part=steps_per_elem - 1)

    _prompt_attention(lam_ref, sg_ref, q_ref, kt_ref, v_ref, o_ref, tile=tile, wide=wide, lam_init=lam_init,
                      side=sample_pages)


def _sample_pages_step(copies, base, part, lam_ref, sg_ref, q_ref, kn_ref, vn_ref, spread_ref, o_ref, kbuf, vbuf,
                       qbd_ref, m_ref, l_ref, acc_ref, *, npg, n_tok, lam_init, last_part):
    for cp in copies:
        cp.wait()
    k_refs = [kbuf.at[base + p] for p in range(npg)]
    v_refs = [vbuf.at[base + p] for p in range(npg)]
    rows_per_head = 2 * n_tok
    head_rows = lambda h: slice(h * rows_per_head, (h + 1) * rows_per_head)

    @pl.when(part == 0)
    def _():
        q_tok = q_ref[...]
        lane_hc = lax.broadcasted_iota(_I32, (1, ATTN_WIDTH), 1) >> (HEAD_DIM.bit_length() - 1)
        for hc in range(2 * N_HEADS):
            qbd_ref[hc * n_tok:(hc + 1) * n_tok, :] = jnp.where(lane_hc == hc, q_tok, 0.0)
        q = qbd_ref[...]
        kn = kn_ref[...]
        vn = vn_ref[...]
        tok = lax.broadcasted_iota(_I32, (QROWS, 1), 0) & (n_tok - 1)
        ss = [jnp.where(tok >= jn, jnp.sum(q * kn[jn:jn + 1, :], axis=-1, keepdims=True), NEG_INF)
              for jn in range(n_tok)]
        m = functools.reduce(jnp.maximum, ss)
        ps = [jnp.exp(s - m) for s in ss]
        m_ref[...] = m
        l_ref[...] = functools.reduce(jnp.add, ps)
        for h in range(N_HEADS):
            lanes = slice(h * V_HEAD_DIM, (h + 1) * V_HEAD_DIM)
            acc_ref[head_rows(h), :] = functools.reduce(
                jnp.add, [ps[jn][head_rows(h)] * vn[jn:jn + 1, lanes] for jn in range(n_tok)])

    assert npg % 2 == 0
    q = qbd_ref[...]
    ss = []
    for a in range(0, npg, 2):
        pair = _dot(q, jnp.concatenate([k_refs[a][...], k_refs[a + 1][...]], axis=1))
        ss += [pair[:, :PAGE_SIZE], pair[:, PAGE_SIZE:]]
    m_old = m_ref[...]
    m_new = m_old
    for s in ss:
        m_new = jnp.maximum(m_new, jnp.max(s, axis=-1, keepdims=True))
    alpha = jnp.exp(m_old - m_new)
    ps = [jnp.exp(s - m_new) for s in ss]
    l = alpha * l_ref[...] + functools.reduce(jnp.add, [jnp.sum(p, axis=-1, keepdims=True) for p in ps])
    m_ref[...] = m_new
    l_ref[...] = l
    spread = _dot(jnp.concatenate(ps, axis=0).astype(_BF16), spread_ref[...])
    row_head = lax.broadcasted_iota(_I32, (QROWS, 1), 0) >> (rows_per_head.bit_length() - 1)
    own_head = (lax.broadcasted_iota(_I32, (1, PAGE_SIZE * N_HEADS), 1) & (N_HEADS - 1)) == row_head
    pv = functools.reduce(jnp.add, [
        _dot(jnp.where(own_head, spread[p * QROWS:(p + 1) * QROWS], 0.0).astype(_BF16), vr[...].astype(_BF16))
        for p, vr in enumerate(v_refs)])
    acc_ref[...] = alpha * acc_ref[...] + pv

    def finish():
        @pl.when(part == last_part)
        def _():
            o = acc_ref[...] / l_ref[...]
            lam = _lambda(lam_ref, lam_init)
            d = o - lam * pltpu.roll(o, QROWS - n_tok, 0)
            o_ref[...] = _sub_norm(d, sg_ref[...], lam_init)

    return finish


def _attn(page_table, lam_p, sg, q, kt, v, qrep, kn, vn, cache_k, cache_v, n_tok, lam_init):
    b, s, _ = q.shape
    tile = ATTN_TILE
    nq = s // tile
    n_steps = b * N_HEADS * nq
    nb, n_pages = page_table.shape
    assert (nb * n_pages) % n_steps == 0 and n_steps % nb == 0
    npg = nb * n_pages // n_steps
    steps_per_elem = n_steps // nb
    width = ATTN_WIDTH
    assert QROWS == 2 * N_HEADS * n_tok and cache_k.shape[1:] == (width, PAGE_SIZE)
    assert cache_v.shape[1:] == (PAGE_SIZE * N_HEADS, V_HEAD_DIM)

    spread = (jnp.arange(PAGE_SIZE)[:, None] == jnp.arange(PAGE_SIZE * N_HEADS)[None, :] // N_HEADS).astype(_BF16)
    elem = lambda bi, h, i: ((bi * N_HEADS + h) * nq + i) // steps_per_elem
    per_elem = lambda rows, cols: pl.BlockSpec((None, rows, cols), lambda bi, h, i, pt: (elem(bi, h, i), 0, 0))
    kern = functools.partial(_attn_kernel, tile=tile, wide=ATTN_WIDE, lam_init=lam_init, n_tok=n_tok,
                             n_steps=n_steps, pages_per_step=npg, steps_per_elem=steps_per_elem)
    grid_spec = pltpu.PrefetchScalarGridSpec(
        num_scalar_prefetch=1,
        grid=(b, N_HEADS, nq),
        in_specs=[
            pl.BlockSpec((4, HEAD_DIM), lambda bi, h, i, pt: (0, 0)),
            pl.BlockSpec((1, V_HEAD_DIM), lambda bi, h, i, pt: (0, 0)),
            pl.BlockSpec((None, tile, LANES), lambda bi, h, i, pt: (bi, i, h)),
            pl.BlockSpec((None, LANES, s), lambda bi, h, i, pt: (bi, h, 0)),
            pl.BlockSpec((None, s, LANES), lambda bi, h, i, pt: (bi, 0, h)),
            per_elem(n_tok, width), per_elem(kn.shape[1], width), per_elem(vn.shape[1], width),
            _whole_vmem(),
            pl.BlockSpec(memory_space=pl.ANY), pl.BlockSpec(memory_space=pl.ANY),
        ],
        out_specs=[pl.BlockSpec((None, tile, LANES), lambda bi, h, i, pt: (bi, i, h)),
                   per_elem(QROWS, V_HEAD_DIM)],
        scratch_shapes=[pltpu.VMEM((2 * npg, width, PAGE_SIZE), _F32),
                        pltpu.VMEM((2 * npg, PAGE_SIZE * N_HEADS, V_HEAD_DIM), _F32),
                        pltpu.SemaphoreType.DMA((2, 2 * npg)),
                        pltpu.VMEM((QROWS, width), _F32),
                        pltpu.VMEM((QROWS, 1), _F32), pltpu.VMEM((QROWS, 1), _F32),
                        pltpu.VMEM((QROWS, V_HEAD_DIM), _F32)],
    )
    return pl.pallas_call(
        kern,
        grid_spec=grid_spec,
        out_shape=[jax.ShapeDtypeStruct((b, s, ATTN_WIDTH), _BF16),
                   jax.ShapeDtypeStruct((nb, QROWS, V_HEAD_DIM), _F32)],
        compiler_params=_params(("arbitrary", "arbitrary", "arbitrary")),
        name="attn",
    )(page_table, lam_p, sg, q, kt, v, qrep, kn, vn, spread, cache_k, cache_v)


def _pool_windows(u_ref, halo_ref, hist_ref, *, tm, tiles_per_seq, start_pos):
    i = pl.program_id(0)
    t_in_seq = i % tiles_per_seq
    u = u_ref[...]
    halo = jnp.where(t_in_seq == 0, hist_ref[...], halo_ref[...])
    ext = jnp.concatenate([halo, u], axis=0)
    pos = start_pos + t_in_seq * tm + lax.broadcasted_iota(_I32, (tm, 1), 0)
    out = []
    for g, w in enumerate(POOL_WINDOWS):
        sl = slice(g * POOL_GROUP_WIDTH, (g + 1) * POOL_GROUP_WIDTH)
        acc = ext[:, sl]
        span = 1
        while span < w:
            acc = acc + pltpu.roll(acc, span, 0)
            span *= 2
        cnt = jnp.minimum(pos + 1, w).astype(_F32)
        out.append((acc[HALO_ROWS:] / cnt - u[:, sl]).astype(_BF16))
    return out


def _pool_specs(tm, tiles_per_seq):
    halo_blocks = max(tm // HALO_ROWS, 1)
    return [
        pl.BlockSpec((tm, POOL_WIDTH), lambda i: (i, 0)),
        pl.BlockSpec((HALO_ROWS, POOL_WIDTH),
                     lambda i: (jnp.where(i % tiles_per_seq == 0, 0, i * halo_blocks - 1), 0)),
        pl.BlockSpec((None, HALO_ROWS, POOL_WIDTH), lambda i: (i // tiles_per_seq, 0, 0)),
    ]


def _poolwin_kernel(u_ref, halo_ref, hist_ref, d_ref, **pool):
    gw = POOL_GROUP_WIDTH
    for g, d in enumerate(_pool_windows(u_ref, halo_ref, hist_ref, **pool)):
        d_ref[:, g * gw:(g + 1) * gw] = d


def _poolwin(u, hist, tm, tiles_per_seq, start_pos):
    t = u.shape[0]
    kern = functools.partial(_poolwin_kernel, tm=tm, tiles_per_seq=tiles_per_seq, start_pos=start_pos)
    return pl.pallas_call(
        kern,
        grid=(t // tm,),
        in_specs=_pool_specs(tm, tiles_per_seq),
        out_specs=pl.BlockSpec((tm, POOL_WIDTH), lambda i: (i, 0)),
        out_shape=jax.ShapeDtypeStruct((t, POOL_WIDTH), _BF16),
        compiler_params=_params(("arbitrary",)),
        name="poolwin",
    )(u, u, hist)


def _mix_kernel(x_ref, a_ref, *refs, tm, pool):
    n_pool = 1 if pool is None else 3
    (wp_ref, ps_ref, wo_ref, g2_ref, wrh_ref, wrl_ref, br_ref, cin_ref,
     x1_ref, h2_ref, route_ref, rect_ref, cout_ref, carry_ref) = refs[n_pool:]
    i = pl.program_id(0)

    @pl.when(i == 0)
    def _():
        carry_ref[...] = cin_ref[...]

    gw = POOL_GROUP_WIDTH
    if pool is None:
        d = refs[0][...]
        ds = [d[:, g * gw:(g + 1) * gw] for g in range(len(POOL_WINDOWS))]
    else:
        ds = _pool_windows(*refs[:3], tm=tm, tiles_per_seq=pool[0], start_pos=pool[1])
    py = jnp.concatenate([_dot(ds[g], wp_ref[g]) for g in range(len(POOL_WINDOWS))], axis=1) * ps_ref[...]
    x1 = (x_ref[...] + _dot(a_ref[...], wo_ref[0:ATTN_WIDTH, :])
          + _dot(py.astype(_BF16), wo_ref[ATTN_WIDTH:, :]))
    x1_ref[...] = x1
    ms = jnp.mean(x1 * x1, axis=-1, keepdims=True)
    h2 = x1 * lax.rsqrt(ms + RMS_EPS) * g2_ref[...]
    _store_token_major(h2_ref, h2)

    hh, hl = _split_bf16(h2)
    both = _dot(hh, wrl_ref[...])
    logits = both[:, :ROUTE_LANES] + both[:, ROUTE_LANES:] + _dot(hl, wrh_ref[...]) + br_ref[...]
    lane = lax.broadcasted_iota(_I32, (1, ROUTE_LANES), 1)
    big = jnp.int32(ROUTE_LANES)
    is_coarse = (lane >= N_EXPERTS) & (lane < N_EXPERTS + N_EXPERT_GROUPS)
    lc = jnp.where(is_coarse, logits, NEG_INF)
    mc = jnp.max(lc, axis=-1, keepdims=True)
    g_idx = jnp.min(jnp.where(lc == mc, lane, big), axis=-1, keepdims=True) - N_EXPERTS
    p_g = 1.0 / jnp.sum(jnp.exp(lc - mc), axis=-1, keepdims=True)
    in_group = (lane < N_EXPERTS) & ((lane >> (EXPERTS_PER_GROUP.bit_length() - 1)) == g_idx)
    lf = jnp.where(in_group, logits, NEG_INF)
    v1 = jnp.max(lf, axis=-1, keepdims=True)
    i1 = jnp.min(jnp.where(lf == v1, lane, big), axis=-1, keepdims=True)
    lf2 = jnp.where(lane == i1, NEG_INF, lf)
    v2 = jnp.max(lf2, axis=-1, keepdims=True)
    i2 = jnp.min(jnp.where(lf2 == v2, lane, big), axis=-1, keepdims=True)
    e21 = jnp.exp(v2 - v1)
    w1 = p_g / (1.0 + e21)
    w2 = p_g * e21 / (1.0 + e21)

    oh1 = lane == i1
    oh2 = lane == i2
    onehot = (oh1 | oh2).astype(_F32)
    r = lax.broadcasted_iota(_I32, (tm, tm), 0)
    col = lax.broadcasted_iota(_I32, (tm, tm), 1)
    lower = (col < r).astype(_BF16)
    before = _dot(lower, onehot.astype(_BF16)) + carry_ref[...]
    rank1 = jnp.sum(jnp.where(oh1, before, 0.0), axis=-1, keepdims=True)
    rank2 = jnp.sum(jnp.where(oh2, before, 0.0), axis=-1, keepdims=True)
    carry = carry_ref[...] + jnp.sum(onehot, axis=0, keepdims=True)
    carry_ref[...] = carry
    cout_ref[...] = carry

    rec = jnp.zeros((tm, ROUTE_LANES), _F32)
    for k, val in enumerate((i1.astype(_F32), i2.astype(_F32), w1, w2, rank1, rank2)):
        rec = jnp.where(lane == k, val, rec)
    route_ref[...] = rec
    rect_ref[...] = rec.T[0:REC_ROWS, :].astype(_I32)


def _mix(x, a, pool_in, wp_bf, ps, wo_bf, g2, wrh, wrl, br, counts_in, tm, pool=None):
    t = x.shape[0]
    row = lambda i: (i, 0)
    const = lambda i: (0, 0)
    kern = functools.partial(_mix_kernel, tm=tm, pool=pool)
    if pool is None:
        pool_specs, pool_args = [pl.BlockSpec((tm, POOL_WIDTH), row)], list(pool_in)
    else:
        pool_specs, pool_args = _pool_specs(tm, pool[0]), [pool_in[0], pool_in[0], pool_in[1]]
    return pl.pallas_call(
        kern,
        grid=(t // tm,),
        in_specs=[
            pl.BlockSpec((tm, D_MODEL), row),
            pl.BlockSpec((tm, ATTN_WIDTH), row),
            *pool_specs,
            _whole_vmem(),
            pl.BlockSpec((1, POOL_WIDTH), const),
            _whole_vmem(),
            pl.BlockSpec((1, D_MODEL), const),
            _whole_vmem(),
            _whole_vmem(),
            pl.BlockSpec((1, ROUTE_LANES), const),
            pl.BlockSpec((1, ROUTE_LANES), const),
        ],
        out_specs=[
            pl.BlockSpec((tm, D_MODEL), row),
            pl.BlockSpec((tm * ROW_CHUNKS, LANES), row),
            pl.BlockSpec((tm, ROUTE_LANES), row),
            pl.BlockSpec((None, REC_ROWS, tm), lambda i: (i, 0, 0)),
            pl.BlockSpec((1, ROUTE_LANES), const),
        ],
        out_shape=[
            jax.ShapeDtypeStruct((t, D_MODEL), _F32),
            jax.ShapeDtypeStruct((t * ROW_CHUNKS, LANES), _F32),
            jax.ShapeDtypeStruct((t, ROUTE_LANES), _F32),
            jax.ShapeDtypeStruct((t // tm, REC_ROWS, tm), _I32),
            jax.ShapeDtypeStruct((1, ROUTE_LANES), _F32),
        ],
        scratch_shapes=[pltpu.VMEM((1, ROUTE_LANES), _F32)],
        compiler_params=_params(("arbitrary",)),
        name="mix",
    )(x, a, *pool_args, wp_bf, ps, wo_bf, g2, wrh, wrl, br, counts_in)


def _tokens(ref, first, n):
    return ref.at[pl.ds(pl.multiple_of(first * ROW_CHUNKS, ROW_CHUNKS), n * ROW_CHUNKS), :]


def _row_copy(src, dst, src_tok, dst_tok, sem):
    return pltpu.make_async_copy(_tokens(src, src_tok, 1), _tokens(dst, dst_tok, 1), sem)


def _dispatch_kernel(zstart_ref, zpad_ref, pos_ref, h_ref, *rest, tm, zero_rows):
    if zero_rows:
        xs_ref, sem, zsem, zbuf = rest
    else:
        _, xs_ref, sem = rest
    i = pl.program_id(0)

    if zero_rows:
        def zero_fill(wait):
            def go(cp):
                cp.wait() if wait else cp.start()

            def body(e, c):
                cursor = zstart_ref[e]
                pad = zpad_ref[e]
                run = zero_rows // 2
                while run >= 1:
                    @pl.when((pad & run) != 0)
                    def _(cursor=cursor, run=run):
                        go(pltpu.make_async_copy(_tokens(zbuf, 0, run), _tokens(xs_ref, cursor, run), zsem))

                    cursor = cursor + (pad & run)
                    run //= 2
                return c

            lax.fori_loop(0, N_EXPERTS, body, 0)

            def tile(t, c):
                rows = zero_rows // 2
                go(pltpu.make_async_copy(_tokens(zbuf, 0, rows),
                                         _tokens(xs_ref, zstart_ref[N_EXPERTS] + t * rows, rows), zsem))
                return c

            lax.fori_loop(0, zpad_ref[N_EXPERTS], tile, 0)

        @pl.when(i == 0)
        def _():
            zbuf[...] = jnp.zeros_like(zbuf)
            zero_fill(False)
            zero_fill(True)

    def body(r, c):
        for k in range(2):
            _row_copy(h_ref, xs_ref, r, pos_ref[0, k, r], sem).start()
        return c

    lax.fori_loop(0, tm, body, 0, unroll=DMA_UNROLL)
    for _ in range(2):
        pltpu.make_async_copy(h_ref, _tokens(xs_ref, 0, tm), sem).wait()


def _dispatch(zstart, zpad, pos, h2, xs, n_rows, zero_rows):
    n_tiles, _, tm = pos.shape
    kern = functools.partial(_dispatch_kernel, tm=tm, zero_rows=zero_rows)
    in_specs = [
        pl.BlockSpec((1, 2, tm), lambda i, z, p: (i, 0, 0), memory_space=pltpu.SMEM),
        pl.BlockSpec((tm * ROW_CHUNKS, LANES), lambda i, z, p: (i, 0)),
    ]
    scratch = [pltpu.SemaphoreType.DMA(())]
    args = [zstart, zpad, pos, h2]
    aliases = {}
    if zero_rows:
        scratch += [pltpu.SemaphoreType.DMA(()), pltpu.VMEM((zero_rows // 2 * ROW_CHUNKS, LANES), _F32)]
    else:
        in_specs.append(pl.BlockSpec(memory_space=pl.ANY))
        args.append(xs)
        aliases = {4: 0}
    grid_spec = pltpu.PrefetchScalarGridSpec(
        num_scalar_prefetch=2, grid=(n_tiles,), in_specs=in_specs,
        out_specs=pl.BlockSpec(memory_space=pl.ANY), scratch_shapes=scratch)
    return pl.pallas_call(
        kern,
        grid_spec=grid_spec,
        out_shape=jax.ShapeDtypeStruct((n_rows * ROW_CHUNKS, LANES), _F32),
        input_output_aliases=aliases,
        compiler_params=_params(("arbitrary",)),
        name="dispatch",
    )(*args)


def _experts_kernel(texp_ref, nused_ref, x_ref, wg_ref, wu_ref, wd_ref, y_ref, wgb, wub, wdb, *, tm):
    i = pl.program_id(0)
    new_expert = (i == 0) | (texp_ref[i] != texp_ref[jnp.maximum(i - 1, 0)])

    @pl.when(new_expert)
    def _():
        wgb[...] = wg_ref[...].astype(_BF16)
        wub[...] = wu_ref[...].astype(_BF16)
        wdb[...] = wd_ref[...].astype(_BF16)

    @pl.when(i < nused_ref[0])
    def _():
        x = _load_token_major(x_ref, tm).astype(_BF16)
        hg = _dot(x, wgb[...])
        hu = _dot(x, wub[...])
        act = (hg * jax.nn.sigmoid(hg) * hu).astype(_BF16)
        y_ref[...] = _dot(act, wdb[...])

    @pl.when(i >= nused_ref[0])
    def _():
        y_ref[...] = jnp.zeros_like(y_ref)


def _experts(tile_expert, n_used, xs, w_gate, w_up, w_down, tm):
    n_tiles = tile_expert.shape[0]
    w_in_spec = pl.BlockSpec((None, D_MODEL, D_EXPERT), lambda i, te, nu: (te[i], 0, 0))
    grid_spec = pltpu.PrefetchScalarGridSpec(
        num_scalar_prefetch=2,
        grid=(n_tiles,),
        in_specs=[
            pl.BlockSpec((tm * ROW_CHUNKS, LANES), lambda i, te, nu: (jnp.minimum(i, nu[0] - 1), 0)),
            w_in_spec, w_in_spec,
            pl.BlockSpec((None, D_EXPERT, D_MODEL), lambda i, te, nu: (te[i], 0, 0)),
        ],
        out_specs=pl.BlockSpec((tm, D_MODEL), lambda i, te, nu: (i, 0)),
        scratch_shapes=[pltpu.VMEM((D_MODEL, D_EXPERT), _BF16), pltpu.VMEM((D_MODEL, D_EXPERT), _BF16),
                        pltpu.VMEM((D_EXPERT, D_MODEL), _BF16)],
    )
    return pl.pallas_call(
        functools.partial(_experts_kernel, tm=tm),
        grid_spec=grid_spec,
        out_shape=jax.ShapeDtypeStruct((n_tiles * tm, D_MODEL), _F32),
        compiler_params=_params(("arbitrary",)),
        name="experts",
    )(tile_expert, n_used, xs, w_gate, w_up, w_down)


def _combine_kernel(pos_ref, posn_ref, x1_ref, route_ref, y_hbm, o_ref, r0, r1, sem, *, tm):
    i = pl.program_id(0)
    n = pl.num_programs(0)
    bufs = (r0, r1)

    def issue(pos, slot):
        def body(r, c):
            for k in range(2):
                pltpu.make_async_copy(y_hbm.at[pl.ds(pos[0, k, r], 1), :], bufs[k].at[slot, pl.ds(r, 1), :],
                                      sem.at[slot, k]).start()
            return c

        lax.fori_loop(0, tm, body, 0, unroll=DMA_UNROLL)

    @pl.when(i == 0)
    def _():
        issue(pos_ref, 0)

    @pl.when(i + 1 < n)
    def _():
        issue(posn_ref, (i + 1) % 2)

    slot = i % 2
    for k in range(2):
        pltpu.make_async_copy(y_hbm.at[pl.ds(0, tm), :], bufs[k].at[slot], sem.at[slot, k]).wait()
    rec = route_ref[...]
    o_ref[...] = x1_ref[...] + rec[:, 2:3] * r0[slot] + rec[:, 3:4] * r1[slot]


def _combine(pos, x1, route, y_sorted):
    n_tiles, _, tm = pos.shape
    t = x1.shape[0]
    kern = functools.partial(_combine_kernel, tm=tm)
    pos_spec = lambda nxt: pl.BlockSpec(
        (1, 2, tm), lambda i: (jnp.minimum(i + nxt, n_tiles - 1), 0, 0), memory_space=pltpu.SMEM)
    return pl.pallas_call(
        kern,
        grid=(n_tiles,),
        in_specs=[
            pos_spec(0), pos_spec(1),
            pl.BlockSpec((tm, D_MODEL), lambda i: (i, 0)),
            pl.BlockSpec((tm, ROUTE_LANES), lambda i: (i, 0)),
            pl.BlockSpec(memory_space=pl.ANY),
        ],
        out_specs=pl.BlockSpec((tm, D_MODEL), lambda i: (i, 0)),
        scratch_shapes=[pltpu.VMEM((2, tm, D_MODEL), _F32), pltpu.VMEM((2, tm, D_MODEL), _F32),
                        pltpu.SemaphoreType.DMA((2, 2))],
        out_shape=jax.ShapeDtypeStruct((t, D_MODEL), _F32),
        compiler_params=_params(("arbitrary",)),
        name="combine",
    )(pos, pos, x1, route, y_sorted)


def _rope_tables(pos):
    inv = ROPE_THETA ** (-jnp.arange(0, HEAD_DIM, 2, dtype=_F32) / HEAD_DIM)
    ang = pos.astype(_F32)[:, None] * inv[None, :]
    cos = jnp.cos(ang)
    sin = jnp.sin(ang)
    return jnp.tile(cos, (1, 4)), jnp.tile(jnp.concatenate([-sin, sin], axis=1), (1, 2))


def _head_sum_matrices():
    lane_head = jnp.arange(ATTN_WIDTH) // HEAD_DIM
    cols = jnp.arange(LANES)
    esum = (lane_head[:, None] == cols[None, :]).astype(_F32) / HEAD_DIM
    eexp = (cols[:, None] == lane_head[None, :]).astype(_F32)
    return esum.astype(_BF16), jnp.concatenate([eexp, eexp], axis=0).astype(_BF16)


def kernel(x_prompt, x_sample, cache_k, cache_v, state_pool, page_table, norm1_g, w_in, q_norm_g, k_norm_g,
           lambda_q1, lambda_k1, lambda_q2, lambda_k2, subln_g, w_pool, pool_scale, w_out, norm2_g, w_coarse,
           b_coarse, w_fine, b_fine, w_gate, w_up, w_down):
    depth = w_in.shape[0]
    assert depth == 1
    l = 0
    lam_init = 0.8 - 0.6 * math.exp(-0.3 * l)
    bp, sp, _ = x_prompt.shape
    bs, ts, _ = x_sample.shape
    tp, tsamp = bp * sp, bs * ts
    n_pages = page_table.shape[1]
    past = n_pages * PAGE_SIZE
    n_hc = 2 * N_HEADS

    w_in_bf = w_in[l].astype(_BF16)
    w_out_bf = w_out[l].astype(_BF16)
    w_pool_bf = w_pool[l].astype(_BF16)
    g1 = norm1_g[l][None, :]
    g2 = norm2_g[l][None, :]
    gq = jnp.tile(q_norm_g[l], n_hc)[None, :]
    gk = jnp.tile(k_norm_g[l], n_hc)[None, :]
    sg = subln_g[l][None, :]
    ps = pool_scale[l][None, :]
    lam_p = jnp.stack([lambda_q1[l], lambda_k1[l], lambda_q2[l], lambda_k2[l]])
    esum, eexp = _head_sum_matrices()
    pad = ROUTE_LANES - N_EXPERTS - N_EXPERT_GROUPS
    w_fine_flat = jnp.transpose(w_fine[l], (1, 0, 2)).reshape(D_MODEL, N_EXPERTS)
    w_route = jnp.concatenate([w_fine_flat, w_coarse[l], jnp.zeros((D_MODEL, pad), _F32)], axis=1)
    wrh, wrl = _split_bf16(w_route)
    wrl = jnp.concatenate([wrh, wrl], axis=1)
    b_route = jnp.concatenate([b_fine[l].reshape(-1), b_coarse[l], jnp.zeros((pad,), _F32)])[None, :]

    cos_p, sin_p = _rope_tables(jnp.arange(sp, dtype=jnp.int32))
    cos_s, sin_s = _rope_tables(past + jnp.arange(ts, dtype=jnp.int32))
    cos_s, sin_s = jnp.tile(cos_s, (bs, 1)), jnp.tile(sin_s, (bs, 1))
    xp2 = x_prompt.reshape(tp, D_MODEL)
    xs2 = x_sample.reshape(tsamp, D_MODEL)
    qp, kpt, kpt_bf, vp, vp_bf, up = _proj(xp2, g1, w_in_bf, gq, gk, cos_p, sin_p, esum, eexp, PROJ_TILE, sp)
    qs, ks, _, vs, _, us = _proj(xs2, g1, w_in_bf, gq, gk, cos_s, sin_s, esum, eexp, tsamp)

    qrep = qs.astype(_F32).reshape(bs, ts, ATTN_WIDTH)
    new_rows = 8
    kn = jnp.pad(ks.reshape(bs, ts, ATTN_WIDTH), ((0, 0), (0, new_rows - ts), (0, 0)))
    vn = jnp.pad(vs.reshape(bs, ts, ATTN_WIDTH), ((0, 0), (0, new_rows - ts), (0, 0)))
    n_phys = cache_k.shape[1]
    ck = jnp.transpose(cache_k[l], (0, 2, 3, 1)).reshape(n_phys, ATTN_WIDTH, PAGE_SIZE)
    cv = cache_v[l].reshape(n_phys, PAGE_SIZE * N_HEADS, V_HEAD_DIM)
    shp = (bp, sp, ATTN_WIDTH)
    a_p, a_s = _attn(page_table, lam_p, sg, qp.reshape(shp), kpt_bf, vp_bf.reshape(shp), qrep, kn, vn, ck, cv,
                     ts, lam_init)
    a_p = a_p.reshape(tp, ATTN_WIDTH)
    a_s = a_s.reshape(bs, N_HEADS, 2, ts, V_HEAD_DIM)[:, :, 0]
    a_s = a_s.transpose(0, 2, 1, 3).reshape(tsamp, ATTN_WIDTH).astype(_BF16)

    hist_p = jnp.zeros((bp, HALO_ROWS, POOL_WIDTH), _F32)
    seq_rows = 8
    us_pad = jnp.pad(us.reshape(bs, ts, POOL_WIDTH), ((0, 0), (0, seq_rows - ts), (0, 0)))
    hist_s = jnp.pad(state_pool[l], ((0, 0), (HALO_ROWS - POOL_HIST, 0), (0, 0)))
    d_s = _poolwin(us_pad.reshape(bs * seq_rows, POOL_WIDTH), hist_s, seq_rows, 1, past)
    d_s = d_s.reshape(bs, seq_rows, POOL_WIDTH)[:, :ts].reshape(tsamp, POOL_WIDTH)

    counts0 = jnp.zeros((1, ROUTE_LANES), _F32)
    x1_p, h2_p, route_p, rect_p, counts_p = _mix(xp2, a_p, (up, hist_p), w_pool_bf, ps, w_out_bf, g2, wrh, wrl,
                                                 b_route, counts0, MIX_TILE, pool=(sp // MIX_TILE, 0))
    x1_s, h2_s, route_s, rect_s, counts = _mix(xs2, a_s, (d_s,), w_pool_bf, ps, w_out_bf, g2, wrh, wrl, b_route,
                                               counts_p, tsamp)

    tm = EXPERT_TILE
    t_all = tp + tsamp
    n_tiles = (2 * t_all + N_EXPERTS * (tm - 1) + tm - 1) // tm
    cnt = counts[0, :N_EXPERTS].astype(jnp.int32)
    padded = (cnt + tm - 1) // tm * tm
    ends = jnp.cumsum(padded)
    off = ends - padded
    n_used = (ends[-1] // tm).astype(jnp.int32)
    tile_ids = jnp.arange(n_tiles, dtype=jnp.int32)
    te = jnp.sum((ends[None, :] <= (tile_ids * tm)[:, None]).astype(jnp.int32), axis=1)
    te = jnp.minimum(te, N_EXPERTS - 1)
    last = jnp.sum(jnp.where(tile_ids == n_used - 1, te, 0))
    tile_expert = jnp.where(tile_ids < n_used, te, last)

    n_rows = n_tiles * tm
    zstart = off + counts_p[0, :N_EXPERTS].astype(jnp.int32)
    zinfo = (jnp.concatenate([zstart, ends[-1:]]), jnp.concatenate([ends - zstart, (n_tiles - n_used)[None]]))
    assert tsamp <= tm

    def sorted_rows(rect):
        experts = jnp.arange(N_EXPERTS, dtype=jnp.int32)
        base = jnp.sum(jnp.where(rect[:, 0:2, :, None] == experts, off, 0), axis=-1)
        return base + rect[:, 4:6, :]

    pos_p, pos_s = sorted_rows(rect_p), sorted_rows(rect_s)
    xs = _dispatch(*zinfo, pos_p, h2_p, None, n_rows, 2 * tm)
    xs = _dispatch(*zinfo, pos_s, h2_s, xs, n_rows, 0)
    y_sorted = _experts(tile_expert, n_used.reshape(1), xs, w_gate[l], w_up[l], w_down[l], tm)
    y_p = _combine(pos_p, x1_p, route_p, y_sorted)
    y_s = _combine(pos_s, x1_s, route_s, y_sorted)

    new_k_p = jnp.transpose(kpt.reshape(bp, n_hc, HEAD_DIM, sp), (0, 3, 1, 2))
    new_pool_p = up.reshape(bp, sp, POOL_WIDTH)[:, sp - POOL_HIST:]
    new_pool_s = jnp.concatenate([state_pool[l], us.reshape(bs, ts, POOL_WIDTH)], axis=1)[:, -POOL_HIST:]
    return (y_p.reshape(bp, sp, D_MODEL),
            y_s.reshape(bs, ts, D_MODEL),
            new_k_p[None],
            vp.reshape(1, bp, sp, N_HEADS, V_HEAD_DIM),
            new_pool_p[None],
            ks.reshape(1, bs, ts, n_hc, HEAD_DIM),
            vs.reshape(1, bs, ts, N_HEADS, V_HEAD_DIM),
            new_pool_s[None])
```

```python
import functools
import math

import jax
import jax.numpy as jnp
from jax import lax
from jax.experimental import pallas as pl
from jax.experimental.pallas import tpu as pltpu

_F32 = jnp.float32
_BF16 = jnp.bfloat16
_I32 = jnp.int32

D_MODEL = 2048
ATTN_WIDTH = 1024
POOL_WIDTH = 1024
HEAD_DIM = 64
N_HEADS = 8
V_HEAD_DIM = 128
POOL_WINDOWS = (2, 4, 8, 16)
POOL_GROUP_WIDTH = 256
POOL_HIST = 15
HALO_ROWS = 16
N_EXPERTS = 32
EXPERTS_PER_GROUP = 8
N_EXPERT_GROUPS = 4
D_EXPERT = 256
ROPE_THETA = 10000.0
RMS_EPS = 1e-6
NEG_INF = -1e30
PAGE_SIZE = 128
LANES = 128
ROUTE_LANES = 128
REC_ROWS = 8
VMEM_LIMIT = 56 * 1024 * 1024

PROJ_TILE = 512
ATTN_TILE = 512
ATTN_WIDE = 4
MIX_TILE = 512
EXPERT_TILE = 256
QROWS = 64
DMA_UNROLL = 8


def _dot(a, b):
    return jnp.dot(a, b, preferred_element_type=_F32)


def _split_bf16(x):
    hi = x.astype(_BF16)
    lo = (x - hi.astype(_F32)).astype(_BF16)
    return hi, lo


def _params(sem):
    return pltpu.CompilerParams(dimension_semantics=sem, vmem_limit_bytes=VMEM_LIMIT)


ROW_CHUNKS = D_MODEL // LANES


def _load_token_major(ref, n):
    return jnp.concatenate([ref[pl.ds(c, n, stride=ROW_CHUNKS), :] for c in range(ROW_CHUNKS)], axis=1)


def _store_token_major(ref, x):
    n = x.shape[0]
    for c in range(ROW_CHUNKS):
        ref[pl.ds(c, n, stride=ROW_CHUNKS), :] = x[:, c * LANES:(c + 1) * LANES]


def _whole_vmem():
    return pl.BlockSpec(memory_space=pltpu.VMEM)


def _proj_kernel(x_ref, g1_ref, w_ref, gq_ref, gk_ref, cos_ref, sin_ref, esum_ref, eexp_ref,
                 q_ref, kf_ref, kb_ref, vf_ref, vb_ref, u_ref, *, k_transposed):
    x = x_ref[...]
    ms = jnp.mean(x * x, axis=-1, keepdims=True)
    h = (x * lax.rsqrt(ms + RMS_EPS) * g1_ref[...]).astype(_BF16)
    cos = cos_ref[...]
    sin = sin_ref[...]
    lane = lax.broadcasted_iota(_I32, (1, LANES), 1)
    upper = (lane & (HEAD_DIM - 1)) >= HEAD_DIM // 2

    def normed_rope(z, g_ref, outs, transposed):
        msq = _dot((z * z).astype(_BF16), esum_ref[...])
        rh, rl = _split_bf16(lax.rsqrt(msq + RMS_EPS))
        rb = _dot(jnp.concatenate([rh, rl], axis=1), eexp_ref[...])
        n = z * rb * g_ref[...]
        for j in range(ATTN_WIDTH // LANES):
            sl = slice(j * LANES, (j + 1) * LANES)
            nj = n[:, sl]
            swapped = jnp.where(upper, pltpu.roll(nj, HEAD_DIM // 2, 1),
                                pltpu.roll(nj, LANES - HEAD_DIM // 2, 1))
            o = nj * cos + swapped * sin
            if transposed:
                ot = o.T
                for ref, _ in outs:
                    ref[sl, :] = ot.astype(ref.dtype)
            else:
                for ref, scale in outs:
                    ref[:, sl] = (o * scale).astype(ref.dtype)

    a = ATTN_WIDTH
    normed_rope(_dot(h, w_ref[:, 0:a]), gq_ref, ((q_ref, HEAD_DIM ** -0.5),), False)
    normed_rope(_dot(h, w_ref[:, a:2 * a]), gk_ref, ((kf_ref, 1.0), (kb_ref, 1.0)), k_transposed)
    zv = _dot(h, w_ref[:, 2 * a:3 * a])
    vf_ref[...] = zv
    vb_ref[...] = zv.astype(_BF16)
    u_ref[...] = _dot(h, w_ref[:, 3 * a:])


def _proj(x, g1, w_bf, gq, gk, cos, sin, esum, eexp, tm, seq_len=None):
    t = x.shape[0]
    n_pos_tiles = cos.shape[0] // tm
    row = lambda i: (i, 0)
    const = lambda i: (0, 0)
    tok_spec = lambda w: pl.BlockSpec((tm, w), row)
    a = ATTN_WIDTH
    if seq_len is None:
        k_spec, k_shape = tok_spec(a), (t, a)
    else:
        tiles_per_seq = seq_len // tm
        k_spec = pl.BlockSpec((None, a, tm), lambda i: (i // tiles_per_seq, 0, i % tiles_per_seq))
        k_shape = (t // seq_len, a, seq_len)
    return pl.pallas_call(
        functools.partial(_proj_kernel, k_transposed=seq_len is not None),
        grid=(t // tm,),
        in_specs=[
            tok_spec(D_MODEL),
            pl.BlockSpec((1, D_MODEL), const),
            _whole_vmem(),
            pl.BlockSpec((1, a), const),
            pl.BlockSpec((1, a), const),
            pl.BlockSpec((tm, LANES), lambda i: (i % n_pos_tiles, 0)),
            pl.BlockSpec((tm, LANES), lambda i: (i % n_pos_tiles, 0)),
            _whole_vmem(),
            _whole_vmem(),
        ],
        out_specs=[tok_spec(a), k_spec, k_spec, tok_spec(a), tok_spec(a), tok_spec(POOL_WIDTH)],
        out_shape=[
            jax.ShapeDtypeStruct((t, a), _BF16),
            jax.ShapeDtypeStruct(k_shape, _F32),
            jax.ShapeDtypeStruct(k_shape, _BF16),
            jax.ShapeDtypeStruct((t, a), _F32),
            jax.ShapeDtypeStruct((t, a), _BF16),
            jax.ShapeDtypeStruct((t, POOL_WIDTH), _F32),
        ],
        compiler_params=_params(("arbitrary",)),
        name="proj",
    )(x, g1, w_bf, gq, gk, cos, sin, esum, eexp)


def _lambda(lam_ref, lam_init):
    lp = lam_ref[...]
    s1 = jnp.sum(lp[0:1] * lp[1:2], axis=-1, keepdims=True)
    s2 = jnp.sum(lp[2:3] * lp[3:4], axis=-1, keepdims=True)
    return jnp.exp(s1) - jnp.exp(s2) + lam_init


def _sub_norm(o, sg, lam_init):
    ms = jnp.mean(o * o, axis=-1, keepdims=True)
    return o * lax.rsqrt(ms + RMS_EPS) * sg * (1.0 - lam_init)


def _prompt_attention(lam_ref, sg_ref, q_ref, kt_ref, v_ref, o_ref, *, tile, wide, lam_init, side):
    qi = pl.program_id(2)
    q = q_ref[...]
    lane = lax.broadcasted_iota(_I32, (1, LANES), 1)
    zero = jnp.zeros_like(q)
    qc = (jnp.where(lane < HEAD_DIM, q, zero), jnp.where(lane >= HEAD_DIM, q, zero))

    def update(s, state, vblk):
        m, l, acc = state
        m_new = jnp.maximum(m, jnp.max(s, axis=-1, keepdims=True))
        alpha = jnp.exp(m - m_new)
        p = jnp.exp(s - m_new)
        l = alpha * l + jnp.sum(p, axis=-1, keepdims=True)
        acc = alpha * acc + _dot(p.astype(_BF16), vblk)
        return m_new, l, acc

    def block(first, carry, masked, width):
        start = pl.multiple_of(first * tile, tile)
        ktblk = kt_ref[:, pl.ds(start, width)]
        vblk = v_ref[pl.ds(start, width), :]
        out = []
        for c in range(2):
            s = _dot(qc[c], ktblk)
            if masked:
                r = lax.broadcasted_iota(_I32, (tile, width), 0)
                col = lax.broadcasted_iota(_I32, (tile, width), 1)
                s = jnp.where(col <= r + (width - tile), s, NEG_INF)
            out.append(update(s, carry[c], vblk))
        return tuple(out)

    init_one = (jnp.full((tile, 1), NEG_INF, _F32), jnp.zeros((tile, 1), _F32),
                jnp.zeros((tile, V_HEAD_DIM), _F32))
    n_full = qi // wide
    carry = lax.fori_loop(0, n_full, lambda j, c: block(j * wide, c, False, wide * tile), (init_one, init_one))
    lam = _lambda(lam_ref, lam_init)
    for last in range(1, wide + 1):
        @pl.when(qi + 1 - n_full * wide == last)
        def _(last=last):
            side_finish = side()
            (_, l0, a0), (_, l1, a1) = block(n_full * wide, carry, True, last * tile)
            o = a0 / l0 - lam * (a1 / l1)
            o_ref[...] = _sub_norm(o, sg_ref[...], lam_init).astype(o_ref.dtype)
            side_finish()


def _attn_kernel(pt_ref, lam_ref, sg_ref, q_ref, kt_ref, v_ref, qs_ref, kn_ref, vn_ref, spread_ref, ck_hbm, cv_hbm,
                 o_ref, os_ref, kbuf, vbuf, sem, qbd_ref, m_ref, l_ref, acc_ref, *, tile, wide, lam_init, n_tok,
                 n_steps, pages_per_step, steps_per_elem):
    npg = pages_per_step
    step = (pl.program_id(0) * pl.num_programs(1) + pl.program_id(1)) * pl.num_programs(2) + pl.program_id(2)
    assert steps_per_elem & (steps_per_elem - 1) == 0
    part = step & (steps_per_elem - 1)

    def page_copies(s):
        elem = lax.shift_right_logical(s, steps_per_elem.bit_length() - 1)
        first_page = (s & (steps_per_elem - 1)) * npg
        base = (s & 1) * npg
        out = []
        for p in range(npg):
            page = pt_ref[elem, first_page + p]
            out.append(pltpu.make_async_copy(ck_hbm.at[page], kbuf.at[base + p], sem.at[0, base + p]))
            out.append(pltpu.make_async_copy(cv_hbm.at[page], vbuf.at[base + p], sem.at[1, base + p]))
        return out

    @pl.when(step == 0)
    def _():
        for cp in page_copies(step):
            cp.start()

    @pl.when(step + 1 < n_steps)
    def _():
        for cp in page_copies(step + 1):
            cp.start()

    def sample_pages():
        return _sample_pages_step(page_copies(step), (step & 1) * npg, part, lam_ref, sg_ref, qs_ref, kn_ref,
                                  vn_ref, spread_ref, os_ref, kbuf, vbuf, qbd_ref, m_ref, l_ref, acc_ref, npg=npg,
                                  n_tok=n_tok,
                                  lam_init=lam_init, last_part=steps_per_elem - 1)

    _prompt_attention(lam_ref, sg_ref, q_ref, kt_ref, v_ref, o_ref, tile=tile, wide=wide, lam_init=lam_init,
                      side=sample_pages)


def _sample_pages_step(copies, base, part, lam_ref, sg_ref, q_ref, kn_ref, vn_ref, spread_ref, o_ref, kbuf, vbuf,
                       qbd_ref, m_ref, l_ref, acc_ref, *, npg, n_tok, lam_init, last_part):
    for cp in copies:
        cp.wait()
    k_refs = [kbuf.at[base + p] for p in range(npg)]
    v_refs = [vbuf.at[base + p] for p in range(npg)]
    rows_per_head = 2 * n_tok
    head_rows = lambda h: slice(h * rows_per_head, (h + 1) * rows_per_head)

    @pl.when(part == 0)
    def _():
        q_tok = q_ref[...]
        lane_hc = lax.broadcasted_iota(_I32, (1, ATTN_WIDTH), 1) >> (HEAD_DIM.bit_length() - 1)
        for hc in range(2 * N_HEADS):
            qbd_ref[hc * n_tok:(hc + 1) * n_tok, :] = jnp.where(lane_hc == hc, q_tok, 0.0)
        q = qbd_ref[...]
        kn = kn_ref[...]
        vn = vn_ref[...]
        tok = lax.broadcasted_iota(_I32, (QROWS, 1), 0) & (n_tok - 1)
        ss = [jnp.where(tok >= jn, jnp.sum(q * kn[jn:jn + 1, :], axis=-1, keepdims=True), NEG_INF)
              for jn in range(n_tok)]
        m = functools.reduce(jnp.maximum, ss)
        ps = [jnp.exp(s - m) for s in ss]
        m_ref[...] = m
        l_ref[...] = functools.reduce(jnp.add, ps)
        for h in range(N_HEADS):
            lanes = slice(h * V_HEAD_DIM, (h + 1) * V_HEAD_DIM)
            acc_ref[head_rows(h), :] = functools.reduce(
                jnp.add, [ps[jn][head_rows(h)] * vn[jn:jn + 1, lanes] for jn in range(n_tok)])

    assert npg % 2 == 0
    q = qbd_ref[...]
    ss = []
    for a in range(0, npg, 2):
        pair = _dot(q, jnp.concatenate([k_refs[a][...], k_refs[a + 1][...]], axis=1))
        ss += [pair[:, :PAGE_SIZE], pair[:, PAGE_SIZE:]]
    m_old = m_ref[...]
    m_new = m_old
    for s in ss:
        m_new = jnp.maximum(m_new, jnp.max(s, axis=-1, keepdims=True))
    alpha = jnp.exp(m_old - m_new)
    ps = [jnp.exp(s - m_new) for s in ss]
    l = alpha * l_ref[...] + functools.reduce(jnp.add, [jnp.sum(p, axis=-1, keepdims=True) for p in ps])
    m_ref[...] = m_new
    l_ref[...] = l
    spread = _dot(jnp.concatenate(ps, axis=0).astype(_BF16), spread_ref[...])
    row_head = lax.broadcasted_iota(_I32, (QROWS, 1), 0) >> (rows_per_head.bit_length() - 1)
    own_head = (lax.broadcasted_iota(_I32, (1, PAGE_SIZE * N_HEADS), 1) & (N_HEADS - 1)) == row_head
    pv = functools.reduce(jnp.add, [
        _dot(jnp.where(own_head, spread[p * QROWS:(p + 1) * QROWS], 0.0).astype(_BF16), vr[...].astype(_BF16))
        for p, vr in enumerate(v_refs)])
    acc_ref[...] = alpha * acc_ref[...] + pv

    def finish():
        @pl.when(part == last_part)
        def _():
            o = acc_ref[...] / l_ref[...]
            lam = _lambda(lam_ref, lam_init)
            d = o - lam * pltpu.roll(o, QROWS - n_tok, 0)
            o_ref[...] = _sub_norm(d, sg_ref[...], lam_init)

    return finish


def _attn(page_table, lam_p, sg, q, kt, v, qrep, kn, vn, cache_k, cache_v, n_tok, lam_init):
    b, s, _ = q.shape
    tile = ATTN_TILE
    nq = s // tile
    n_steps = b * N_HEADS * nq
    nb, n_pages = page_table.shape
    assert (nb * n_pages) % n_steps == 0 and n_steps % nb == 0
    npg = nb * n_pages // n_steps
    steps_per_elem = n_steps // nb
    width = ATTN_WIDTH
    assert QROWS == 2 * N_HEADS * n_tok and cache_k.shape[1:] == (width, PAGE_SIZE)
    assert cache_v.shape[1:] == (PAGE_SIZE * N_HEADS, V_HEAD_DIM)

    spread = (jnp.arange(PAGE_SIZE)[:, None] == jnp.arange(PAGE_SIZE * N_HEADS)[None, :] // N_HEADS).astype(_BF16)
    elem = lambda bi, h, i: ((bi * N_HEADS + h) * nq + i) // steps_per_elem
    per_elem = lambda rows, cols: pl.BlockSpec((None, rows, cols), lambda bi, h, i, pt: (elem(bi, h, i), 0, 0))
    kern = functools.partial(_attn_kernel, tile=tile, wide=ATTN_WIDE, lam_init=lam_init, n_tok=n_tok,
                             n_steps=n_steps, pages_per_step=npg, steps_per_elem=steps_per_elem)
    grid_spec = pltpu.PrefetchScalarGridSpec(
        num_scalar_prefetch=1,
        grid=(b, N_HEADS, nq),
        in_specs=[
            pl.BlockSpec((4, HEAD_DIM), lambda bi, h, i, pt: (0, 0)),
            pl.BlockSpec((1, V_HEAD_DIM), lambda bi, h, i, pt: (0, 0)),
            pl.BlockSpec((None, tile, LANES), lambda bi, h, i, pt: (bi, i, h)),
            pl.BlockSpec((None, LANES, s), lambda bi, h, i, pt: (bi, h, 0)),
            pl.BlockSpec((None, s, LANES), lambda bi, h, i, pt: (bi, 0, h)),
            per_elem(n_tok, width), per_elem(kn.shape[1], width), per_elem(vn.shape[1], width),
            _whole_vmem(),
            pl.BlockSpec(memory_space=pl.ANY), pl.BlockSpec(memory_space=pl.ANY),
        ],
        out_specs=[pl.BlockSpec((None, tile, LANES), lambda bi, h, i, pt: (bi, i, h)),
                   per_elem(QROWS, V_HEAD_DIM)],
        scratch_shapes=[pltpu.VMEM((2 * npg, width, PAGE_SIZE), _F32),
                        pltpu.VMEM((2 * npg, PAGE_SIZE * N_HEADS, V_HEAD_DIM), _F32),
                        pltpu.SemaphoreType.DMA((2, 2 * npg)),
                        pltpu.VMEM((QROWS, width), _F32),
                        pltpu.VMEM((QROWS, 1), _F32), pltpu.VMEM((QROWS, 1), _F32),
                        pltpu.VMEM((QROWS, V_HEAD_DIM), _F32)],
    )
    return pl.pallas_call(
        kern,
        grid_spec=grid_spec,
        out_shape=[jax.ShapeDtypeStruct((b, s, ATTN_WIDTH), _BF16),
                   jax.ShapeDtypeStruct((nb, QROWS, V_HEAD_DIM), _F32)],
        compiler_params=_params(("arbitrary", "arbitrary", "arbitrary")),
        name="attn",
    )(page_table, lam_p, sg, q, kt, v, qrep, kn, vn, spread, cache_k, cache_v)


def _pool_windows(u_ref, halo_ref, hist_ref, *, tm, tiles_per_seq, start_pos):
    i = pl.program_id(0)
    t_in_seq = i % tiles_per_seq
    u = u_ref[...]
    halo = jnp.where(t_in_seq == 0, hist_ref[...], halo_ref[...])
    ext = jnp.concatenate([halo, u], axis=0)
    pos = start_pos + t_in_seq * tm + lax.broadcasted_iota(_I32, (tm, 1), 0)
    out = []
    for g, w in enumerate(POOL_WINDOWS):
        sl = slice(g * POOL_GROUP_WIDTH, (g + 1) * POOL_GROUP_WIDTH)
        acc = ext[:, sl]
        span = 1
        while span < w:
            acc = acc + pltpu.roll(acc, span, 0)
            span *= 2
        cnt = jnp.minimum(pos + 1, w).astype(_F32)
        out.append((acc[HALO_ROWS:] / cnt - u[:, sl]).astype(_BF16))
    return out


def _pool_specs(tm, tiles_per_seq):
    halo_blocks = max(tm // HALO_ROWS, 1)
    return [
        pl.BlockSpec((tm, POOL_WIDTH), lambda i: (i, 0)),
        pl.BlockSpec((HALO_ROWS, POOL_WIDTH),
                     lambda i: (jnp.where(i % tiles_per_seq == 0, 0, i * halo_blocks - 1), 0)),
        pl.BlockSpec((None, HALO_ROWS, POOL_WIDTH), lambda i: (i // tiles_per_seq, 0, 0)),
    ]


def _poolwin_kernel(u_ref, halo_ref, hist_ref, d_ref, **pool):
    gw = POOL_GROUP_WIDTH
    for g, d in enumerate(_pool_windows(u_ref, halo_ref, hist_ref, **pool)):
        d_ref[:, g * gw:(g + 1) * gw] = d


def _poolwin(u, hist, tm, tiles_per_seq, start_pos):
    t = u.shape[0]
    kern = functools.partial(_poolwin_kernel, tm=tm, tiles_per_seq=tiles_per_seq, start_pos=start_pos)
    return pl.pallas_call(
        kern,
        grid=(t // tm,),
        in_specs=_pool_specs(tm, tiles_per_seq),
        out_specs=pl.BlockSpec((tm, POOL_WIDTH), lambda i: (i, 0)),
        out_shape=jax.ShapeDtypeStruct((t, POOL_WIDTH), _BF16),
        compiler_params=_params(("arbitrary",)),
        name="poolwin",
    )(u, u, hist)


def _mix_kernel(x_ref, a_ref, *refs, tm, pool):
    n_pool = 1 if pool is None else 3
    (wp_ref, ps_ref, wo_ref, g2_ref, wrh_ref, wrl_ref, br_ref, cin_ref,
     x1_ref, h2_ref, route_ref, rect_ref, cout_ref, carry_ref) = refs[n_pool:]
    i = pl.program_id(0)

    @pl.when(i == 0)
    def _():
        carry_ref[...] = cin_ref[...]

    gw = POOL_GROUP_WIDTH
    if pool is None:
        d = refs[0][...]
        ds = [d[:, g * gw:(g + 1) * gw] for g in range(len(POOL_WINDOWS))]
    else:
        ds = _pool_windows(*refs[:3], tm=tm, tiles_per_seq=pool[0], start_pos=pool[1])
    py = jnp.concatenate([_dot(ds[g], wp_ref[g]) for g in range(len(POOL_WINDOWS))], axis=1) * ps_ref[...]
    x1 = (x_ref[...] + _dot(a_ref[...], wo_ref[0:ATTN_WIDTH, :])
          + _dot(py.astype(_BF16), wo_ref[ATTN_WIDTH:, :]))
    x1_ref[...] = x1
    ms = jnp.mean(x1 * x1, axis=-1, keepdims=True)
    h2 = x1 * lax.rsqrt(ms + RMS_EPS) * g2_ref[...]
    _store_token_major(h2_ref, h2)

    hh, hl = _split_bf16(h2)
    both = _dot(hh, wrl_ref[...])
    logits = both[:, :ROUTE_LANES] + both[:, ROUTE_LANES:] + _dot(hl, wrh_ref[...]) + br_ref[...]
    lane = lax.broadcasted_iota(_I32, (1, ROUTE_LANES), 1)
    big = jnp.int32(ROUTE_LANES)
    is_coarse = (lane >= N_EXPERTS) & (lane < N_EXPERTS + N_EXPERT_GROUPS)
    lc = jnp.where(is_coarse, logits, NEG_INF)
    mc = jnp.max(lc, axis=-1, keepdims=True)
    g_idx = jnp.min(jnp.where(lc == mc, lane, big), axis=-1, keepdims=True) - N_EXPERTS
    p_g = 1.0 / jnp.sum(jnp.exp(lc - mc), axis=-1, keepdims=True)
    in_group = (lane < N_EXPERTS) & ((lane >> (EXPERTS_PER_GROUP.bit_length() - 1)) == g_idx)
    lf = jnp.where(in_group, logits, NEG_INF)
    v1 = jnp.max(lf, axis=-1, keepdims=True)
    i1 = jnp.min(jnp.where(lf == v1, lane, big), axis=-1, keepdims=True)
    lf2 = jnp.where(lane == i1, NEG_INF, lf)
    v2 = jnp.max(lf2, axis=-1, keepdims=True)
    i2 = jnp.min(jnp.where(lf2 == v2, lane, big), axis=-1, keepdims=True)
    e21 = jnp.exp(v2 - v1)
    w1 = p_g / (1.0 + e21)
    w2 = p_g * e21 / (1.0 + e21)

    oh1 = lane == i1
    oh2 = lane == i2
    onehot = (oh1 | oh2).astype(_F32)
    r = lax.broadcasted_iota(_I32, (tm, tm), 0)
    col = lax.broadcasted_iota(_I32, (tm, tm), 1)
    lower = (col < r).astype(_BF16)
    before = _dot(lower, onehot.astype(_BF16)) + carry_ref[...]
    rank1 = jnp.sum(jnp.where(oh1, before, 0.0), axis=-1, keepdims=True)
    rank2 = jnp.sum(jnp.where(oh2, before, 0.0), axis=-1, keepdims=True)
    carry = carry_ref[...] + jnp.sum(onehot, axis=0, keepdims=True)
    carry_ref[...] = carry
    cout_ref[...] = carry

    rec = jnp.zeros((tm, ROUTE_LANES), _F32)
    for k, val in enumerate((i1.astype(_F32), i2.astype(_F32), w1, w2, rank1, rank2)):
        rec = jnp.where(lane == k, val, rec)
    route_ref[...] = rec
    rect_ref[...] = rec.T[0:REC_ROWS, :].astype(_I32)


def _mix(x, a, pool_in, wp_bf, ps, wo_bf, g2, wrh, wrl, br, counts_in, tm, pool=None):
    t = x.shape[0]
    row = lambda i: (i, 0)
    const = lambda i: (0, 0)
    kern = functools.partial(_mix_kernel, tm=tm, pool=pool)
    if pool is None:
        pool_specs, pool_args = [pl.BlockSpec((tm, POOL_WIDTH), row)], list(pool_in)
    else:
        pool_specs, pool_args = _pool_specs(tm, pool[0]), [pool_in[0], pool_in[0], pool_in[1]]
    return pl.pallas_call(
        kern,
        grid=(t // tm,),
        in_specs=[
            pl.BlockSpec((tm, D_MODEL), row),
            pl.BlockSpec((tm, ATTN_WIDTH), row),
            *pool_specs,
            _whole_vmem(),
            pl.BlockSpec((1, POOL_WIDTH), const),
            _whole_vmem(),
            pl.BlockSpec((1, D_MODEL), const),
            _whole_vmem(),
            _whole_vmem(),
            pl.BlockSpec((1, ROUTE_LANES), const),
            pl.BlockSpec((1, ROUTE_LANES), const),
        ],
        out_specs=[
            pl.BlockSpec((tm, D_MODEL), row),
            pl.BlockSpec((tm * ROW_CHUNKS, LANES), row),
            pl.BlockSpec((tm, ROUTE_LANES), row),
            pl.BlockSpec((None, REC_ROWS, tm), lambda i: (i, 0, 0)),
            pl.BlockSpec((1, ROUTE_LANES), const),
        ],
        out_shape=[
            jax.ShapeDtypeStruct((t, D_MODEL), _F32),
            jax.ShapeDtypeStruct((t * ROW_CHUNKS, LANES), _F32),
            jax.ShapeDtypeStruct((t, ROUTE_LANES), _F32),
            jax.ShapeDtypeStruct((t // tm, REC_ROWS, tm), _I32),
            jax.ShapeDtypeStruct((1, ROUTE_LANES), _F32),
        ],
        scratch_shapes=[pltpu.VMEM((1, ROUTE_LANES), _F32)],
        compiler_params=_params(("arbitrary",)),
        name="mix",
    )(x, a, *pool_args, wp_bf, ps, wo_bf, g2, wrh, wrl, br, counts_in)


def _tokens(ref, first, n):
    return ref.at[pl.ds(pl.multiple_of(first * ROW_CHUNKS, ROW_CHUNKS), n * ROW_CHUNKS), :]


def _row_copy(src, dst, src_tok, dst_tok, sem):
    return pltpu.make_async_copy(_tokens(src, src_tok, 1), _tokens(dst, dst_tok, 1), sem)


def _dispatch_kernel(zstart_ref, zpad_ref, pos_ref, h_ref, *rest, tm, zero_rows):
    if zero_rows:
        xs_ref, sem, zsem, zbuf = rest
    else:
        _, xs_ref, sem = rest
    i = pl.program_id(0)

    if zero_rows:
        def zero_fill(wait):
            def go(cp):
                cp.wait() if wait else cp.start()

            def body(e, c):
                cursor = zstart_ref[e]
                pad = zpad_ref[e]
                run = zero_rows // 2
                while run >= 1:
                    @pl.when((pad & run) != 0)
                    def _(cursor=cursor, run=run):
                        go(pltpu.make_async_copy(_tokens(zbuf, 0, run), _tokens(xs_ref, cursor, run), zsem))

                    cursor = cursor + (pad & run)
                    run //= 2
                return c

            lax.fori_loop(0, N_EXPERTS, body, 0)

            def tile(t, c):
                rows = zero_rows // 2
                go(pltpu.make_async_copy(_tokens(zbuf, 0, rows),
                                         _tokens(xs_ref, zstart_ref[N_EXPERTS] + t * rows, rows), zsem))
                return c

            lax.fori_loop(0, zpad_ref[N_EXPERTS], tile, 0)

        @pl.when(i == 0)
        def _():
            zbuf[...] = jnp.zeros_like(zbuf)
            zero_fill(False)
            zero_fill(True)

    def body(r, c):
        for k in range(2):
            _row_copy(h_ref, xs_ref, r, pos_ref[0, k, r], sem).start(priority=k)
        return c

    lax.fori_loop(0, tm, body, 0, unroll=DMA_UNROLL)
    for _ in range(2):
        pltpu.make_async_copy(h_ref, _tokens(xs_ref, 0, tm), sem).wait()


def _dispatch(zstart, zpad, pos, h2, xs, n_rows, zero_rows):
    n_tiles, _, tm = pos.shape
    kern = functools.partial(_dispatch_kernel, tm=tm, zero_rows=zero_rows)
    in_specs = [
        pl.BlockSpec((1, 2, tm), lambda i, z, p: (i, 0, 0), memory_space=pltpu.SMEM),
        pl.BlockSpec((tm * ROW_CHUNKS, LANES), lambda i, z, p: (i, 0)),
    ]
    scratch = [pltpu.SemaphoreType.DMA(())]
    args = [zstart, zpad, pos, h2]
    aliases = {}
    if zero_rows:
        scratch += [pltpu.SemaphoreType.DMA(()), pltpu.VMEM((zero_rows // 2 * ROW_CHUNKS, LANES), _F32)]
    else:
        in_specs.append(pl.BlockSpec(memory_space=pl.ANY))
        args.append(xs)
        aliases = {4: 0}
    grid_spec = pltpu.PrefetchScalarGridSpec(
        num_scalar_prefetch=2, grid=(n_tiles,), in_specs=in_specs,
        out_specs=pl.BlockSpec(memory_space=pl.ANY), scratch_shapes=scratch)
    return pl.pallas_call(
        kern,
        grid_spec=grid_spec,
        out_shape=jax.ShapeDtypeStruct((n_rows * ROW_CHUNKS, LANES), _F32),
        input_output_aliases=aliases,
        compiler_params=_params(("arbitrary",)),
        name="dispatch",
    )(*args)


def _experts_kernel(texp_ref, nused_ref, x_ref, wg_ref, wu_ref, wd_ref, y_ref, wgb, wub, wdb, *, tm):
    i = pl.program_id(0)
    new_expert = (i == 0) | (texp_ref[i] != texp_ref[jnp.maximum(i - 1, 0)])

    @pl.when(new_expert)
    def _():
        wgb[...] = wg_ref[...].astype(_BF16)
        wub[...] = wu_ref[...].astype(_BF16)
        wdb[...] = wd_ref[...].astype(_BF16)

    @pl.when(i < nused_ref[0])
    def _():
        x = _load_token_major(x_ref, tm).astype(_BF16)
        hg = _dot(x, wgb[...])
        hu = _dot(x, wub[...])
        act = (hg * jax.nn.sigmoid(hg) * hu).astype(_BF16)
        y_ref[...] = _dot(act, wdb[...])

    @pl.when(i >= nused_ref[0])
    def _():
        y_ref[...] = jnp.zeros_like(y_ref)


def _experts(tile_expert, n_used, xs, w_gate, w_up, w_down, tm):
    n_tiles = tile_expert.shape[0]
    w_in_spec = pl.BlockSpec((None, D_MODEL, D_EXPERT), lambda i, te, nu: (te[i], 0, 0))
    grid_spec = pltpu.PrefetchScalarGridSpec(
        num_scalar_prefetch=2,
        grid=(n_tiles,),
        in_specs=[
            pl.BlockSpec((tm * ROW_CHUNKS, LANES), lambda i, te, nu: (jnp.minimum(i, nu[0] - 1), 0)),
            w_in_spec, w_in_spec,
            pl.BlockSpec((None, D_EXPERT, D_MODEL), lambda i, te, nu: (te[i], 0, 0)),
        ],
        out_specs=pl.BlockSpec((tm, D_MODEL), lambda i, te, nu: (i, 0)),
        scratch_shapes=[pltpu.VMEM((D_MODEL, D_EXPERT), _BF16), pltpu.VMEM((D_MODEL, D_EXPERT), _BF16),
                        pltpu.VMEM((D_EXPERT, D_MODEL), _BF16)],
    )
    return pl.pallas_call(
        functools.partial(_experts_kernel, tm=tm),
        grid_spec=grid_spec,
        out_shape=jax.ShapeDtypeStruct((n_tiles * tm, D_MODEL), _F32),
        compiler_params=_params(("arbitrary",)),
        name="experts",
    )(tile_expert, n_used, xs, w_gate, w_up, w_down)


def _combine_kernel(pos_ref, posn_ref, x1_ref, route_ref, y_hbm, o_ref, r0, r1, sem, *, tm):
    i = pl.program_id(0)
    n = pl.num_programs(0)
    bufs = (r0, r1)

    def issue(pos, slot):
        def body(r, c):
            for k in range(2):
                pltpu.make_async_copy(y_hbm.at[pl.ds(pos[0, k, r], 1), :], bufs[k].at[slot, pl.ds(r, 1), :],
                                      sem.at[slot, k]).start(priority=k)
            return c

        lax.fori_loop(0, tm, body, 0, unroll=DMA_UNROLL)

    @pl.when(i == 0)
    def _():
        issue(pos_ref, 0)

    @pl.when(i + 1 < n)
    def _():
        issue(posn_ref, (i + 1) % 2)

    slot = i % 2
    for k in range(2):
        pltpu.make_async_copy(y_hbm.at[pl.ds(0, tm), :], bufs[k].at[slot], sem.at[slot, k]).wait()
    rec = route_ref[...]
    o_ref[...] = x1_ref[...] + rec[:, 2:3] * r0[slot] + rec[:, 3:4] * r1[slot]


def _combine(pos, x1, route, y_sorted):
    n_tiles, _, tm = pos.shape
    t = x1.shape[0]
    kern = functools.partial(_combine_kernel, tm=tm)
    pos_spec = lambda nxt: pl.BlockSpec(
        (1, 2, tm), lambda i: (jnp.minimum(i + nxt, n_tiles - 1), 0, 0), memory_space=pltpu.SMEM)
    return pl.pallas_call(
        kern,
        grid=(n_tiles,),
        in_specs=[
            pos_spec(0), pos_spec(1),
            pl.BlockSpec((tm, D_MODEL), lambda i: (i, 0)),
            pl.BlockSpec((tm, ROUTE_LANES), lambda i: (i, 0)),
            pl.BlockSpec(memory_space=pl.ANY),
        ],
        out_specs=pl.BlockSpec((tm, D_MODEL), lambda i: (i, 0)),
        scratch_shapes=[pltpu.VMEM((2, tm, D_MODEL), _F32), pltpu.VMEM((2, tm, D_MODEL), _F32),
                        pltpu.SemaphoreType.DMA((2, 2))],
        out_shape=jax.ShapeDtypeStruct((t, D_MODEL), _F32),
        compiler_params=_params(("arbitrary",)),
        name="combine",
    )(pos, pos, x1, route, y_sorted)


def _rope_tables(pos):
    inv = ROPE_THETA ** (-jnp.arange(0, HEAD_DIM, 2, dtype=_F32) / HEAD_DIM)
    ang = pos.astype(_F32)[:, None] * inv[None, :]
    cos = jnp.cos(ang)
    sin = jnp.sin(ang)
    return jnp.tile(cos, (1, 4)), jnp.tile(jnp.concatenate([-sin, sin], axis=1), (1, 2))


def _head_sum_matrices():
    lane_head = jnp.arange(ATTN_WIDTH) // HEAD_DIM
    cols = jnp.arange(LANES)
    esum = (lane_head[:, None] == cols[None, :]).astype(_F32) / HEAD_DIM
    eexp = (cols[:, None] == lane_head[None, :]).astype(_F32)
    return esum.astype(_BF16), jnp.concatenate([eexp, eexp], axis=0).astype(_BF16)


def kernel(x_prompt, x_sample, cache_k, cache_v, state_pool, page_table, norm1_g, w_in, q_norm_g, k_norm_g,
           lambda_q1, lambda_k1, lambda_q2, lambda_k2, subln_g, w_pool, pool_scale, w_out, norm2_g, w_coarse,
           b_coarse, w_fine, b_fine, w_gate, w_up, w_down):
    depth = w_in.shape[0]
    assert depth == 1
    l = 0
    lam_init = 0.8 - 0.6 * math.exp(-0.3 * l)
    bp, sp, _ = x_prompt.shape
    bs, ts, _ = x_sample.shape
    tp, tsamp = bp * sp, bs * ts
    n_pages = page_table.shape[1]
    past = n_pages * PAGE_SIZE
    n_hc = 2 * N_HEADS

    w_in_bf = w_in[l].astype(_BF16)
    w_out_bf = w_out[l].astype(_BF16)
    w_pool_bf = w_pool[l].astype(_BF16)
    g1 = norm1_g[l][None, :]
    g2 = norm2_g[l][None, :]
    gq = jnp.tile(q_norm_g[l], n_hc)[None, :]
    gk = jnp.tile(k_norm_g[l], n_hc)[None, :]
    sg = subln_g[l][None, :]
    ps = pool_scale[l][None, :]
    lam_p = jnp.stack([lambda_q1[l], lambda_k1[l], lambda_q2[l], lambda_k2[l]])
    esum, eexp = _head_sum_matrices()
    pad = ROUTE_LANES - N_EXPERTS - N_EXPERT_GROUPS
    w_fine_flat = jnp.transpose(w_fine[l], (1, 0, 2)).reshape(D_MODEL, N_EXPERTS)
    w_route = jnp.concatenate([w_fine_flat, w_coarse[l], jnp.zeros((D_MODEL, pad), _F32)], axis=1)
    wrh, wrl = _split_bf16(w_route)
    wrl = jnp.concatenate([wrh, wrl], axis=1)
    b_route = jnp.concatenate([b_fine[l].reshape(-1), b_coarse[l], jnp.zeros((pad,), _F32)])[None, :]

    cos_p, sin_p = _rope_tables(jnp.arange(sp, dtype=jnp.int32))
    cos_s, sin_s = _rope_tables(past + jnp.arange(ts, dtype=jnp.int32))
    cos_s, sin_s = jnp.tile(cos_s, (bs, 1)), jnp.tile(sin_s, (bs, 1))
    xp2 = x_prompt.reshape(tp, D_MODEL)
    xs2 = x_sample.reshape(tsamp, D_MODEL)
    qp, kpt, kpt_bf, vp, vp_bf, up = _proj(xp2, g1, w_in_bf, gq, gk, cos_p, sin_p, esum, eexp, PROJ_TILE, sp)
    qs, ks, _, vs, _, us = _proj(xs2, g1, w_in_bf, gq, gk, cos_s, sin_s, esum, eexp, tsamp)

    qrep = qs.astype(_F32).reshape(bs, ts, ATTN_WIDTH)
    new_rows = 8
    kn = jnp.pad(ks.reshape(bs, ts, ATTN_WIDTH), ((0, 0), (0, new_rows - ts), (0, 0)))
    vn = jnp.pad(vs.reshape(bs, ts, ATTN_WIDTH), ((0, 0), (0, new_rows - ts), (0, 0)))
    n_phys = cache_k.shape[1]
    ck = jnp.transpose(cache_k[l], (0, 2, 3, 1)).reshape(n_phys, ATTN_WIDTH, PAGE_SIZE)
    cv = cache_v[l].reshape(n_phys, PAGE_SIZE * N_HEADS, V_HEAD_DIM)
    shp = (bp, sp, ATTN_WIDTH)
    a_p, a_s = _attn(page_table, lam_p, sg, qp.reshape(shp), kpt_bf, vp_bf.reshape(shp), qrep, kn, vn, ck, cv,
                     ts, lam_init)
    a_p = a_p.reshape(tp, ATTN_WIDTH)
    a_s = a_s.reshape(bs, N_HEADS, 2, ts, V_HEAD_DIM)[:, :, 0]
    a_s = a_s.transpose(0, 2, 1, 3).reshape(tsamp, ATTN_WIDTH).astype(_BF16)

    hist_p = jnp.zeros((bp, HALO_ROWS, POOL_WIDTH), _F32)
    seq_rows = 8
    us_pad = jnp.pad(us.reshape(bs, ts, POOL_WIDTH), ((0, 0), (0, seq_rows - ts), (0, 0)))
    hist_s = jnp.pad(state_pool[l], ((0, 0), (HALO_ROWS - POOL_HIST, 0), (0, 0)))
    d_s = _poolwin(us_pad.reshape(bs * seq_rows, POOL_WIDTH), hist_s, seq_rows, 1, past)
    d_s = d_s.reshape(bs, seq_rows, POOL_WIDTH)[:, :ts].reshape(tsamp, POOL_WIDTH)

    counts0 = jnp.zeros((1, ROUTE_LANES), _F32)
    x1_p, h2_p, route_p, rect_p, counts_p = _mix(xp2, a_p, (up, hist_p), w_pool_bf, ps, w_out_bf, g2, wrh, wrl,
                                                 b_route, counts0, MIX_TILE, pool=(sp // MIX_TILE, 0))
    x1_s, h2_s, route_s, rect_s, counts = _mix(xs2, a_s, (d_s,), w_pool_bf, ps, w_out_bf, g2, wrh, wrl, b_route,
                                               counts_p, tsamp)

    tm = EXPERT_TILE
    t_all = tp + tsamp
    n_tiles = (2 * t_all + N_EXPERTS * (tm - 1) + tm - 1) // tm
    cnt = counts[0, :N_EXPERTS].astype(jnp.int32)
    padded = (cnt + tm - 1) // tm * tm
    ends = jnp.cumsum(padded)
    off = ends - padded
    n_used = (ends[-1] // tm).astype(jnp.int32)
    tile_ids = jnp.arange(n_tiles, dtype=jnp.int32)
    te = jnp.sum((ends[None, :] <= (tile_ids * tm)[:, None]).astype(jnp.int32), axis=1)
    te = jnp.minimum(te, N_EXPERTS - 1)
    last = jnp.sum(jnp.where(tile_ids == n_used - 1, te, 0))
    tile_expert = jnp.where(tile_ids < n_used, te, last)

    n_rows = n_tiles * tm
    zstart = off + counts_p[0, :N_EXPERTS].astype(jnp.int32)
    zinfo = (jnp.concatenate([zstart, ends[-1:]]), jnp.concatenate([ends - zstart, (n_tiles - n_used)[None]]))
    assert tsamp <= tm

    def sorted_rows(rect):
        experts = jnp.arange(N_EXPERTS, dtype=jnp.int32)
        base = jnp.sum(jnp.where(rect[:, 0:2, :, None] == experts, off, 0), axis=-1)
        return base + rect[:, 4:6, :]

    pos_p, pos_s = sorted_rows(rect_p), sorted_rows(rect_s)
    xs = _dispatch(*zinfo, pos_p, h2_p, None, n_rows, 2 * tm)
    xs = _dispatch(*zinfo, pos_s, h2_s, xs, n_rows, 0)
    y_sorted = _experts(tile_expert, n_used.reshape(1), xs, w_gate[l], w_up[l], w_down[l], tm)
    y_p = _combine(pos_p, x1_p, route_p, y_sorted)
    y_s = _combine(pos_s, x1_s, route_s, y_sorted)

    new_k_p = jnp.transpose(kpt.reshape(bp, n_hc, HEAD_DIM, sp), (0, 3, 1, 2))
    new_pool_p = up.reshape(bp, sp, POOL_WIDTH)[:, sp - POOL_HIST:]
    new_pool_s = jnp.concatenate([state_pool[l], us.reshape(bs, ts, POOL_WIDTH)], axis=1)[:, -POOL_HIST:]
    return (y_p.reshape(bp, sp, D_MODEL),
            y_s.reshape(bs, ts, D_MODEL),
            new_k_p[None],
            vp.reshape(1, bp, sp, N_HEADS, V_HEAD_DIM),
            new_pool_p[None],
            ks.reshape(1, bs, ts, n_hc, HEAD_DIM),
            vs.reshape(1, bs, ts, N_HEADS, V_HEAD_DIM),
            new_pool_s[None])
```

```python
import functools
import math

import jax
import jax.numpy as jnp
from jax import lax
from jax.experimental import pallas as pl
from jax.experimental.pallas import tpu as pltpu

_F32 = jnp.float32
_BF16 = jnp.bfloat16
_I32 = jnp.int32

D_MODEL = 2048
ATTN_WIDTH = 1024
POOL_WIDTH = 1024
HEAD_DIM = 64
N_HEADS = 8
V_HEAD_DIM = 128
POOL_WINDOWS = (2, 4, 8, 16)
POOL_GROUP_WIDTH = 256
POOL_HIST = 15
HALO_ROWS = 16
N_EXPERTS = 32
EXPERTS_PER_GROUP = 8
N_EXPERT_GROUPS = 4
D_EXPERT = 256
ROPE_THETA = 10000.0
RMS_EPS = 1e-6
NEG_INF = -1e30
PAGE_SIZE = 128
LANES = 128
ROUTE_LANES = 128
REC_ROWS = 8
VMEM_LIMIT = 56 * 1024 * 1024

PROJ_TILE = 512
ATTN_TILE = 512
ATTN_WIDE = 4
MIX_TILE = 512
EXPERT_TILE = 256
QROWS = 64
DMA_UNROLL = 8


def _dot(a, b):
    return jnp.dot(a, b, preferred_element_type=_F32)


def _split_bf16(x):
    hi = x.astype(_BF16)
    lo = (x - hi.astype(_F32)).astype(_BF16)
    return hi, lo


def _params(sem):
    return pltpu.CompilerParams(dimension_semantics=sem, vmem_limit_bytes=VMEM_LIMIT)


ROW_CHUNKS = D_MODEL // LANES


def _load_token_major(ref, n):
    return jnp.concatenate([ref[pl.ds(c, n, stride=ROW_CHUNKS), :] for c in range(ROW_CHUNKS)], axis=1)


def _store_token_major(ref, x):
    n = x.shape[0]
    for c in range(ROW_CHUNKS):
        ref[pl.ds(c, n, stride=ROW_CHUNKS), :] = x[:, c * LANES:(c + 1) * LANES]


def _whole_vmem():
    return pl.BlockSpec(memory_space=pltpu.VMEM)


def _proj_kernel(x_ref, g1_ref, w_ref, gq_ref, gk_ref, cos_ref, sin_ref, esum_ref, eexp_ref,
                 q_ref, kf_ref, kb_ref, vf_ref, vb_ref, u_ref, *, k_transposed):
    x = x_ref[...]
    ms = jnp.mean(x * x, axis=-1, keepdims=True)
    h = (x * lax.rsqrt(ms + RMS_EPS) * g1_ref[...]).astype(_BF16)
    cos = cos_ref[...]
    sin = sin_ref[...]
    lane = lax.broadcasted_iota(_I32, (1, LANES), 1)
    upper = (lane & (HEAD_DIM - 1)) >= HEAD_DIM // 2

    def normed_rope(z, g_ref, outs, transposed):
        msq = _dot((z * z).astype(_BF16), esum_ref[...])
        rh, rl = _split_bf16(lax.rsqrt(msq + RMS_EPS))
        rb = _dot(jnp.concatenate([rh, rl], axis=1), eexp_ref[...])
        n = z * rb * g_ref[...]
        for j in range(ATTN_WIDTH // LANES):
            sl = slice(j * LANES, (j + 1) * LANES)
            nj = n[:, sl]
            swapped = jnp.where(upper, pltpu.roll(nj, HEAD_DIM // 2, 1),
                                pltpu.roll(nj, LANES - HEAD_DIM // 2, 1))
            o = nj * cos + swapped * sin
            if transposed:
                ot = o.T
                for ref, _ in outs:
                    ref[sl, :] = ot.astype(ref.dtype)
            else:
                for ref, scale in outs:
                    ref[:, sl] = (o * scale).astype(ref.dtype)

    a = ATTN_WIDTH
    normed_rope(_dot(h, w_ref[:, 0:a]), gq_ref, ((q_ref, HEAD_DIM ** -0.5),), False)
    normed_rope(_dot(h, w_ref[:, a:2 * a]), gk_ref, ((kf_ref, 1.0), (kb_ref, 1.0)), k_transposed)
    zv = _dot(h, w_ref[:, 2 * a:3 * a])
    vf_ref[...] = zv
    vb_ref[...] = zv.astype(_BF16)
    u_ref[...] = _dot(h, w_ref[:, 3 * a:])


def _proj(x, g1, w_bf, gq, gk, cos, sin, esum, eexp, tm, seq_len=None):
    t = x.shape[0]
    n_pos_tiles = cos.shape[0] // tm
    row = lambda i: (i, 0)
    const = lambda i: (0, 0)
    tok_spec = lambda w: pl.BlockSpec((tm, w), row)
    a = ATTN_WIDTH
    if seq_len is None:
        k_spec, k_shape = tok_spec(a), (t, a)
    else:
        tiles_per_seq = seq_len // tm
        k_spec = pl.BlockSpec((None, a, tm), lambda i: (i // tiles_per_seq, 0, i % tiles_per_seq))
        k_shape = (t // seq_len, a, seq_len)
    return pl.pallas_call(
        functools.partial(_proj_kernel, k_transposed=seq_len is not None),
        grid=(t // tm,),
        in_specs=[
            tok_spec(D_MODEL),
            pl.BlockSpec((1, D_MODEL), const),
            _whole_vmem(),
            pl.BlockSpec((1, a), const),
            pl.BlockSpec((1, a), const),
            pl.BlockSpec((tm, LANES), lambda i: (i % n_pos_tiles, 0)),
            pl.BlockSpec((tm, LANES), lambda i: (i % n_pos_tiles, 0)),
            _whole_vmem(),
            _whole_vmem(),
        ],
        out_specs=[tok_spec(a), k_spec, k_spec, tok_spec(a), tok_spec(a), tok_spec(POOL_WIDTH)],
        out_shape=[
            jax.ShapeDtypeStruct((t, a), _BF16),
            jax.ShapeDtypeStruct(k_shape, _F32),
            jax.ShapeDtypeStruct(k_shape, _BF16),
            jax.ShapeDtypeStruct((t, a), _F32),
            jax.ShapeDtypeStruct((t, a), _BF16),
            jax.ShapeDtypeStruct((t, POOL_WIDTH), _F32),
        ],
        compiler_params=_params(("arbitrary",)),
        name="proj",
    )(x, g1, w_bf, gq, gk, cos, sin, esum, eexp)


def _lambda(lam_ref, lam_init):
    lp = lam_ref[...]
    s1 = jnp.sum(lp[0:1] * lp[1:2], axis=-1, keepdims=True)
    s2 = jnp.sum(lp[2:3] * lp[3:4], axis=-1, keepdims=True)
    return jnp.exp(s1) - jnp.exp(s2) + lam_init


def _sub_norm(o, sg, lam_init):
    ms = jnp.mean(o * o, axis=-1, keepdims=True)
    return o * lax.rsqrt(ms + RMS_EPS) * sg * (1.0 - lam_init)


def _prompt_attention(lam_ref, sg_ref, q_ref, kt_ref, v_ref, o_ref, *, tile, wide, lam_init, side):
    qi = pl.program_id(2)
    q = q_ref[...]
    lane = lax.broadcasted_iota(_I32, (1, LANES), 1)
    zero = jnp.zeros_like(q)
    qc = (jnp.where(lane < HEAD_DIM, q, zero), jnp.where(lane >= HEAD_DIM, q, zero))

    def update(s, state, vblk):
        m, l, acc = state
        m_new = jnp.maximum(m, jnp.max(s, axis=-1, keepdims=True))
        alpha = jnp.exp(m - m_new)
        p = jnp.exp(s - m_new)
        l = alpha * l + jnp.sum(p, axis=-1, keepdims=True)
        acc = alpha * acc + _dot(p.astype(_BF16), vblk)
        return m_new, l, acc

    def block(first, carry, masked, width):
        start = pl.multiple_of(first * tile, tile)
        ktblk = kt_ref[:, pl.ds(start, width)]
        vblk = v_ref[pl.ds(start, width), :]
        out = []
        for c in range(2):
            s = _dot(qc[c], ktblk)
            if masked:
                r = lax.broadcasted_iota(_I32, (tile, width), 0)
                col = lax.broadcasted_iota(_I32, (tile, width), 1)
                s = jnp.where(col <= r + (width - tile), s, NEG_INF)
            out.append(update(s, carry[c], vblk))
        return tuple(out)

    init_one = (jnp.full((tile, 1), NEG_INF, _F32), jnp.zeros((tile, 1), _F32),
                jnp.zeros((tile, V_HEAD_DIM), _F32))
    n_full = qi // wide
    n_loop = jnp.maximum(n_full - 1, 0)
    carry = lax.fori_loop(0, n_loop, lambda j, c: block(j * wide, c, False, wide * tile), (init_one, init_one))
    lam = _lambda(lam_ref, lam_init)
    for last, has_full in [(w, f) for w in range(1, wide + 1) for f in (False, True)]:
        @pl.when((qi + 1 - n_full * wide == last) & ((n_full > 0) == has_full))
        def _(last=last, has_full=has_full):
            side_finish = side()
            c = block(n_loop * wide, carry, False, wide * tile) if has_full else carry
            (_, l0, a0), (_, l1, a1) = block(n_full * wide, c, True, last * tile)
            o = a0 / l0 - lam * (a1 / l1)
            o_ref[...] = _sub_norm(o, sg_ref[...], lam_init).astype(o_ref.dtype)
            side_finish()


def _attn_kernel(pt_ref, lam_ref, sg_ref, q_ref, kt_ref, v_ref, qs_ref, kn_ref, vn_ref, spread_ref, ck_hbm, cv_hbm,
                 o_ref, os_ref, kbuf, vbuf, sem, qbd_ref, m_ref, l_ref, acc_ref, *, tile, wide, lam_init, n_tok,
                 n_steps, pages_per_step, steps_per_elem):
    npg = pages_per_step
    step = (pl.program_id(0) * pl.num_programs(1) + pl.program_id(1)) * pl.num_programs(2) + pl.program_id(2)
    assert steps_per_elem & (steps_per_elem - 1) == 0
    part = step & (steps_per_elem - 1)

    def page_copies(s):
        elem = lax.shift_right_logical(s, steps_per_elem.bit_length() - 1)
        first_page = (s & (steps_per_elem - 1)) * npg
        base = (s & 1) * npg
        out = []
        for p in range(npg):
            page = pt_ref[elem, first_page + p]
            out.append(pltpu.make_async_copy(ck_hbm.at[page], kbuf.at[base + p], sem.at[0, base + p]))
            out.append(pltpu.make_async_copy(cv_hbm.at[page], vbuf.at[base + p], sem.at[1, base + p]))
        return out

    @pl.when(step == 0)
    def _():
        for cp in page_copies(step):
            cp.start()

    @pl.when(step + 1 < n_steps)
    def _():
        for cp in page_copies(step + 1):
            cp.start()

    def sample_pages():
        return _sample_pages_step(page_copies(step), (step & 1) * npg, part, lam_ref, sg_ref, qs_ref, kn_ref,
                                  vn_ref, spread_ref, os_ref, kbuf, vbuf, qbd_ref, m_ref, l_ref, acc_ref, npg=npg,
                                  n_tok=n_tok,
                                  lam_init=lam_init, last_part=steps_per_elem - 1)

    _prompt_attention(lam_ref, sg_ref, q_ref, kt_ref, v_ref, o_ref, tile=tile, wide=wide, lam_init=lam_init,
                      side=sample_pages)


def _sample_pages_step(copies, base, part, lam_ref, sg_ref, q_ref, kn_ref, vn_ref, spread_ref, o_ref, kbuf, vbuf,
                       qbd_ref, m_ref, l_ref, acc_ref, *, npg, n_tok, lam_init, last_part):
    for cp in copies:
        cp.wait()
    k_refs = [kbuf.at[base + p] for p in range(npg)]
    v_refs = [vbuf.at[base + p] for p in range(npg)]
    rows_per_head = 2 * n_tok
    head_rows = lambda h: slice(h * rows_per_head, (h + 1) * rows_per_head)

    @pl.when(part == 0)
    def _():
        q_tok = q_ref[...]
        lane_hc = lax.broadcasted_iota(_I32, (1, ATTN_WIDTH), 1) >> (HEAD_DIM.bit_length() - 1)
        for hc in range(2 * N_HEADS):
            qbd_ref[hc * n_tok:(hc + 1) * n_tok, :] = jnp.where(lane_hc == hc, q_tok, 0.0)
        q = qbd_ref[...]
        kn = kn_ref[...]
        vn = vn_ref[...]
        tok = lax.broadcasted_iota(_I32, (QROWS, 1), 0) & (n_tok - 1)
        ss = [jnp.where(tok >= jn, jnp.sum(q * kn[jn:jn + 1, :], axis=-1, keepdims=True), NEG_INF)
              for jn in range(n_tok)]
        m = functools.reduce(jnp.maximum, ss)
        ps = [jnp.exp(s - m) for s in ss]
        m_ref[...] = m
        l_ref[...] = functools.reduce(jnp.add, ps)
        for h in range(N_HEADS):
            lanes = slice(h * V_HEAD_DIM, (h + 1) * V_HEAD_DIM)
            acc_ref[head_rows(h), :] = functools.reduce(
                jnp.add, [ps[jn][head_rows(h)] * vn[jn:jn + 1, lanes] for jn in range(n_tok)])

    assert npg % 2 == 0
    q = qbd_ref[...]
    ss = []
    for a in range(0, npg, 2):
        pair = _dot(q, jnp.concatenate([k_refs[a][...], k_refs[a + 1][...]], axis=1))
        ss += [pair[:, :PAGE_SIZE], pair[:, PAGE_SIZE:]]
    m_old = m_ref[...]
    m_new = m_old
    for s in ss:
        m_new = jnp.maximum(m_new, jnp.max(s, axis=-1, keepdims=True))
    alpha = jnp.exp(m_old - m_new)
    ps = [jnp.exp(s - m_new) for s in ss]
    l = alpha * l_ref[...] + functools.reduce(jnp.add, [jnp.sum(p, axis=-1, keepdims=True) for p in ps])
    m_ref[...] = m_new
    l_ref[...] = l
    spread = _dot(jnp.concatenate(ps, axis=0).astype(_BF16), spread_ref[...])
    row_head = lax.broadcasted_iota(_I32, (QROWS, 1), 0) >> (rows_per_head.bit_length() - 1)
    own_head = (lax.broadcasted_iota(_I32, (1, PAGE_SIZE * N_HEADS), 1) & (N_HEADS - 1)) == row_head
    pv = functools.reduce(jnp.add, [
        _dot(jnp.where(own_head, spread[p * QROWS:(p + 1) * QROWS], 0.0).astype(_BF16), vr[...].astype(_BF16))
        for p, vr in enumerate(v_refs)])
    acc_ref[...] = alpha * acc_ref[...] + pv

    def finish():
        @pl.when(part == last_part)
        def _():
            o = acc_ref[...] / l_ref[...]
            lam = _lambda(lam_ref, lam_init)
            d = o - lam * pltpu.roll(o, QROWS - n_tok, 0)
            o_ref[...] = _sub_norm(d, sg_ref[...], lam_init)

    return finish


def _attn(page_table, lam_p, sg, q, kt, v, qrep, kn, vn, cache_k, cache_v, n_tok, lam_init):
    b, s, _ = q.shape
    tile = ATTN_TILE
    nq = s // tile
    n_steps = b * N_HEADS * nq
    nb, n_pages = page_table.shape
    assert (nb * n_pages) % n_steps == 0 and n_steps % nb == 0
    npg = nb * n_pages // n_steps
    steps_per_elem = n_steps // nb
    width = ATTN_WIDTH
    assert QROWS == 2 * N_HEADS * n_tok and cache_k.shape[1:] == (width, PAGE_SIZE)
    assert cache_v.shape[1:] == (PAGE_SIZE * N_HEADS, V_HEAD_DIM)

    spread = (jnp.arange(PAGE_SIZE)[:, None] == jnp.arange(PAGE_SIZE * N_HEADS)[None, :] // N_HEADS).astype(_BF16)
    elem = lambda bi, h, i: ((bi * N_HEADS + h) * nq + i) // steps_per_elem
    per_elem = lambda rows, cols: pl.BlockSpec((None, rows, cols), lambda bi, h, i, pt: (elem(bi, h, i), 0, 0))
    kern = functools.partial(_attn_kernel, tile=tile, wide=ATTN_WIDE, lam_init=lam_init, n_tok=n_tok,
                             n_steps=n_steps, pages_per_step=npg, steps_per_elem=steps_per_elem)
    grid_spec = pltpu.PrefetchScalarGridSpec(
        num_scalar_prefetch=1,
        grid=(b, N_HEADS, nq),
        in_specs=[
            pl.BlockSpec((4, HEAD_DIM), lambda bi, h, i, pt: (0, 0)),
            pl.BlockSpec((1, V_HEAD_DIM), lambda bi, h, i, pt: (0, 0)),
            pl.BlockSpec((None, tile, LANES), lambda bi, h, i, pt: (bi, i, h)),
            pl.BlockSpec((None, LANES, s), lambda bi, h, i, pt: (bi, h, 0)),
            pl.BlockSpec((None, s, LANES), lambda bi, h, i, pt: (bi, 0, h)),
            per_elem(n_tok, width), per_elem(kn.shape[1], width), per_elem(vn.shape[1], width),
            _whole_vmem(),
            pl.BlockSpec(memory_space=pl.ANY), pl.BlockSpec(memory_space=pl.ANY),
        ],
        out_specs=[pl.BlockSpec((None, tile, LANES), lambda bi, h, i, pt: (bi, i, h)),
                   per_elem(QROWS, V_HEAD_DIM)],
        scratch_shapes=[pltpu.VMEM((2 * npg, width, PAGE_SIZE), _F32),
                        pltpu.VMEM((2 * npg, PAGE_SIZE * N_HEADS, V_HEAD_DIM), _F32),
                        pltpu.SemaphoreType.DMA((2, 2 * npg)),
                        pltpu.VMEM((QROWS, width), _F32),
                        pltpu.VMEM((QROWS, 1), _F32), pltpu.VMEM((QROWS, 1), _F32),
                        pltpu.VMEM((QROWS, V_HEAD_DIM), _F32)],
    )
    return pl.pallas_call(
        kern,
        grid_spec=grid_spec,
        out_shape=[jax.ShapeDtypeStruct((b, s, ATTN_WIDTH), _BF16),
                   jax.ShapeDtypeStruct((nb, QROWS, V_HEAD_DIM), _F32)],
        compiler_params=_params(("arbitrary", "arbitrary", "arbitrary")),
        name="attn",
    )(page_table, lam_p, sg, q, kt, v, qrep, kn, vn, spread, cache_k, cache_v)


def _pool_windows(u_ref, halo_ref, hist_ref, *, tm, tiles_per_seq, start_pos):
    i = pl.program_id(0)
    t_in_seq = i % tiles_per_seq
    u = u_ref[...]
    halo = jnp.where(t_in_seq == 0, hist_ref[...], halo_ref[...])
    ext = jnp.concatenate([halo, u], axis=0)
    pos = start_pos + t_in_seq * tm + lax.broadcasted_iota(_I32, (tm, 1), 0)
    out = []
    for g, w in enumerate(POOL_WINDOWS):
        sl = slice(g * POOL_GROUP_WIDTH, (g + 1) * POOL_GROUP_WIDTH)
        acc = ext[:, sl]
        span = 1
        while span < w:
            acc = acc + pltpu.roll(acc, span, 0)
            span *= 2
        cnt = jnp.minimum(pos + 1, w).astype(_F32)
        out.append((acc[HALO_ROWS:] / cnt - u[:, sl]).astype(_BF16))
    return out


def _pool_specs(tm, tiles_per_seq):
    halo_blocks = max(tm // HALO_ROWS, 1)
    return [
        pl.BlockSpec((tm, POOL_WIDTH), lambda i: (i, 0)),
        pl.BlockSpec((HALO_ROWS, POOL_WIDTH),
                     lambda i: (jnp.where(i % tiles_per_seq == 0, 0, i * halo_blocks - 1), 0)),
        pl.BlockSpec((None, HALO_ROWS, POOL_WIDTH), lambda i: (i // tiles_per_seq, 0, 0)),
    ]


def _poolwin_kernel(u_ref, halo_ref, hist_ref, d_ref, **pool):
    gw = POOL_GROUP_WIDTH
    for g, d in enumerate(_pool_windows(u_ref, halo_ref, hist_ref, **pool)):
        d_ref[:, g * gw:(g + 1) * gw] = d


def _poolwin(u, hist, tm, tiles_per_seq, start_pos):
    t = u.shape[0]
    kern = functools.partial(_poolwin_kernel, tm=tm, tiles_per_seq=tiles_per_seq, start_pos=start_pos)
    return pl.pallas_call(
        kern,
        grid=(t // tm,),
        in_specs=_pool_specs(tm, tiles_per_seq),
        out_specs=pl.BlockSpec((tm, POOL_WIDTH), lambda i: (i, 0)),
        out_shape=jax.ShapeDtypeStruct((t, POOL_WIDTH), _BF16),
        compiler_params=_params(("arbitrary",)),
        name="poolwin",
    )(u, u, hist)


def _mix_kernel(x_ref, a_ref, *refs, tm, pool):
    n_pool = 1 if pool is None else 3
    (wp_ref, ps_ref, wo_ref, g2_ref, wrh_ref, wrl_ref, br_ref, cin_ref,
     x1_ref, h2_ref, route_ref, rect_ref, cout_ref, carry_ref) = refs[n_pool:]
    i = pl.program_id(0)

    @pl.when(i == 0)
    def _():
        carry_ref[...] = cin_ref[...]

    gw = POOL_GROUP_WIDTH
    if pool is None:
        d = refs[0][...]
        ds = [d[:, g * gw:(g + 1) * gw] for g in range(len(POOL_WINDOWS))]
    else:
        ds = _pool_windows(*refs[:3], tm=tm, tiles_per_seq=pool[0], start_pos=pool[1])
    py = jnp.concatenate([_dot(ds[g], wp_ref[g]) for g in range(len(POOL_WINDOWS))], axis=1) * ps_ref[...]
    x1 = (x_ref[...] + _dot(a_ref[...], wo_ref[0:ATTN_WIDTH, :])
          + _dot(py.astype(_BF16), wo_ref[ATTN_WIDTH:, :]))
    x1_ref[...] = x1
    ms = jnp.mean(x1 * x1, axis=-1, keepdims=True)
    h2 = x1 * lax.rsqrt(ms + RMS_EPS) * g2_ref[...]
    _store_token_major(h2_ref, h2)

    hh, hl = _split_bf16(h2)
    both = _dot(hh, wrl_ref[...])
    logits = both[:, :ROUTE_LANES] + both[:, ROUTE_LANES:] + _dot(hl, wrh_ref[...]) + br_ref[...]
    lane = lax.broadcasted_iota(_I32, (1, ROUTE_LANES), 1)
    big = jnp.int32(ROUTE_LANES)
    is_coarse = (lane >= N_EXPERTS) & (lane < N_EXPERTS + N_EXPERT_GROUPS)
    lc = jnp.where(is_coarse, logits, NEG_INF)
    mc = jnp.max(lc, axis=-1, keepdims=True)
    g_idx = jnp.min(jnp.where(lc == mc, lane, big), axis=-1, keepdims=True) - N_EXPERTS
    p_g = 1.0 / jnp.sum(jnp.exp(lc - mc), axis=-1, keepdims=True)
    in_group = (lane < N_EXPERTS) & ((lane >> (EXPERTS_PER_GROUP.bit_length() - 1)) == g_idx)
    lf = jnp.where(in_group, logits, NEG_INF)
    v1 = jnp.max(lf, axis=-1, keepdims=True)
    i1 = jnp.min(jnp.where(lf == v1, lane, big), axis=-1, keepdims=True)
    lf2 = jnp.where(lane == i1, NEG_INF, lf)
    v2 = jnp.max(lf2, axis=-1, keepdims=True)
    i2 = jnp.min(jnp.where(lf2 == v2, lane, big), axis=-1, keepdims=True)
    e21 = jnp.exp(v2 - v1)
    w1 = p_g / (1.0 + e21)
    w2 = p_g * e21 / (1.0 + e21)

    oh1 = lane == i1
    oh2 = lane == i2
    onehot = (oh1 | oh2).astype(_F32)
    r = lax.broadcasted_iota(_I32, (tm, tm), 0)
    col = lax.broadcasted_iota(_I32, (tm, tm), 1)
    lower = (col < r).astype(_BF16)
    before = _dot(lower, onehot.astype(_BF16)) + carry_ref[...]
    rank1 = jnp.sum(jnp.where(oh1, before, 0.0), axis=-1, keepdims=True)
    rank2 = jnp.sum(jnp.where(oh2, before, 0.0), axis=-1, keepdims=True)
    carry = carry_ref[...] + jnp.sum(onehot, axis=0, keepdims=True)
    carry_ref[...] = carry
    cout_ref[...] = carry

    rec = jnp.zeros((tm, ROUTE_LANES), _F32)
    for k, val in enumerate((i1.astype(_F32), i2.astype(_F32), w1, w2, rank1, rank2)):
        rec = jnp.where(lane == k, val, rec)
    route_ref[...] = rec
    rect_ref[...] = rec.T[0:REC_ROWS, :].astype(_I32)


def _mix(x, a, pool_in, wp_bf, ps, wo_bf, g2, wrh, wrl, br, counts_in, tm, pool=None):
    t = x.shape[0]
    row = lambda i: (i, 0)
    const = lambda i: (0, 0)
    kern = functools.partial(_mix_kernel, tm=tm, pool=pool)
    if pool is None:
        pool_specs, pool_args = [pl.BlockSpec((tm, POOL_WIDTH), row)], list(pool_in)
    else:
        pool_specs, pool_args = _pool_specs(tm, pool[0]), [pool_in[0], pool_in[0], pool_in[1]]
    return pl.pallas_call(
        kern,
        grid=(t // tm,),
        in_specs=[
            pl.BlockSpec((tm, D_MODEL), row),
            pl.BlockSpec((tm, ATTN_WIDTH), row),
            *pool_specs,
            _whole_vmem(),
            pl.BlockSpec((1, POOL_WIDTH), const),
            _whole_vmem(),
            pl.BlockSpec((1, D_MODEL), const),
            _whole_vmem(),
            _whole_vmem(),
            pl.BlockSpec((1, ROUTE_LANES), const),
            pl.BlockSpec((1, ROUTE_LANES), const),
        ],
        out_specs=[
            pl.BlockSpec((tm, D_MODEL), row),
            pl.BlockSpec((tm * ROW_CHUNKS, LANES), row),
            pl.BlockSpec((tm, ROUTE_LANES), row),
            pl.BlockSpec((None, REC_ROWS, tm), lambda i: (i, 0, 0)),
            pl.BlockSpec((1, ROUTE_LANES), const),
        ],
        out_shape=[
            jax.ShapeDtypeStruct((t, D_MODEL), _F32),
            jax.ShapeDtypeStruct((t * ROW_CHUNKS, LANES), _F32),
            jax.ShapeDtypeStruct((t, ROUTE_LANES), _F32),
            jax.ShapeDtypeStruct((t // tm, REC_ROWS, tm), _I32),
            jax.ShapeDtypeStruct((1, ROUTE_LANES), _F32),
        ],
        scratch_shapes=[pltpu.VMEM((1, ROUTE_LANES), _F32)],
        compiler_params=_params(("arbitrary",)),
        name="mix",
    )(x, a, *pool_args, wp_bf, ps, wo_bf, g2, wrh, wrl, br, counts_in)


def _tokens(ref, first, n):
    return ref.at[pl.ds(pl.multiple_of(first * ROW_CHUNKS, ROW_CHUNKS), n * ROW_CHUNKS), :]


def _row_copy(src, dst, src_tok, dst_tok, sem):
    return pltpu.make_async_copy(_tokens(src, src_tok, 1), _tokens(dst, dst_tok, 1), sem)


def _dispatch_kernel(zstart_ref, zpad_ref, pos_ref, h_ref, *rest, tm, zero_rows):
    if zero_rows:
        xs_ref, sem, zsem, zbuf = rest
    else:
        _, xs_ref, sem = rest
    i = pl.program_id(0)

    if zero_rows:
        def zero_fill(wait):
            def go(cp):
                cp.wait() if wait else cp.start()

            def body(e, c):
                cursor = zstart_ref[e]
                pad = zpad_ref[e]
                run = zero_rows // 2
                while run >= 1:
                    @pl.when((pad & run) != 0)
                    def _(cursor=cursor, run=run):
                        go(pltpu.make_async_copy(_tokens(zbuf, 0, run), _tokens(xs_ref, cursor, run), zsem))

                    cursor = cursor + (pad & run)
                    run //= 2
                return c

            lax.fori_loop(0, N_EXPERTS, body, 0)

            def tile(t, c):
                rows = zero_rows // 2
                go(pltpu.make_async_copy(_tokens(zbuf, 0, rows),
                                         _tokens(xs_ref, zstart_ref[N_EXPERTS] + t * rows, rows), zsem))
                return c

            lax.fori_loop(0, zpad_ref[N_EXPERTS], tile, 0)

        @pl.when(i == 0)
        def _():
            zbuf[...] = jnp.zeros_like(zbuf)
            zero_fill(False)
            zero_fill(True)

    def body(r, c):
        for k in range(2):
            _row_copy(h_ref, xs_ref, r, pos_ref[0, k, r], sem).start(priority=k)
        return c

    lax.fori_loop(0, tm, body, 0, unroll=DMA_UNROLL)
    for _ in range(2):
        pltpu.make_async_copy(h_ref, _tokens(xs_ref, 0, tm), sem).wait()


def _dispatch(zstart, zpad, pos, h2, xs, n_rows, zero_rows):
    n_tiles, _, tm = pos.shape
    kern = functools.partial(_dispatch_kernel, tm=tm, zero_rows=zero_rows)
    in_specs = [
        pl.BlockSpec((1, 2, tm), lambda i, z, p: (i, 0, 0), memory_space=pltpu.SMEM),
        pl.BlockSpec((tm * ROW_CHUNKS, LANES), lambda i, z, p: (i, 0)),
    ]
    scratch = [pltpu.SemaphoreType.DMA(())]
    args = [zstart, zpad, pos, h2]
    aliases = {}
    if zero_rows:
        scratch += [pltpu.SemaphoreType.DMA(()), pltpu.VMEM((zero_rows // 2 * ROW_CHUNKS, LANES), _F32)]
    else:
        in_specs.append(pl.BlockSpec(memory_space=pl.ANY))
        args.append(xs)
        aliases = {4: 0}
    grid_spec = pltpu.PrefetchScalarGridSpec(
        num_scalar_prefetch=2, grid=(n_tiles,), in_specs=in_specs,
        out_specs=pl.BlockSpec(memory_space=pl.ANY), scratch_shapes=scratch)
    return pl.pallas_call(
        kern,
        grid_spec=grid_spec,
        out_shape=jax.ShapeDtypeStruct((n_rows * ROW_CHUNKS, LANES), _F32),
        input_output_aliases=aliases,
        compiler_params=_params(("arbitrary",)),
        name="dispatch",
    )(*args)


def _experts_kernel(texp_ref, nused_ref, x_ref, wg_ref, wu_ref, wd_ref, y_ref, wgb, wub, wdb, *, tm):
    i = pl.program_id(0)
    new_expert = (i == 0) | (texp_ref[i] != texp_ref[jnp.maximum(i - 1, 0)])

    @pl.when(new_expert)
    def _():
        wgb[...] = wg_ref[...].astype(_BF16)
        wub[...] = wu_ref[...].astype(_BF16)
        wdb[...] = wd_ref[...].astype(_BF16)

    @pl.when(i < nused_ref[0])
    def _():
        x = _load_token_major(x_ref, tm).astype(_BF16)
        hg = _dot(x, wgb[...])
        hu = _dot(x, wub[...])
        act = (hg * jax.nn.sigmoid(hg) * hu).astype(_BF16)
        y_ref[...] = _dot(act, wdb[...])

    @pl.when(i >= nused_ref[0])
    def _():
        y_ref[...] = jnp.zeros_like(y_ref)


def _experts(tile_expert, n_used, xs, w_gate, w_up, w_down, tm):
    n_tiles = tile_expert.shape[0]
    w_in_spec = pl.BlockSpec((None, D_MODEL, D_EXPERT), lambda i, te, nu: (te[i], 0, 0))
    grid_spec = pltpu.PrefetchScalarGridSpec(
        num_scalar_prefetch=2,
        grid=(n_tiles,),
        in_specs=[
            pl.BlockSpec((tm * ROW_CHUNKS, LANES), lambda i, te, nu: (jnp.minimum(i, nu[0] - 1), 0)),
            w_in_spec, w_in_spec,
            pl.BlockSpec((None, D_EXPERT, D_MODEL), lambda i, te, nu: (te[i], 0, 0)),
        ],
        out_specs=pl.BlockSpec((tm, D_MODEL), lambda i, te, nu: (i, 0)),
        scratch_shapes=[pltpu.VMEM((D_MODEL, D_EXPERT), _BF16), pltpu.VMEM((D_MODEL, D_EXPERT), _BF16),
                        pltpu.VMEM((D_EXPERT, D_MODEL), _BF16)],
    )
    return pl.pallas_call(
        functools.partial(_experts_kernel, tm=tm),
        grid_spec=grid_spec,
        out_shape=jax.ShapeDtypeStruct((n_tiles * tm, D_MODEL), _F32),
        compiler_params=_params(("arbitrary",)),
        name="experts",
    )(tile_expert, n_used, xs, w_gate, w_up, w_down)


def _combine_kernel(pos_ref, posn_ref, x1_ref, route_ref, y_hbm, o_ref, r0, r1, sem, *, tm):
    i = pl.program_id(0)
    n = pl.num_programs(0)
    bufs = (r0, r1)

    def issue(pos, slot):
        def body(r, c):
            for k in range(2):
                pltpu.make_async_copy(y_hbm.at[pl.ds(pos[0, k, r], 1), :], bufs[k].at[slot, pl.ds(r, 1), :],
                                      sem.at[slot, k]).start(priority=k)
            return c

        lax.fori_loop(0, tm, body, 0, unroll=DMA_UNROLL)

    @pl.when(i == 0)
    def _():
        issue(pos_ref, 0)

    @pl.when(i + 1 < n)
    def _():
        issue(posn_ref, (i + 1) % 2)

    slot = i % 2
    for k in range(2):
        pltpu.make_async_copy(y_hbm.at[pl.ds(0, tm), :], bufs[k].at[slot], sem.at[slot, k]).wait()
    rec = route_ref[...]
    o_ref[...] = x1_ref[...] + rec[:, 2:3] * r0[slot] + rec[:, 3:4] * r1[slot]


def _combine(pos, x1, route, y_sorted):
    n_tiles, _, tm = pos.shape
    t = x1.shape[0]
    kern = functools.partial(_combine_kernel, tm=tm)
    pos_spec = lambda nxt: pl.BlockSpec(
        (1, 2, tm), lambda i: (jnp.minimum(i + nxt, n_tiles - 1), 0, 0), memory_space=pltpu.SMEM)
    return pl.pallas_call(
        kern,
        grid=(n_tiles,),
        in_specs=[
            pos_spec(0), pos_spec(1),
            pl.BlockSpec((tm, D_MODEL), lambda i: (i, 0)),
            pl.BlockSpec((tm, ROUTE_LANES), lambda i: (i, 0)),
            pl.BlockSpec(memory_space=pl.ANY),
        ],
        out_specs=pl.BlockSpec((tm, D_MODEL), lambda i: (i, 0)),
        scratch_shapes=[pltpu.VMEM((2, tm, D_MODEL), _F32), pltpu.VMEM((2, tm, D_MODEL), _F32),
                        pltpu.SemaphoreType.DMA((2, 2))],
        out_shape=jax.ShapeDtypeStruct((t, D_MODEL), _F32),
        compiler_params=_params(("arbitrary",)),
        name="combine",
    )(pos, pos, x1, route, y_sorted)


def _rope_tables(pos):
    inv = ROPE_THETA ** (-jnp.arange(0, HEAD_DIM, 2, dtype=_F32) / HEAD_DIM)
    ang = pos.astype(_F32)[:, None] * inv[None, :]
    cos = jnp.cos(ang)
    sin = jnp.sin(ang)
    return jnp.tile(cos, (1, 4)), jnp.tile(jnp.concatenate([-sin, sin], axis=1), (1, 2))


def _head_sum_matrices():
    lane_head = jnp.arange(ATTN_WIDTH) // HEAD_DIM
    cols = jnp.arange(LANES)
    esum = (lane_head[:, None] == cols[None, :]).astype(_F32) / HEAD_DIM
    eexp = (cols[:, None] == lane_head[None, :]).astype(_F32)
    return esum.astype(_BF16), jnp.concatenate([eexp, eexp], axis=0).astype(_BF16)


def kernel(x_prompt, x_sample, cache_k, cache_v, state_pool, page_table, norm1_g, w_in, q_norm_g, k_norm_g,
           lambda_q1, lambda_k1, lambda_q2, lambda_k2, subln_g, w_pool, pool_scale, w_out, norm2_g, w_coarse,
           b_coarse, w_fine, b_fine, w_gate, w_up, w_down):
    depth = w_in.shape[0]
    assert depth == 1
    l = 0
    lam_init = 0.8 - 0.6 * math.exp(-0.3 * l)
    bp, sp, _ = x_prompt.shape
    bs, ts, _ = x_sample.shape
    tp, tsamp = bp * sp, bs * ts
    n_pages = page_table.shape[1]
    past = n_pages * PAGE_SIZE
    n_hc = 2 * N_HEADS

    w_in_bf = w_in[l].astype(_BF16)
    w_out_bf = w_out[l].astype(_BF16)
    w_pool_bf = w_pool[l].astype(_BF16)
    g1 = norm1_g[l][None, :]
    g2 = norm2_g[l][None, :]
    gq = jnp.tile(q_norm_g[l], n_hc)[None, :]
    gk = jnp.tile(k_norm_g[l], n_hc)[None, :]
    sg = subln_g[l][None, :]
    ps = pool_scale[l][None, :]
    lam_p = jnp.stack([lambda_q1[l], lambda_k1[l], lambda_q2[l], lambda_k2[l]])
    esum, eexp = _head_sum_matrices()
    pad = ROUTE_LANES - N_EXPERTS - N_EXPERT_GROUPS
    w_fine_flat = jnp.transpose(w_fine[l], (1, 0, 2)).reshape(D_MODEL, N_EXPERTS)
    w_route = jnp.concatenate([w_fine_flat, w_coarse[l], jnp.zeros((D_MODEL, pad), _F32)], axis=1)
    wrh, wrl = _split_bf16(w_route)
    wrl = jnp.concatenate([wrh, wrl], axis=1)
    b_route = jnp.concatenate([b_fine[l].reshape(-1), b_coarse[l], jnp.zeros((pad,), _F32)])[None, :]

    cos_p, sin_p = _rope_tables(jnp.arange(sp, dtype=jnp.int32))
    cos_s, sin_s = _rope_tables(past + jnp.arange(ts, dtype=jnp.int32))
    cos_s, sin_s = jnp.tile(cos_s, (bs, 1)), jnp.tile(sin_s, (bs, 1))
    xp2 = x_prompt.reshape(tp, D_MODEL)
    xs2 = x_sample.reshape(tsamp, D_MODEL)
    qp, kpt, kpt_bf, vp, vp_bf, up = _proj(xp2, g1, w_in_bf, gq, gk, cos_p, sin_p, esum, eexp, PROJ_TILE, sp)
    qs, ks, _, vs, _, us = _proj(xs2, g1, w_in_bf, gq, gk, cos_s, sin_s, esum, eexp, tsamp)

    qrep = qs.astype(_F32).reshape(bs, ts, ATTN_WIDTH)
    new_rows = 8
    kn = jnp.pad(ks.reshape(bs, ts, ATTN_WIDTH), ((0, 0), (0, new_rows - ts), (0, 0)))
    vn = jnp.pad(vs.reshape(bs, ts, ATTN_WIDTH), ((0, 0), (0, new_rows - ts), (0, 0)))
    n_phys = cache_k.shape[1]
    ck = jnp.transpose(cache_k[l], (0, 2, 3, 1)).reshape(n_phys, ATTN_WIDTH, PAGE_SIZE)
    cv = cache_v[l].reshape(n_phys, PAGE_SIZE * N_HEADS, V_HEAD_DIM)
    shp = (bp, sp, ATTN_WIDTH)
    a_p, a_s = _attn(page_table, lam_p, sg, qp.reshape(shp), kpt_bf, vp_bf.reshape(shp), qrep, kn, vn, ck, cv,
                     ts, lam_init)
    a_p = a_p.reshape(tp, ATTN_WIDTH)
    a_s = a_s.reshape(bs, N_HEADS, 2, ts, V_HEAD_DIM)[:, :, 0]
    a_s = a_s.transpose(0, 2, 1, 3).reshape(tsamp, ATTN_WIDTH).astype(_BF16)

    hist_p = jnp.zeros((bp, HALO_ROWS, POOL_WIDTH), _F32)
    seq_rows = 8
    us_pad = jnp.pad(us.reshape(bs, ts, POOL_WIDTH), ((0, 0), (0, seq_rows - ts), (0, 0)))
    hist_s = jnp.pad(state_pool[l], ((0, 0), (HALO_ROWS - POOL_HIST, 0), (0, 0)))
    d_s = _poolwin(us_pad.reshape(bs * seq_rows, POOL_WIDTH), hist_s, seq_rows, 1, past)
    d_s = d_s.reshape(bs, seq_rows, POOL_WIDTH)[:, :ts].reshape(tsamp, POOL_WIDTH)

    counts0 = jnp.zeros((1, ROUTE_LANES), _F32)
    x1_p, h2_p, route_p, rect_p, counts_p = _mix(xp2, a_p, (up, hist_p), w_pool_bf, ps, w_out_bf, g2, wrh, wrl,
                                                 b_route, counts0, MIX_TILE, pool=(sp // MIX_TILE, 0))
    x1_s, h2_s, route_s, rect_s, counts = _mix(xs2, a_s, (d_s,), w_pool_bf, ps, w_out_bf, g2, wrh, wrl, b_route,
                                               counts_p, tsamp)

    tm = EXPERT_TILE
    t_all = tp + tsamp
    n_tiles = (2 * t_all + N_EXPERTS * (tm - 1) + tm - 1) // tm
    cnt = counts[0, :N_EXPERTS].astype(jnp.int32)
    padded = (cnt + tm - 1) // tm * tm
    ends = jnp.cumsum(padded)
    off = ends - padded
    n_used = (ends[-1] // tm).astype(jnp.int32)
    tile_ids = jnp.arange(n_tiles, dtype=jnp.int32)
    te = jnp.sum((ends[None, :] <= (tile_ids * tm)[:, None]).astype(jnp.int32), axis=1)
    te = jnp.minimum(te, N_EXPERTS - 1)
    last = jnp.sum(jnp.where(tile_ids == n_used - 1, te, 0))
    tile_expert = jnp.where(tile_ids < n_used, te, last)

    n_rows = n_tiles * tm
    zstart = off + counts_p[0, :N_EXPERTS].astype(jnp.int32)
    zinfo = (jnp.concatenate([zstart, ends[-1:]]), jnp.concatenate([ends - zstart, (n_tiles - n_used)[None]]))
    assert tsamp <= tm

    def sorted_rows(rect):
        experts = jnp.arange(N_EXPERTS, dtype=jnp.int32)
        base = jnp.sum(jnp.where(rect[:, 0:2, :, None] == experts, off, 0), axis=-1)
        return base + rect[:, 4:6, :]

    pos_p, pos_s = sorted_rows(rect_p), sorted_rows(rect_s)
    xs = _dispatch(*zinfo, pos_p, h2_p, None, n_rows, 2 * tm)
    xs = _dispatch(*zinfo, pos_s, h2_s, xs, n_rows, 0)
    y_sorted = _experts(tile_expert, n_used.reshape(1), xs, w_gate[l], w_up[l], w_down[l], tm)
    y_p = _combine(pos_p, x1_p, route_p, y_sorted)
    y_s = _combine(pos_s, x1_s, route_s, y_sorted)

    new_k_p = jnp.transpose(kpt.reshape(bp, n_hc, HEAD_DIM, sp), (0, 3, 1, 2))
    new_pool_p = up.reshape(bp, sp, POOL_WIDTH)[:, sp - POOL_HIST:]
    new_pool_s = jnp.concatenate([state_pool[l], us.reshape(bs, ts, POOL_WIDTH)], axis=1)[:, -POOL_HIST:]
    return (y_p.reshape(bp, sp, D_MODEL),
            y_s.reshape(bs, ts, D_MODEL),
            new_k_p[None],
            vp.reshape(1, bp, sp, N_HEADS, V_HEAD_DIM),
            new_pool_p[None],
            ks.reshape(1, bs, ts, n_hc, HEAD_DIM),
            vs.reshape(1, bs, ts, N_HEADS, V_HEAD_DIM),
            new_pool_s[None])
```

```python
import functools
import math

import jax
import jax.numpy as jnp
from jax import lax
from jax.experimental import pallas as pl
from jax.experimental.pallas import tpu as pltpu

_F32 = jnp.float32
_BF16 = jnp.bfloat16
_I32 = jnp.int32

D_MODEL = 2048
ATTN_WIDTH = 1024
POOL_WIDTH = 1024
HEAD_DIM = 64
N_HEADS = 8
V_HEAD_DIM = 128
POOL_WINDOWS = (2, 4, 8, 16)
POOL_GROUP_WIDTH = 256
POOL_HIST = 15
HALO_ROWS = 16
N_EXPERTS = 32
EXPERTS_PER_GROUP = 8
N_EXPERT_GROUPS = 4
D_EXPERT = 256
ROPE_THETA = 10000.0
RMS_EPS = 1e-6
NEG_INF = -1e30
PAGE_SIZE = 128
LANES = 128
ROUTE_LANES = 128
REC_ROWS = 8
VMEM_LIMIT = 56 * 1024 * 1024

PROJ_TILE = 512
ATTN_TILE = 512
ATTN_WIDE = 4
MIX_TILE = 512
EXPERT_TILE = 256
QROWS = 64
DMA_UNROLL = 8


def _dot(a, b):
    return jnp.dot(a, b, preferred_element_type=_F32)


def _split_bf16(x):
    hi = x.astype(_BF16)
    lo = (x - hi.astype(_F32)).astype(_BF16)
    return hi, lo


def _params(sem):
    return pltpu.CompilerParams(dimension_semantics=sem, vmem_limit_bytes=VMEM_LIMIT)


ROW_CHUNKS = D_MODEL // LANES


def _load_token_major(ref, n):
    return jnp.concatenate([ref[pl.ds(c, n, stride=ROW_CHUNKS), :] for c in range(ROW_CHUNKS)], axis=1)


def _store_token_major(ref, x):
    n = x.shape[0]
    for c in range(ROW_CHUNKS):
        ref[pl.ds(c, n, stride=ROW_CHUNKS), :] = x[:, c * LANES:(c + 1) * LANES]


def _whole_vmem():
    return pl.BlockSpec(memory_space=pltpu.VMEM)


def _proj_kernel(x_ref, g1_ref, w_ref, gq_ref, gk_ref, cos_ref, sin_ref, esum_ref, eexp_ref,
                 q_ref, kf_ref, kb_ref, vf_ref, vb_ref, u_ref, *, k_transposed):
    x = x_ref[...]
    ms = jnp.mean(x * x, axis=-1, keepdims=True)
    h = (x * lax.rsqrt(ms + RMS_EPS) * g1_ref[...]).astype(_BF16)
    cos = cos_ref[...]
    sin = sin_ref[...]
    lane = lax.broadcasted_iota(_I32, (1, LANES), 1)
    upper = (lane & (HEAD_DIM - 1)) >= HEAD_DIM // 2

    def normed_rope(z, g_ref, outs, transposed):
        msq = _dot((z * z).astype(_BF16), esum_ref[...])
        rh, rl = _split_bf16(lax.rsqrt(msq + RMS_EPS))
        rb = _dot(jnp.concatenate([rh, rl], axis=1), eexp_ref[...])
        n = z * rb * g_ref[...]
        for j in range(ATTN_WIDTH // LANES):
            sl = slice(j * LANES, (j + 1) * LANES)
            nj = n[:, sl]
            swapped = jnp.where(upper, pltpu.roll(nj, HEAD_DIM // 2, 1),
                                pltpu.roll(nj, LANES - HEAD_DIM // 2, 1))
            o = nj * cos + swapped * sin
            if transposed:
                ot = o.T
                for ref, _ in outs:
                    ref[sl, :] = ot.astype(ref.dtype)
            else:
                for ref, scale in outs:
                    ref[:, sl] = (o * scale).astype(ref.dtype)

    a = ATTN_WIDTH
    normed_rope(_dot(h, w_ref[:, 0:a]), gq_ref, ((q_ref, HEAD_DIM ** -0.5),), False)
    normed_rope(_dot(h, w_ref[:, a:2 * a]), gk_ref, ((kf_ref, 1.0), (kb_ref, 1.0)), k_transposed)
    zv = _dot(h, w_ref[:, 2 * a:3 * a])
    vf_ref[...] = zv
    vb_ref[...] = zv.astype(_BF16)
    u_ref[...] = _dot(h, w_ref[:, 3 * a:])


def _proj(x, g1, w_bf, gq, gk, cos, sin, esum, eexp, tm, seq_len=None):
    t = x.shape[0]
    n_pos_tiles = cos.shape[0] // tm
    row = lambda i: (i, 0)
    const = lambda i: (0, 0)
    tok_spec = lambda w: pl.BlockSpec((tm, w), row)
    a = ATTN_WIDTH
    if seq_len is None:
        k_spec, k_shape = tok_spec(a), (t, a)
    else:
        tiles_per_seq = seq_len // tm
        k_spec = pl.BlockSpec((None, a, tm), lambda i: (i // tiles_per_seq, 0, i % tiles_per_seq))
        k_shape = (t // seq_len, a, seq_len)
    return pl.pallas_call(
        functools.partial(_proj_kernel, k_transposed=seq_len is not None),
        grid=(t // tm,),
        in_specs=[
            tok_spec(D_MODEL),
            pl.BlockSpec((1, D_MODEL), const),
            _whole_vmem(),
            pl.BlockSpec((1, a), const),
            pl.BlockSpec((1, a), const),
            pl.BlockSpec((tm, LANES), lambda i: (i % n_pos_tiles, 0)),
            pl.BlockSpec((tm, LANES), lambda i: (i % n_pos_tiles, 0)),
            _whole_vmem(),
            _whole_vmem(),
        ],
        out_specs=[tok_spec(a), k_spec, k_spec, tok_spec(a), tok_spec(a), tok_spec(POOL_WIDTH)],
        out_shape=[
            jax.ShapeDtypeStruct((t, a), _BF16),
            jax.ShapeDtypeStruct(k_shape, _F32),
            jax.ShapeDtypeStruct(k_shape, _BF16),
            jax.ShapeDtypeStruct((t, a), _F32),
            jax.ShapeDtypeStruct((t, a), _BF16),
            jax.ShapeDtypeStruct((t, POOL_WIDTH), _F32),
        ],
        compiler_params=_params(("arbitrary",)),
        name="proj",
    )(x, g1, w_bf, gq, gk, cos, sin, esum, eexp)


def _lambda(lam_ref, lam_init):
    lp = lam_ref[...]
    s1 = jnp.sum(lp[0:1] * lp[1:2], axis=-1, keepdims=True)
    s2 = jnp.sum(lp[2:3] * lp[3:4], axis=-1, keepdims=True)
    return jnp.exp(s1) - jnp.exp(s2) + lam_init


def _sub_norm(o, sg, lam_init):
    ms = jnp.mean(o * o, axis=-1, keepdims=True)
    return o * lax.rsqrt(ms + RMS_EPS) * sg * (1.0 - lam_init)


def _prompt_attention(lam_ref, sg_ref, q_ref, kt_ref, v_ref, o_ref, *, tile, wide, lam_init, side):
    qi = pl.program_id(2)
    q = q_ref[...]
    lane = lax.broadcasted_iota(_I32, (1, LANES), 1)
    zero = jnp.zeros_like(q)
    qc = (jnp.where(lane < HEAD_DIM, q, zero), jnp.where(lane >= HEAD_DIM, q, zero))

    def update(s, state, vblk):
        m, l, acc = state
        m_new = jnp.maximum(m, jnp.max(s, axis=-1, keepdims=True))
        alpha = jnp.exp(m - m_new)
        p = jnp.exp(s - m_new)
        l = alpha * l + jnp.sum(p, axis=-1, keepdims=True)
        acc = alpha * acc + _dot(p.astype(_BF16), vblk)
        return m_new, l, acc

    def block(first, carry, masked, width):
        start = pl.multiple_of(first * tile, tile)
        ktblk = kt_ref[:, pl.ds(start, width)]
        vblk = v_ref[pl.ds(start, width), :]
        out = []
        for c in range(2):
            s = _dot(qc[c], ktblk)
            if masked:
                r = lax.broadcasted_iota(_I32, (tile, width), 0)
                col = lax.broadcasted_iota(_I32, (tile, width), 1)
                s = jnp.where(col <= r + (width - tile), s, NEG_INF)
            out.append(update(s, carry[c], vblk))
        return tuple(out)

    init_one = (jnp.full((tile, 1), NEG_INF, _F32), jnp.zeros((tile, 1), _F32),
                jnp.zeros((tile, V_HEAD_DIM), _F32))
    n_full = qi // wide
    carry = lax.fori_loop(0, n_full, lambda j, c: block(j * wide, c, False, wide * tile), (init_one, init_one))
    lam = _lambda(lam_ref, lam_init)
    for last in range(1, wide + 1):
        @pl.when(qi + 1 - n_full * wide == last)
        def _(last=last):
            side_finish = side()
            (_, l0, a0), (_, l1, a1) = block(n_full * wide, carry, True, last * tile)
            o = a0 / l0 - lam * (a1 / l1)
            o_ref[...] = _sub_norm(o, sg_ref[...], lam_init).astype(o_ref.dtype)
            side_finish()


def _attn_kernel(pt_ref, lam_ref, sg_ref, q_ref, kt_ref, v_ref, qs_ref, kn_ref, vn_ref, spread_ref, ck_hbm, cv_hbm,
                 o_ref, os_ref, kbuf, vbuf, sem, qbd_ref, m_ref, l_ref, acc_ref, *, tile, wide, lam_init, n_tok,
                 n_steps, pages_per_step, steps_per_elem):
    npg = pages_per_step
    step = (pl.program_id(0) * pl.num_programs(1) + pl.program_id(1)) * pl.num_programs(2) + pl.program_id(2)
    assert steps_per_elem & (steps_per_elem - 1) == 0
    part = step & (steps_per_elem - 1)

    def page_copies(s):
        elem = lax.shift_right_logical(s, steps_per_elem.bit_length() - 1)
        first_page = (s & (steps_per_elem - 1)) * npg
        base = (s & 1) * npg
        out = []
        for p in range(npg):
            page = pt_ref[elem, first_page + p]
            out.append(pltpu.make_async_copy(ck_hbm.at[page], kbuf.at[base + p], sem.at[0, base + p]))
            out.append(pltpu.make_async_copy(cv_hbm.at[page], vbuf.at[base + p], sem.at[1, base + p]))
        return out

    @pl.when(step == 0)
    def _():
        for cp in page_copies(step):
            cp.start()

    @pl.when(step + 1 < n_steps)
    def _():
        for cp in page_copies(step + 1):
            cp.start()

    def sample_pages():
        return _sample_pages_step(page_copies(step), (step & 1) * npg, part, lam_ref, sg_ref, qs_ref, kn_ref,
                                  vn_ref, spread_ref, os_ref, kbuf, vbuf, qbd_ref, m_ref, l_ref, acc_ref, npg=npg,
                                  n_tok=n_tok,
                                  lam_init=lam_init, last_part=steps_per_elem - 1)

    _prompt_attention(lam_ref, sg_ref, q_ref, kt_ref, v_ref, o_ref, tile=tile, wide=wide, lam_init=lam_init,
                      side=sample_pages)


def _sample_pages_step(copies, base, part, lam_ref, sg_ref, q_ref, kn_ref, vn_ref, spread_ref, o_ref, kbuf, vbuf,
                       qbd_ref, m_ref, l_ref, acc_ref, *, npg, n_tok, lam_init, last_part):
    for cp in copies:
        cp.wait()
    k_refs = [kbuf.at[base + p] for p in range(npg)]
    v_refs = [vbuf.at[base + p] for p in range(npg)]
    rows_per_head = 2 * n_tok
    head_rows = lambda h: slice(h * rows_per_head, (h + 1) * rows_per_head)

    @pl.when(part == 0)
    def _():
        q_tok = q_ref[...]
        lane_hc = lax.broadcasted_iota(_I32, (1, ATTN_WIDTH), 1) >> (HEAD_DIM.bit_length() - 1)
        for hc in range(2 * N_HEADS):
            qbd_ref[hc * n_tok:(hc + 1) * n_tok, :] = jnp.where(lane_hc == hc, q_tok, 0.0)
        q = qbd_ref[...]
        kn = kn_ref[...]
        vn = vn_ref[...]
        tok = lax.broadcasted_iota(_I32, (QROWS, 1), 0) & (n_tok - 1)
        ss = [jnp.where(tok >= jn, jnp.sum(q * kn[jn:jn + 1, :], axis=-1, keepdims=True), NEG_INF)
              for jn in range(n_tok)]
        m = functools.reduce(jnp.maximum, ss)
        ps = [jnp.exp(s - m) for s in ss]
        m_ref[...] = m
        l_ref[...] = functools.reduce(jnp.add, ps)
        for h in range(N_HEADS):
            lanes = slice(h * V_HEAD_DIM, (h + 1) * V_HEAD_DIM)
            acc_ref[head_rows(h), :] = functools.reduce(
                jnp.add, [ps[jn][head_rows(h)] * vn[jn:jn + 1, lanes] for jn in range(n_tok)])

    q = qbd_ref[...]
    ss = [_dot(q, kr[...]) for kr in k_refs]
    m_old = m_ref[...]
    m_new = m_old
    for s in ss:
        m_new = jnp.maximum(m_new, jnp.max(s, axis=-1, keepdims=True))
    alpha = jnp.exp(m_old - m_new)
    ps = [jnp.exp(s - m_new) for s in ss]
    l = alpha * l_ref[...] + functools.reduce(jnp.add, [jnp.sum(p, axis=-1, keepdims=True) for p in ps])
    m_ref[...] = m_new
    l_ref[...] = l
    spread = _dot(jnp.concatenate(ps, axis=0).astype(_BF16), spread_ref[...])
    row_head = lax.broadcasted_iota(_I32, (QROWS, 1), 0) >> (rows_per_head.bit_length() - 1)
    own_head = (lax.broadcasted_iota(_I32, (1, PAGE_SIZE * N_HEADS), 1) & (N_HEADS - 1)) == row_head
    pv = functools.reduce(jnp.add, [
        _dot(jnp.where(own_head, spread[p * QROWS:(p + 1) * QROWS], 0.0).astype(_BF16), vr[...].astype(_BF16))
        for p, vr in enumerate(v_refs)])
    acc_ref[...] = alpha * acc_ref[...] + pv

    def finish():
        @pl.when(part == last_part)
        def _():
            o = acc_ref[...] / l_ref[...]
            lam = _lambda(lam_ref, lam_init)
            d = o - lam * pltpu.roll(o, QROWS - n_tok, 0)
            o_ref[...] = _sub_norm(d, sg_ref[...], lam_init)

    return finish


def _attn(page_table, lam_p, sg, q, kt, v, qrep, kn, vn, cache_k, cache_v, n_tok, lam_init):
    b, s, _ = q.shape
    tile = ATTN_TILE
    nq = s // tile
    n_steps = b * N_HEADS * nq
    nb, n_pages = page_table.shape
    assert (nb * n_pages) % n_steps == 0 and n_steps % nb == 0
    npg = nb * n_pages // n_steps
    steps_per_elem = n_steps // nb
    width = ATTN_WIDTH
    assert QROWS == 2 * N_HEADS * n_tok and cache_k.shape[1:] == (width, PAGE_SIZE)
    assert cache_v.shape[1:] == (PAGE_SIZE * N_HEADS, V_HEAD_DIM)

    spread = (jnp.arange(PAGE_SIZE)[:, None] == jnp.arange(PAGE_SIZE * N_HEADS)[None, :] // N_HEADS).astype(_BF16)
    elem = lambda bi, h, i: ((bi * N_HEADS + h) * nq + i) // steps_per_elem
    per_elem = lambda rows, cols: pl.BlockSpec((None, rows, cols), lambda bi, h, i, pt: (elem(bi, h, i), 0, 0))
    kern = functools.partial(_attn_kernel, tile=tile, wide=ATTN_WIDE, lam_init=lam_init, n_tok=n_tok,
                             n_steps=n_steps, pages_per_step=npg, steps_per_elem=steps_per_elem)
    grid_spec = pltpu.PrefetchScalarGridSpec(
        num_scalar_prefetch=1,
        grid=(b, N_HEADS, nq),
        in_specs=[
            pl.BlockSpec((4, HEAD_DIM), lambda bi, h, i, pt: (0, 0)),
            pl.BlockSpec((1, V_HEAD_DIM), lambda bi, h, i, pt: (0, 0)),
            pl.BlockSpec((None, tile, LANES), lambda bi, h, i, pt: (bi, i, h)),
            pl.BlockSpec((None, LANES, s), lambda bi, h, i, pt: (bi, h, 0)),
            pl.BlockSpec((None, s, LANES), lambda bi, h, i, pt: (bi, 0, h)),
            per_elem(n_tok, width), per_elem(kn.shape[1], width), per_elem(vn.shape[1], width),
            _whole_vmem(),
            pl.BlockSpec(memory_space=pl.ANY), pl.BlockSpec(memory_space=pl.ANY),
        ],
        out_specs=[pl.BlockSpec((None, tile, LANES), lambda bi, h, i, pt: (bi, i, h)),
                   per_elem(QROWS, V_HEAD_DIM)],
        scratch_shapes=[pltpu.VMEM((2 * npg, width, PAGE_SIZE), _F32),
                        pltpu.VMEM((2 * npg, PAGE_SIZE * N_HEADS, V_HEAD_DIM), _F32),
                        pltpu.SemaphoreType.DMA((2, 2 * npg)),
                        pltpu.VMEM((QROWS, width), _F32),
                        pltpu.VMEM((QROWS, 1), _F32), pltpu.VMEM((QROWS, 1), _F32),
                        pltpu.VMEM((QROWS, V_HEAD_DIM), _F32)],
    )
    return pl.pallas_call(
        kern,
        grid_spec=grid_spec,
        out_shape=[jax.ShapeDtypeStruct((b, s, ATTN_WIDTH), _BF16),
                   jax.ShapeDtypeStruct((nb, QROWS, V_HEAD_DIM), _F32)],
        compiler_params=_params(("arbitrary", "arbitrary", "arbitrary")),
        name="attn",
    )(page_table, lam_p, sg, q, kt, v, qrep, kn, vn, spread, cache_k, cache_v)


def _pool_windows(u_ref, halo_ref, hist_ref, *, tm, tiles_per_seq, start_pos):
    i = pl.program_id(0)
    t_in_seq = i % tiles_per_seq
    u = u_ref[...]
    halo = jnp.where(t_in_seq == 0, hist_ref[...], halo_ref[...])
    ext = jnp.concatenate([halo, u], axis=0)
    pos = start_pos + t_in_seq * tm + lax.broadcasted_iota(_I32, (tm, 1), 0)
    out = []
    for g, w in enumerate(POOL_WINDOWS):
        sl = slice(g * POOL_GROUP_WIDTH, (g + 1) * POOL_GROUP_WIDTH)
        acc = ext[:, sl]
        span = 1
        while span < w:
            acc = acc + pltpu.roll(acc, span, 0)
            span *= 2
        cnt = jnp.minimum(pos + 1, w).astype(_F32)
        out.append((acc[HALO_ROWS:] / cnt - u[:, sl]).astype(_BF16))
    return out


def _pool_specs(tm, tiles_per_seq):
    halo_blocks = max(tm // HALO_ROWS, 1)
    return [
        pl.BlockSpec((tm, POOL_WIDTH), lambda i: (i, 0)),
        pl.BlockSpec((HALO_ROWS, POOL_WIDTH),
                     lambda i: (jnp.where(i % tiles_per_seq == 0, 0, i * halo_blocks - 1), 0)),
        pl.BlockSpec((None, HALO_ROWS, POOL_WIDTH), lambda i: (i // tiles_per_seq, 0, 0)),
    ]


def _poolwin_kernel(u_ref, halo_ref, hist_ref, d_ref, **pool):
    gw = POOL_GROUP_WIDTH
    for g, d in enumerate(_pool_windows(u_ref, halo_ref, hist_ref, **pool)):
        d_ref[:, g * gw:(g + 1) * gw] = d


def _poolwin(u, hist, tm, tiles_per_seq, start_pos):
    t = u.shape[0]
    kern = functools.partial(_poolwin_kernel, tm=tm, tiles_per_seq=tiles_per_seq, start_pos=start_pos)
    return pl.pallas_call(
        kern,
        grid=(t // tm,),
        in_specs=_pool_specs(tm, tiles_per_seq),
        out_specs=pl.BlockSpec((tm, POOL_WIDTH), lambda i: (i, 0)),
        out_shape=jax.ShapeDtypeStruct((t, POOL_WIDTH), _BF16),
        compiler_params=_params(("arbitrary",)),
        name="poolwin",
    )(u, u, hist)


def _mix_kernel(x_ref, a_ref, *refs, tm, pool):
    n_pool = 1 if pool is None else 3
    (wp_ref, ps_ref, wo_ref, g2_ref, wrh_ref, wrl_ref, br_ref, cin_ref,
     x1_ref, h2_ref, route_ref, rect_ref, cout_ref, carry_ref) = refs[n_pool:]
    i = pl.program_id(0)

    @pl.when(i == 0)
    def _():
        carry_ref[...] = cin_ref[...]

    gw = POOL_GROUP_WIDTH
    if pool is None:
        d = refs[0][...]
        ds = [d[:, g * gw:(g + 1) * gw] for g in range(len(POOL_WINDOWS))]
    else:
        ds = _pool_windows(*refs[:3], tm=tm, tiles_per_seq=pool[0], start_pos=pool[1])
    py = jnp.concatenate([_dot(ds[g], wp_ref[g]) for g in range(len(POOL_WINDOWS))], axis=1) * ps_ref[...]
    x1 = (x_ref[...] + _dot(a_ref[...], wo_ref[0:ATTN_WIDTH, :])
          + _dot(py.astype(_BF16), wo_ref[ATTN_WIDTH:, :]))
    x1_ref[...] = x1
    ms = jnp.mean(x1 * x1, axis=-1, keepdims=True)
    h2 = x1 * lax.rsqrt(ms + RMS_EPS) * g2_ref[...]
    _store_token_major(h2_ref, h2)

    hh, hl = _split_bf16(h2)
    both = _dot(hh, wrl_ref[...])
    logits = both[:, :ROUTE_LANES] + both[:, ROUTE_LANES:] + _dot(hl, wrh_ref[...]) + br_ref[...]
    lane = lax.broadcasted_iota(_I32, (1, ROUTE_LANES), 1)
    big = jnp.int32(ROUTE_LANES)
    is_coarse = (lane >= N_EXPERTS) & (lane < N_EXPERTS + N_EXPERT_GROUPS)
    lc = jnp.where(is_coarse, logits, NEG_INF)
    mc = jnp.max(lc, axis=-1, keepdims=True)
    g_idx = jnp.min(jnp.where(lc == mc, lane, big), axis=-1, keepdims=True) - N_EXPERTS
    p_g = 1.0 / jnp.sum(jnp.exp(lc - mc), axis=-1, keepdims=True)
    in_group = (lane < N_EXPERTS) & ((lane >> (EXPERTS_PER_GROUP.bit_length() - 1)) == g_idx)
    lf = jnp.where(in_group, logits, NEG_INF)
    v1 = jnp.max(lf, axis=-1, keepdims=True)
    i1 = jnp.min(jnp.where(lf == v1, lane, big), axis=-1, keepdims=True)
    lf2 = jnp.where(lane == i1, NEG_INF, lf)
    v2 = jnp.max(lf2, axis=-1, keepdims=True)
    i2 = jnp.min(jnp.where(lf2 == v2, lane, big), axis=-1, keepdims=True)
    e21 = jnp.exp(v2 - v1)
    w1 = p_g / (1.0 + e21)
    w2 = p_g * e21 / (1.0 + e21)

    oh1 = lane == i1
    oh2 = lane == i2
    onehot = (oh1 | oh2).astype(_F32)
    r = lax.broadcasted_iota(_I32, (tm, tm), 0)
    col = lax.broadcasted_iota(_I32, (tm, tm), 1)
    lower = (col < r).astype(_BF16)
    before = _dot(lower, onehot.astype(_BF16)) + carry_ref[...]
    rank1 = jnp.sum(jnp.where(oh1, before, 0.0), axis=-1, keepdims=True)
    rank2 = jnp.sum(jnp.where(oh2, before, 0.0), axis=-1, keepdims=True)
    carry = carry_ref[...] + jnp.sum(onehot, axis=0, keepdims=True)
    carry_ref[...] = carry
    cout_ref[...] = carry

    rec = jnp.zeros((tm, ROUTE_LANES), _F32)
    for k, val in enumerate((i1.astype(_F32), i2.astype(_F32), w1, w2, rank1, rank2)):
        rec = jnp.where(lane == k, val, rec)
    route_ref[...] = rec
    rect_ref[...] = rec.T[0:REC_ROWS, :].astype(_I32)


def _mix(x, a, pool_in, wp_bf, ps, wo_bf, g2, wrh, wrl, br, counts_in, tm, pool=None):
    t = x.shape[0]
    row = lambda i: (i, 0)
    const = lambda i: (0, 0)
    kern = functools.partial(_mix_kernel, tm=tm, pool=pool)
    if pool is None:
        pool_specs, pool_args = [pl.BlockSpec((tm, POOL_WIDTH), row)], list(pool_in)
    else:
        pool_specs, pool_args = _pool_specs(tm, pool[0]), [pool_in[0], pool_in[0], pool_in[1]]
    return pl.pallas_call(
        kern,
        grid=(t // tm,),
        in_specs=[
            pl.BlockSpec((tm, D_MODEL), row),
            pl.BlockSpec((tm, ATTN_WIDTH), row),
            *pool_specs,
            _whole_vmem(),
            pl.BlockSpec((1, POOL_WIDTH), const),
            _whole_vmem(),
            pl.BlockSpec((1, D_MODEL), const),
            _whole_vmem(),
            _whole_vmem(),
            pl.BlockSpec((1, ROUTE_LANES), const),
            pl.BlockSpec((1, ROUTE_LANES), const),
        ],
        out_specs=[
            pl.BlockSpec((tm, D_MODEL), row),
            pl.BlockSpec((tm * ROW_CHUNKS, LANES), row),
            pl.BlockSpec((tm, ROUTE_LANES), row),
            pl.BlockSpec((None, REC_ROWS, tm), lambda i: (i, 0, 0)),
            pl.BlockSpec((1, ROUTE_LANES), const),
        ],
        out_shape=[
            jax.ShapeDtypeStruct((t, D_MODEL), _F32),
            jax.ShapeDtypeStruct((t * ROW_CHUNKS, LANES), _F32),
            jax.ShapeDtypeStruct((t, ROUTE_LANES), _F32),
            jax.ShapeDtypeStruct((t // tm, REC_ROWS, tm), _I32),
            jax.ShapeDtypeStruct((1, ROUTE_LANES), _F32),
        ],
        scratch_shapes=[pltpu.VMEM((1, ROUTE_LANES), _F32)],
        compiler_params=_params(("arbitrary",)),
        name="mix",
    )(x, a, *pool_args, wp_bf, ps, wo_bf, g2, wrh, wrl, br, counts_in)


def _tokens(ref, first, n):
    return ref.at[pl.ds(pl.multiple_of(first * ROW_CHUNKS, ROW_CHUNKS), n * ROW_CHUNKS), :]


def _row_copy(src, dst, src_tok, dst_tok, sem):
    return pltpu.make_async_copy(_tokens(src, src_tok, 1), _tokens(dst, dst_tok, 1), sem)


def _dispatch_kernel(zstart_ref, zpad_ref, pos_ref, h_ref, *rest, tm, zero_rows):
    if zero_rows:
        xs_ref, sem, zsem, zbuf = rest
    else:
        _, xs_ref, sem = rest
    i = pl.program_id(0)

    if zero_rows:
        def zero_fill(wait):
            def go(cp):
                cp.wait() if wait else cp.start()

            def body(e, c):
                cursor = zstart_ref[e]
                pad = zpad_ref[e]
                run = zero_rows // 2
                while run >= 1:
                    @pl.when((pad & run) != 0)
                    def _(cursor=cursor, run=run):
                        go(pltpu.make_async_copy(_tokens(zbuf, 0, run), _tokens(xs_ref, cursor, run), zsem))

                    cursor = cursor + (pad & run)
                    run //= 2
                return c

            lax.fori_loop(0, N_EXPERTS, body, 0)

            def tile(t, c):
                rows = zero_rows // 2
                go(pltpu.make_async_copy(_tokens(zbuf, 0, rows),
                                         _tokens(xs_ref, zstart_ref[N_EXPERTS] + t * rows, rows), zsem))
                return c

            lax.fori_loop(0, zpad_ref[N_EXPERTS], tile, 0)

        @pl.when(i == 0)
        def _():
            zbuf[...] = jnp.zeros_like(zbuf)
            zero_fill(False)
            zero_fill(True)

    def body(r, c):
        for k in range(2):
            _row_copy(h_ref, xs_ref, r, pos_ref[0, k, r], sem).start(priority=k)
        return c

    lax.fori_loop(0, tm, body, 0, unroll=DMA_UNROLL)
    for _ in range(2):
        pltpu.make_async_copy(h_ref, _tokens(xs_ref, 0, tm), sem).wait()


def _dispatch(zstart, zpad, pos, h2, xs, n_rows, zero_rows):
    n_tiles, _, tm = pos.shape
    kern = functools.partial(_dispatch_kernel, tm=tm, zero_rows=zero_rows)
    in_specs = [
        pl.BlockSpec((1, 2, tm), lambda i, z, p: (i, 0, 0), memory_space=pltpu.SMEM),
        pl.BlockSpec((tm * ROW_CHUNKS, LANES), lambda i, z, p: (i, 0)),
    ]
    scratch = [pltpu.SemaphoreType.DMA(())]
    args = [zstart, zpad, pos, h2]
    aliases = {}
    if zero_rows:
        scratch += [pltpu.SemaphoreType.DMA(()), pltpu.VMEM((zero_rows // 2 * ROW_CHUNKS, LANES), _F32)]
    else:
        in_specs.append(pl.BlockSpec(memory_space=pl.ANY))
        args.append(xs)
        aliases = {4: 0}
    grid_spec = pltpu.PrefetchScalarGridSpec(
        num_scalar_prefetch=2, grid=(n_tiles,), in_specs=in_specs,
        out_specs=pl.BlockSpec(memory_space=pl.ANY), scratch_shapes=scratch)
    return pl.pallas_call(
        kern,
        grid_spec=grid_spec,
        out_shape=jax.ShapeDtypeStruct((n_rows * ROW_CHUNKS, LANES), _F32),
        input_output_aliases=aliases,
        compiler_params=_params(("arbitrary",)),
        name="dispatch",
    )(*args)


def _experts_kernel(texp_ref, nused_ref, x_ref, wg_ref, wu_ref, wd_ref, y_ref, wgb, wub, wdb, *, tm):
    i = pl.program_id(0)
    new_expert = (i == 0) | (texp_ref[i] != texp_ref[jnp.maximum(i - 1, 0)])

    @pl.when(new_expert)
    def _():
        wgb[...] = wg_ref[...].astype(_BF16)
        wub[...] = wu_ref[...].astype(_BF16)
        wdb[...] = wd_ref[...].astype(_BF16)

    @pl.when(i < nused_ref[0])
    def _():
        x = _load_token_major(x_ref, tm).astype(_BF16)
        hg = _dot(x, wgb[...])
        hu = _dot(x, wub[...])
        act = (hg * jax.nn.sigmoid(hg) * hu).astype(_BF16)
        y_ref[...] = _dot(act, wdb[...])

    @pl.when(i >= nused_ref[0])
    def _():
        y_ref[...] = jnp.zeros_like(y_ref)


def _experts(tile_expert, n_used, xs, w_gate, w_up, w_down, tm):
    n_tiles = tile_expert.shape[0]
    w_in_spec = pl.BlockSpec((None, D_MODEL, D_EXPERT), lambda i, te, nu: (te[i], 0, 0))
    grid_spec = pltpu.PrefetchScalarGridSpec(
        num_scalar_prefetch=2,
        grid=(n_tiles,),
        in_specs=[
            pl.BlockSpec((tm * ROW_CHUNKS, LANES), lambda i, te, nu: (jnp.minimum(i, nu[0] - 1), 0)),
            w_in_spec, w_in_spec,
            pl.BlockSpec((None, D_EXPERT, D_MODEL), lambda i, te, nu: (te[i], 0, 0)),
        ],
        out_specs=pl.BlockSpec((tm, D_MODEL), lambda i, te, nu: (i, 0)),
        scratch_shapes=[pltpu.VMEM((D_MODEL, D_EXPERT), _BF16), pltpu.VMEM((D_MODEL, D_EXPERT), _BF16),
                        pltpu.VMEM((D_EXPERT, D_MODEL), _BF16)],
    )
    return pl.pallas_call(
        functools.partial(_experts_kernel, tm=tm),
        grid_spec=grid_spec,
        out_shape=jax.ShapeDtypeStruct((n_tiles * tm, D_MODEL), _F32),
        compiler_params=_params(("arbitrary",)),
        name="experts",
    )(tile_expert, n_used, xs, w_gate, w_up, w_down)


def _combine_kernel(pos_ref, posn_ref, x1_ref, route_ref, y_hbm, o_ref, r0, r1, sem, *, tm):
    i = pl.program_id(0)
    n = pl.num_programs(0)
    bufs = (r0, r1)

    def issue(pos, slot):
        def body(r, c):
            for k in range(2):
                pltpu.make_async_copy(y_hbm.at[pl.ds(pos[0, k, r], 1), :], bufs[k].at[slot, pl.ds(r, 1), :],
                                      sem.at[slot, k]).start(priority=k)
            return c

        lax.fori_loop(0, tm, body, 0, unroll=DMA_UNROLL)

    @pl.when(i == 0)
    def _():
        issue(pos_ref, 0)

    @pl.when(i + 1 < n)
    def _():
        issue(posn_ref, (i + 1) % 2)

    slot = i % 2
    for k in range(2):
        pltpu.make_async_copy(y_hbm.at[pl.ds(0, tm), :], bufs[k].at[slot], sem.at[slot, k]).wait()
    rec = route_ref[...]
    o_ref[...] = x1_ref[...] + rec[:, 2:3] * r0[slot] + rec[:, 3:4] * r1[slot]


def _combine(pos, x1, route, y_sorted):
    n_tiles, _, tm = pos.shape
    t = x1.shape[0]
    kern = functools.partial(_combine_kernel, tm=tm)
    pos_spec = lambda nxt: pl.BlockSpec(
        (1, 2, tm), lambda i: (jnp.minimum(i + nxt, n_tiles - 1), 0, 0), memory_space=pltpu.SMEM)
    return pl.pallas_call(
        kern,
        grid=(n_tiles,),
        in_specs=[
            pos_spec(0), pos_spec(1),
            pl.BlockSpec((tm, D_MODEL), lambda i: (i, 0)),
            pl.BlockSpec((tm, ROUTE_LANES), lambda i: (i, 0)),
            pl.BlockSpec(memory_space=pl.ANY),
        ],
        out_specs=pl.BlockSpec((tm, D_MODEL), lambda i: (i, 0)),
        scratch_shapes=[pltpu.VMEM((2, tm, D_MODEL), _F32), pltpu.VMEM((2, tm, D_MODEL), _F32),
                        pltpu.SemaphoreType.DMA((2, 2))],
        out_shape=jax.ShapeDtypeStruct((t, D_MODEL), _F32),
        compiler_params=_params(("arbitrary",)),
        name="combine",
    )(pos, pos, x1, route, y_sorted)


def _rope_tables(pos):
    inv = ROPE_THETA ** (-jnp.arange(0, HEAD_DIM, 2, dtype=_F32) / HEAD_DIM)
    ang = pos.astype(_F32)[:, None] * inv[None, :]
    cos = jnp.cos(ang)
    sin = jnp.sin(ang)
    return jnp.tile(cos, (1, 4)), jnp.tile(jnp.concatenate([-sin, sin], axis=1), (1, 2))


def _head_sum_matrices():
    lane_head = jnp.arange(ATTN_WIDTH) // HEAD_DIM
    cols = jnp.arange(LANES)
    esum = (lane_head[:, None] == cols[None, :]).astype(_F32) / HEAD_DIM
    eexp = (cols[:, None] == lane_head[None, :]).astype(_F32)
    return esum.astype(_BF16), jnp.concatenate([eexp, eexp], axis=0).astype(_BF16)


def kernel(x_prompt, x_sample, cache_k, cache_v, state_pool, page_table, norm1_g, w_in, q_norm_g, k_norm_g,
           lambda_q1, lambda_k1, lambda_q2, lambda_k2, subln_g, w_pool, pool_scale, w_out, norm2_g, w_coarse,
           b_coarse, w_fine, b_fine, w_gate, w_up, w_down):
    depth = w_in.shape[0]
    assert depth == 1
    l = 0
    lam_init = 0.8 - 0.6 * math.exp(-0.3 * l)
    bp, sp, _ = x_prompt.shape
    bs, ts, _ = x_sample.shape
    tp, tsamp = bp * sp, bs * ts
    n_pages = page_table.shape[1]
    past = n_pages * PAGE_SIZE
    n_hc = 2 * N_HEADS

    w_in_bf = w_in[l].astype(_BF16)
    w_out_bf = w_out[l].astype(_BF16)
    w_pool_bf = w_pool[l].astype(_BF16)
    g1 = norm1_g[l][None, :]
    g2 = norm2_g[l][None, :]
    gq = jnp.tile(q_norm_g[l], n_hc)[None, :]
    gk = jnp.tile(k_norm_g[l], n_hc)[None, :]
    sg = subln_g[l][None, :]
    ps = pool_scale[l][None, :]
    lam_p = jnp.stack([lambda_q1[l], lambda_k1[l], lambda_q2[l], lambda_k2[l]])
    esum, eexp = _head_sum_matrices()
    pad = ROUTE_LANES - N_EXPERTS - N_EXPERT_GROUPS
    w_fine_flat = jnp.transpose(w_fine[l], (1, 0, 2)).reshape(D_MODEL, N_EXPERTS)
    w_route = jnp.concatenate([w_fine_flat, w_coarse[l], jnp.zeros((D_MODEL, pad), _F32)], axis=1)
    wrh, wrl = _split_bf16(w_route)
    wrl = jnp.concatenate([wrh, wrl], axis=1)
    b_route = jnp.concatenate([b_fine[l].reshape(-1), b_coarse[l], jnp.zeros((pad,), _F32)])[None, :]

    cos_p, sin_p = _rope_tables(jnp.arange(sp, dtype=jnp.int32))
    cos_s, sin_s = _rope_tables(past + jnp.arange(ts, dtype=jnp.int32))
    cos_s, sin_s = jnp.tile(cos_s, (bs, 1)), jnp.tile(sin_s, (bs, 1))
    xp2 = x_prompt.reshape(tp, D_MODEL)
    xs2 = x_sample.reshape(tsamp, D_MODEL)
    qp, kpt, kpt_bf, vp, vp_bf, up = _proj(xp2, g1, w_in_bf, gq, gk, cos_p, sin_p, esum, eexp, PROJ_TILE, sp)
    qs, ks, _, vs, _, us = _proj(xs2, g1, w_in_bf, gq, gk, cos_s, sin_s, esum, eexp, tsamp)

    qrep = qs.astype(_F32).reshape(bs, ts, ATTN_WIDTH)
    new_rows = 8
    kn = jnp.pad(ks.reshape(bs, ts, ATTN_WIDTH), ((0, 0), (0, new_rows - ts), (0, 0)))
    vn = jnp.pad(vs.reshape(bs, ts, ATTN_WIDTH), ((0, 0), (0, new_rows - ts), (0, 0)))
    n_phys = cache_k.shape[1]
    ck = jnp.transpose(cache_k[l], (0, 2, 3, 1)).reshape(n_phys, ATTN_WIDTH, PAGE_SIZE)
    cv = cache_v[l].reshape(n_phys, PAGE_SIZE * N_HEADS, V_HEAD_DIM)
    shp = (bp, sp, ATTN_WIDTH)
    a_p, a_s = _attn(page_table, lam_p, sg, qp.reshape(shp), kpt_bf, vp_bf.reshape(shp), qrep, kn, vn, ck, cv,
                     ts, lam_init)
    a_p = a_p.reshape(tp, ATTN_WIDTH)
    a_s = a_s.reshape(bs, N_HEADS, 2, ts, V_HEAD_DIM)[:, :, 0]
    a_s = a_s.transpose(0, 2, 1, 3).reshape(tsamp, ATTN_WIDTH).astype(_BF16)

    hist_p = jnp.zeros((bp, HALO_ROWS, POOL_WIDTH), _F32)
    seq_rows = 8
    us_pad = jnp.pad(us.reshape(bs, ts, POOL_WIDTH), ((0, 0), (0, seq_rows - ts), (0, 0)))
    hist_s = jnp.pad(state_pool[l], ((0, 0), (HALO_ROWS - POOL_HIST, 0), (0, 0)))
    d_s = _poolwin(us_pad.reshape(bs * seq_rows, POOL_WIDTH), hist_s, seq_rows, 1, past)
    d_s = d_s.reshape(bs, seq_rows, POOL_WIDTH)[:, :ts].reshape(tsamp, POOL_WIDTH)

    counts0 = jnp.zeros((1, ROUTE_LANES), _F32)
    x1_p, h2_p, route_p, rect_p, counts_p = _mix(xp2, a_p, (up, hist_p), w_pool_bf, ps, w_out_bf, g2, wrh, wrl,
                                                 b_route, counts0, MIX_TILE, pool=(sp // MIX_TILE, 0))
    x1_s, h2_s, route_s, rect_s, counts = _mix(xs2, a_s, (d_s,), w_pool_bf, ps, w_out_bf, g2, wrh, wrl, b_route,
                                               counts_p, tsamp)

    tm = EXPERT_TILE
    t_all = tp + tsamp
    n_tiles = (2 * t_all + N_EXPERTS * (tm - 1) + tm - 1) // tm
    cnt = counts[0, :N_EXPERTS].astype(jnp.int32)
    padded = (cnt + tm - 1) // tm * tm
    ends = jnp.cumsum(padded)
    off = ends - padded
    n_used = (ends[-1] // tm).astype(jnp.int32)
    tile_ids = jnp.arange(n_tiles, dtype=jnp.int32)
    te = jnp.sum((ends[None, :] <= (tile_ids * tm)[:, None]).astype(jnp.int32), axis=1)
    te = jnp.minimum(te, N_EXPERTS - 1)
    last = jnp.sum(jnp.where(tile_ids == n_used - 1, te, 0))
    tile_expert = jnp.where(tile_ids < n_used, te, last)

    n_rows = n_tiles * tm
    zstart = off + counts_p[0, :N_EXPERTS].astype(jnp.int32)
    zinfo = (jnp.concatenate([zstart, ends[-1:]]), jnp.concatenate([ends - zstart, (n_tiles - n_used)[None]]))
    assert tsamp <= tm

    def sorted_rows(rect):
        experts = jnp.arange(N_EXPERTS, dtype=jnp.int32)
        base = jnp.sum(jnp.where(rect[:, 0:2, :, None] == experts, off, 0), axis=-1)
        return base + rect[:, 4:6, :]

    pos_p, pos_s = sorted_rows(rect_p), sorted_rows(rect_s)
    xs = _dispatch(*zinfo, pos_p, h2_p, None, n_rows, 2 * tm)
    xs = _dispatch(*zinfo, pos_s, h2_s, xs, n_rows, 0)
    y_sorted = _experts(tile_expert, n_used.reshape(1), xs, w_gate[l], w_up[l], w_down[l], tm)
    y_p = _combine(pos_p, x1_p, route_p, y_sorted)
    y_s = _combine(pos_s, x1_s, route_s, y_sorted)

    new_k_p = jnp.transpose(kpt.reshape(bp, n_hc, HEAD_DIM, sp), (0, 3, 1, 2))
    new_pool_p = up.reshape(bp, sp, POOL_WIDTH)[:, sp - POOL_HIST:]
    new_pool_s = jnp.concatenate([state_pool[l], us.reshape(bs, ts, POOL_WIDTH)], axis=1)[:, -POOL_HIST:]
    return (y_p.reshape(bp, sp, D_MODEL),
            y_s.reshape(bs, ts, D_MODEL),
            new_k_p[None],
            vp.reshape(1, bp, sp, N_HEADS, V_HEAD_DIM),
            new_pool_p[None],
            ks.reshape(1, bs, ts, n_hc, HEAD_DIM),
            vs.reshape(1, bs, ts, N_HEADS, V_HEAD_DIM),
            new_pool_s[None])
```
